```python
import math
import jax, jax.numpy as jnp
from jax import lax
import numpy as np

D_MODEL = 1024
BATCH = 2
SEQ = 8192
DEPTH = 2

CTX_LEN = 256
GRID_W = 64
N_MOD = 9
D_MIX = D_MODEL
W_A = D_MIX // 2
W_B = D_MIX // 4
W_C = D_MIX - W_A - W_B
D_IN = 2 * W_A + W_B + 3 * W_C
RG_HEADS = 8
RG_HD = W_A // RG_HEADS
RG_CONV = 4
RG_C = 8.0
POOL_WINDOWS = (2, 4, 8, 16)
POOL_GROUPS = len(POOL_WINDOWS)
POOL_GW = W_B // POOL_GROUPS
HY_ORDER = 2
HY_SHORT = 3
HY_BANDS = 16
HY_EMB = 2 * HY_BANDS + 1
HY_FH = 64
HY_TARGET = 1e-2
HY_FAST = 0.3
HY_SLOW = 1.5
D_FF = 256 * math.ceil(8 * D_MODEL / 3 / 256)
EPS = 1e-6

kernel_name = 'hybrid_lru_pool_hyena_prefix_dit'


def rmsnorm(x, g):
    xf = x.astype(jnp.float32)
    y = xf * lax.rsqrt(jnp.mean(xf * xf, axis=-1, keepdims=True) + EPS)
    return (y * g.astype(jnp.float32)).astype(x.dtype)


def adanorm(x, g, shift, scale):
    return rmsnorm(x, g) * (1 + scale) + shift


def swiglu(h, w_in, w_out):
    gate, up = jnp.split(h @ w_in, 2, axis=-1)
    return (jax.nn.silu(gate) * up) @ w_out


def dwconv(u, w, b, pad_left):
    k_w = w.shape[0]
    n = u.shape[1]
    up = jnp.pad(u, ((0, 0), (pad_left, k_w - 1 - pad_left), (0, 0)))
    out = b + w[0] * up[:, 0:n]
    for k in range(1, k_w):
        out = out + w[k] * up[:, k:k + n]
    return out


def _affine_combine(e1, e2):
    a1, b1 = e1
    a2, b2 = e2
    return a1 * a2, a2 * b1 + b2


def rglru_coeffs(u, w_g, b_g, lam):
    bn, n, w = u.shape
    pre = jnp.einsum('blhc,ghcd->gblhd', u.reshape(bn, n, RG_HEADS, RG_HD), w_g.astype(jnp.float32))
    pre = pre.reshape(2, bn, n, w) + b_g.astype(jnp.float32)[:, None, None, :]
    r = jax.nn.sigmoid(pre[0])
    i = jax.nn.sigmoid(pre[1])
    log_a = -RG_C * r * jax.nn.softplus(-lam.astype(jnp.float32))
    return jnp.exp(log_a), jnp.sqrt(-jnp.expm1(2.0 * log_a)) * (i * u)


def rglru_scans(xa, conv_w, conv_b, gate_w, gate_b, lam, h0_f, h0_b):
    u = dwconv(xa, conv_w, conv_b, RG_CONV // 2).astype(jnp.float32)
    states = []
    for d, (h0, rev) in enumerate(((h0_f, False), (h0_b, True))):
        a, bx = rglru_coeffs(u, gate_w[d], gate_b[d], lam[d])
        a_cum, b_cum = lax.associative_scan(_affine_combine, (a, bx), reverse=rev, axis=1)
        states.append(a_cum * h0[:, None, :] + b_cum)
    return states[0], states[1]


def multi_pool(u):
    n = u.shape[-2]
    uf = u.astype(jnp.float32)
    csum = jnp.concatenate([jnp.zeros_like(uf[..., :1, :]), jnp.cumsum(uf, axis=-2)], axis=-2)
    pos = jnp.arange(n)
    outs = []
    for g, win in enumerate(POOL_WINDOWS):
        lo = jnp.clip(pos - win // 2, 0, n)
        hi = jnp.clip(pos + win - win // 2, 0, n)
        cg = csum[..., g * POOL_GW:(g + 1) * POOL_GW]
        tot = jnp.take(cg, hi, axis=-2) - jnp.take(cg, lo, axis=-2)
        outs.append(tot / (hi - lo).astype(jnp.float32)[:, None])
    return jnp.concatenate(outs, axis=-1) - uf


def pool_branch(up, pool_w, pool_b, pool_scale, rows):
    bn, n, w = up.shape
    if rows is None:
        pooled = multi_pool(up)
    else:
        pooled = multi_pool(up.reshape(bn, rows, GRID_W, w)).reshape(bn, n, w)
    y = jnp.einsum('blgc,gcd->blgd', pooled.reshape(bn, n, POOL_GROUPS, POOL_GW), pool_w.astype(jnp.float32))
    return (y.reshape(bn, n, w) + pool_b) * pool_scale


def hyena_filter_fft(n, w1, b1, freq, w2, b2, w3):
    f32 = jnp.float32
    t = jnp.linspace(0.0, 1.0, n, dtype=f32)[:, None]
    wpos = 2.0 * math.pi * jnp.arange(n, dtype=f32)[:, None] / n
    bands = jnp.linspace(1e-4, HY_BANDS - 1, HY_BANDS, dtype=f32)[None, :]
    feats = jnp.concatenate([t, jnp.cos(bands * wpos), -jnp.sin(bands * wpos)], axis=-1)
    fr = freq.astype(f32)
    hid = jnp.sin(fr * (feats @ w1.astype(f32) + b1.astype(f32)))
    hid = jnp.sin(fr * (hid @ w2.astype(f32) + b2.astype(f32)))
    filt = (hid @ w3.astype(f32)).reshape(n, HY_ORDER, 2, W_C)
    deltas = jnp.abs(jnp.linspace(math.log(HY_TARGET) / HY_FAST, math.log(HY_TARGET) / HY_SLOW, W_C, dtype=f32))
    filt = filt * jnp.exp(-t[:, :, None, None] * deltas)
    fwd, bwd = filt[:, :, 0], filt[:, :, 1]
    circ = jnp.concatenate([fwd, jnp.zeros((1, HY_ORDER, W_C), f32), bwd[1:][::-1]], axis=0)
    return jnp.fft.rfft(circ, axis=0)


def fftconv(u, filt_f):
    n = u.shape[1]
    spec = jnp.fft.rfft(u, n=2 * n, axis=1) * filt_f
    return jnp.fft.irfft(spec, n=2 * n, axis=1)[:, :n]


def hyena_branch(uh, conv_w, conv_b, skip, filt_f):
    z = dwconv(uh, conv_w, conv_b, HY_SHORT // 2).astype(jnp.float32)
    v, x1, x2 = jnp.split(z, 3, axis=-1)
    sk = skip.astype(jnp.float32)
    y = x1 * (fftconv(v, filt_f[:, 0]) + sk[0] * v)
    return x2 * (fftconv(y, filt_f[:, 1]) + sk[1] * y)


def merge_heads(proj, hf, hb, w_out, pool, hy, filt_f, rows):
    ga = proj[..., W_A:2 * W_A]
    up = proj[..., 2 * W_A:2 * W_A + W_B]
    uh = proj[..., 2 * W_A + W_B:]
    y_a = (hf + hb) * jax.nn.gelu(ga.astype(jnp.float32))
    y_b = pool_branch(up, *pool, rows)
    y_c = hyena_branch(uh, *hy, filt_f)
    return jnp.concatenate([y_a, y_b, y_c], axis=-1).astype(proj.dtype) @ w_out


def setup_inputs(seed: int = 0) -> dict:
    key = jax.random.key(seed)
    ks = jax.random.split(key, 32)
    f32 = jnp.float32

    def nrm(k, shape, scale):
        return jax.random.normal(k, shape, f32) * scale

    u_lam = jax.random.uniform(ks[15], (DEPTH, 2, W_A), f32, 0.9, 0.999)
    a0 = u_lam ** (1.0 / RG_C)
    return {
        'x': nrm(ks[0], (BATCH, SEQ, D_MODEL), 1.0),
        'c': nrm(ks[1], (BATCH, D_MODEL), 1.0),
        'ctx': nrm(ks[2], (BATCH, CTX_LEN, D_MODEL), 1.0),
        'c_ctx': nrm(ks[3], (D_MODEL,), 1.0),
        'w_mod': nrm(ks[4], (DEPTH, D_MODEL, N_MOD * D_MODEL), 0.5 * D_MODEL ** -0.5),
        'b_mod': nrm(ks[5], (DEPTH, N_MOD * D_MODEL), 0.02),
        'norm_g': 1.0 + nrm(ks[6], (DEPTH, 3, D_MODEL), 0.05),
        'ffn_w_in': nrm(ks[7], (DEPTH, 2, D_MODEL, 2 * D_FF), D_MODEL ** -0.5),
        'ffn_w_out': nrm(ks[8], (DEPTH, 2, D_FF, D_MODEL), D_FF ** -0.5),
        'w_in': nrm(ks[9], (DEPTH, D_MODEL, D_IN), D_MODEL ** -0.5),
        'w_out': nrm(ks[10], (DEPTH, D_MIX, D_MODEL), D_MIX ** -0.5),
        'lru_conv_w': nrm(ks[11], (DEPTH, RG_CONV, W_A), RG_CONV ** -0.5),
        'lru_conv_b': nrm(ks[12], (DEPTH, W_A), 0.02),
        'lru_gate_w': nrm(ks[13], (DEPTH, 2, 2, RG_HEADS, RG_HD, RG_HD), RG_HD ** -0.5),
        'lru_gate_b': nrm(ks[14], (DEPTH, 2, 2, W_A), 0.02),
        'lru_lambda': jnp.log(a0) - jnp.log1p(-a0),
        'pool_w': nrm(ks[16], (DEPTH, POOL_GROUPS, POOL_GW, POOL_GW), POOL_GW ** -0.5),
        'pool_b': nrm(ks[17], (DEPTH, W_B), 0.02),
        'pool_scale': 1.0 + nrm(ks[18], (DEPTH, W_B), 0.05),
        'hy_conv_w': nrm(ks[19], (DEPTH, HY_SHORT, 3 * W_C), HY_SHORT ** -0.5),
        'hy_conv_b': nrm(ks[20], (DEPTH, 3 * W_C), 0.02),
        'hy_w1': nrm(ks[21], (DEPTH, HY_EMB, HY_FH), HY_EMB ** -0.5),
        'hy_b1': nrm(ks[22], (DEPTH, HY_FH), 0.02),
        'hy_freq': 1.0 + nrm(ks[23], (DEPTH, HY_FH), 0.05),
        'hy_w2': nrm(ks[24], (DEPTH, HY_FH, HY_FH), HY_FH ** -0.5),
        'hy_b2': nrm(ks[25], (DEPTH, HY_FH), 0.02),
        'hy_w3': nrm(ks[26], (DEPTH, HY_FH, HY_ORDER * 2 * W_C), 0.05 * HY_FH ** -0.5),
        'hy_skip': nrm(ks[27], (DEPTH, HY_ORDER, W_C), 0.5),
        'final_g': 1.0 + nrm(ks[28], (D_MODEL,), 0.05),
    }


def reference(x, c, ctx, c_ctx, w_mod, b_mod, norm_g, ffn_w_in, ffn_w_out, w_in, w_out,
              lru_conv_w, lru_conv_b, lru_gate_w, lru_gate_b, lru_lambda,
              pool_w, pool_b, pool_scale, hy_conv_w, hy_conv_b,
              hy_w1, hy_b1, hy_freq, hy_w2, hy_b2, hy_w3, hy_skip, final_g):
    bn, n, _ = x.shape
    n_ctx = ctx.shape[1]
    rows = n // GRID_W
    zeros = jnp.zeros((bn, W_A), jnp.float32)
    xc = ctx
    for l in range(DEPTH):
        last = l == DEPTH - 1
        m_x = (jax.nn.silu(c) @ w_mod[l] + b_mod[l]).reshape(bn, N_MOD, 1, D_MODEL)
        m_c = (jax.nn.silu(c_ctx) @ w_mod[l] + b_mod[l]).reshape(N_MOD, D_MODEL)
        mx = [m_x[:, j] for j in range(N_MOD)]
        mc = [m_c[j] for j in range(N_MOD)]
        lru = (lru_conv_w[l], lru_conv_b[l], lru_gate_w[l], lru_gate_b[l], lru_lambda[l])
        pool = (pool_w[l], pool_b[l], pool_scale[l])
        hy = (hy_conv_w[l], hy_conv_b[l], hy_skip[l])
        filt_args = (hy_w1[l], hy_b1[l], hy_freq[l], hy_w2[l], hy_b2[l], hy_w3[l])

        x = x + 0.5 * mx[2] * swiglu(adanorm(x, norm_g[l, 0], mx[0], mx[1]), ffn_w_in[l, 0], ffn_w_out[l, 0])
        xc = xc + 0.5 * mc[2] * swiglu(adanorm(xc, norm_g[l, 0], mc[0], mc[1]), ffn_w_in[l, 0], ffn_w_out[l, 0])

        hc = adanorm(xc, norm_g[l, 1], mc[3], mc[4])
        if last:
            hf_c, hb_c = rglru_scans(hc @ w_in[l][:, :W_A], *lru, zeros, zeros)
        else:
            proj_c = hc @ w_in[l]
            hf_c, hb_c = rglru_scans(proj_c[..., :W_A], *lru, zeros, zeros)
            y_ctx = merge_heads(proj_c, hf_c, hb_c, w_out[l], pool, hy,
                                hyena_filter_fft(n_ctx, *filt_args), None)
            xc = xc + mc[5] * y_ctx

        h = adanorm(x, norm_g[l, 1], mx[3], mx[4])
        proj = h @ w_in[l]
        hf, hb = rglru_scans(proj[..., :W_A], *lru, hf_c[:, -1], hb_c[:, 0])
        y = merge_heads(proj, hf, hb, w_out[l], pool, hy, hyena_filter_fft(n, *filt_args), rows)
        x = x + mx[5] * y

        x = x + 0.5 * mx[8] * swiglu(adanorm(x, norm_g[l, 2], mx[6], mx[7]), ffn_w_in[l, 1], ffn_w_out[l, 1])
        if not last:
            xc = xc + 0.5 * mc[8] * swiglu(adanorm(xc, norm_g[l, 2], mc[6], mc[7]), ffn_w_in[l, 1], ffn_w_out[l, 1])
    return rmsnorm(x, final_g)
```

```python
import functools
import math

import numpy as np
import jax
import jax.numpy as jnp
from jax import lax
from jax.experimental import pallas as pl
from jax.experimental.pallas import tpu as pltpu

F32 = jnp.float32
BF16 = jnp.bfloat16

D_MODEL = 1024
GRID_W = 64
N_MOD = 9
W_A, W_B, W_C = 512, 256, 256
D_IN = 2 * W_A + W_B + 3 * W_C
RG_HEADS, RG_HD, RG_CONV, RG_C = 8, 64, 4, 8.0
POOL_WINDOWS = (2, 4, 8, 16)
POOL_GW = W_B // len(POOL_WINDOWS)
HY_ORDER, HY_SHORT, HY_BANDS, HY_FH = 2, 3, 16, 64
HY_EMB = 2 * HY_BANDS + 1
HY_EMB_PAD = 40
HY_TARGET, HY_FAST, HY_SLOW = 1e-2, 0.3, 1.5
D_FF = 2816
EPS = 1e-6

V7X_LANES = 128
V7X_SUBLANES = 8
V7X_MXU_DIM = 256
V7X_VMEM_LIMIT_BYTES = 56 * 1024 * 1024

FFT_MINOR = V7X_LANES
HALO = V7X_SUBLANES
FF_CHUNKS = ((0, 768), (768, 768), (1536, 768), (2304, 512))
POOL_TILE = 256
CTX_FFT_LEN = 1024


def _params(*sem):
    return pltpu.CompilerParams(dimension_semantics=sem, vmem_limit_bytes=V7X_VMEM_LIMIT_BYTES)


def _resident(shape, index_map):
    return pl.BlockSpec(shape, index_map, pipeline_mode=pl.Buffered(1))


def _bf16_table(values):
    return jnp.asarray(values, F32).astype(BF16)


def _adanorm(x, g, shift, scale):
    ms = jnp.mean(x * x, axis=-1, keepdims=True)
    return (x * lax.rsqrt(ms + EPS)) * g * (1.0 + scale) + shift


def _mod_kernel(c_ref, w_ref, b_ref, o_ref):
    s = c_ref[...]
    s = s * jax.nn.sigmoid(s)
    o_ref[0] = jnp.dot(s.astype(BF16), w_ref[0].astype(BF16), preferred_element_type=F32) + b_ref[0]


def _modulation(cvec, w_mod, b_mod):
    depth, _, n = w_mod.shape
    tn = 1024
    return pl.pallas_call(
        _mod_kernel,
        grid=(depth, n // tn),
        in_specs=[pl.BlockSpec((V7X_SUBLANES, D_MODEL), lambda l, j: (0, 0)),
                  pl.BlockSpec((1, D_MODEL, tn), lambda l, j: (l, 0, j)),
                  pl.BlockSpec((1, 1, tn), lambda l, j: (l, 0, j))],
        out_specs=pl.BlockSpec((1, V7X_SUBLANES, tn), lambda l, j: (l, 0, j)),
        out_shape=jax.ShapeDtypeStruct((depth, V7X_SUBLANES, n), F32),
        compiler_params=_params("arbitrary", "arbitrary"),
        name="modulation",
    )(cvec, w_mod, b_mod.reshape(depth, 1, n))


def _ffn_kernel(x_ref, m_ref, g_ref, win_ref, wout_ref, fg_ref, o_ref, *, j0, final):
    x = x_ref[0]
    shift = m_ref[0, j0:j0 + 1, :]
    scale = m_ref[0, j0 + 1:j0 + 2, :]
    gate = m_ref[0, j0 + 2:j0 + 3, :]
    h = _adanorm(x, g_ref[...], shift, scale).astype(BF16)
    acc = None
    for s, w in FF_CHUNKS:
        gt = jnp.dot(h, win_ref[:, s:s + w], preferred_element_type=F32)
        up = jnp.dot(h, win_ref[:, D_FF + s:D_FF + s + w], preferred_element_type=F32)
        a = (gt * jax.nn.sigmoid(gt) * up).astype(BF16)
        p = jnp.dot(a, wout_ref[s:s + w, :], preferred_element_type=F32)
        acc = p if acc is None else acc + p
    y = x + (0.5 * gate) * acc
    if final:
        ms = jnp.mean(y * y, axis=-1, keepdims=True)
        y = (y * lax.rsqrt(ms + EPS)) * fg_ref[...]
    o_ref[0] = y


def _ffn(x, mod, g, w_in, w_out, final_g, *, j0, final, tm):
    bn, n, _ = x.shape
    return pl.pallas_call(
        functools.partial(_ffn_kernel, j0=j0, final=final),
        grid=(bn, n // tm),
        in_specs=[pl.BlockSpec((1, tm, D_MODEL), lambda b, i: (b, i, 0)),
                  pl.BlockSpec((1, N_MOD, D_MODEL), lambda b, i: (b, 0, 0)),
                  pl.BlockSpec((1, D_MODEL), lambda b, i: (0, 0)),
                  _resident((D_MODEL, 2 * D_FF), lambda b, i: (0, 0)),
                  _resident((D_FF, D_MODEL), lambda b, i: (0, 0)),
                  pl.BlockSpec((1, D_MODEL), lambda b, i: (0, 0))],
        out_specs=pl.BlockSpec((1, tm, D_MODEL), lambda b, i: (b, i, 0)),
        out_shape=jax.ShapeDtypeStruct(x.shape, F32),
        compiler_params=_params("arbitrary", "arbitrary"),
        name="ffn",
    )(x, mod, g.reshape(1, D_MODEL), w_in, w_out, final_g.reshape(1, D_MODEL))


def _proj_kernel(xp_ref, x_ref, xn_ref, m_ref, g_ref, w_ref, cwa_ref, cba_ref, cwh_ref, cbh_ref,
                 ua_ref, ga_ref, up_ref, z_ref, *, tm, n_tiles):
    i = pl.program_id(1)
    xe = jnp.concatenate([xp_ref[0], x_ref[0], xn_ref[0]], axis=0)
    h = _adanorm(xe, g_ref[...], m_ref[0, 3:4, :], m_ref[0, 4:5, :]).astype(BF16)
    proj = jnp.dot(h, w_ref[...], preferred_element_type=F32)
    row = lax.broadcasted_iota(jnp.int32, (tm + 2 * HALO, 1), 0)
    lo_ok = jnp.where(i > 0, 0, HALO)
    hi_ok = jnp.where(i < n_tiles - 1, tm + 2 * HALO, tm + HALO)
    proj = jnp.where((row >= lo_ok) & (row < hi_ok), proj, 0.0)

    xa = proj[:, :W_A]
    ua = cba_ref[...] + cwa_ref[0:1, :] * xa[HALO - 2:HALO - 2 + tm]
    for k in range(1, RG_CONV):
        ua = ua + cwa_ref[k:k + 1, :] * xa[HALO - 2 + k:HALO - 2 + k + tm]
    ua_ref[0] = ua
    ga_ref[0] = proj[HALO:HALO + tm, W_A:2 * W_A]
    up_ref[0] = proj[HALO:HALO + tm, 2 * W_A:2 * W_A + W_B]
    uh = proj[:, 2 * W_A + W_B:]
    z = cbh_ref[...] + cwh_ref[0:1, :] * uh[HALO - 1:HALO - 1 + tm]
    for k in range(1, HY_SHORT):
        z = z + cwh_ref[k:k + 1, :] * uh[HALO - 1 + k:HALO - 1 + k + tm]
    z_ref[0] = z


def _proj(x, mod, g, w_in, cwa, cba, cwh, cbh, *, tm):
    bn, n, _ = x.shape
    n_tiles = n // tm
    hb = tm // HALO
    last_hb = n // HALO - 1
    outs = (W_A, W_A, W_B, 3 * W_C)
    return pl.pallas_call(
        functools.partial(_proj_kernel, tm=tm, n_tiles=n_tiles),
        grid=(bn, n_tiles),
        in_specs=[pl.BlockSpec((1, HALO, D_MODEL), lambda b, i: (b, jnp.maximum(i * hb - 1, 0), 0)),
                  pl.BlockSpec((1, tm, D_MODEL), lambda b, i: (b, i, 0)),
                  pl.BlockSpec((1, HALO, D_MODEL), lambda b, i: (b, jnp.minimum((i + 1) * hb, last_hb), 0)),
                  pl.BlockSpec((1, N_MOD, D_MODEL), lambda b, i: (b, 0, 0)),
                  pl.BlockSpec((1, D_MODEL), lambda b, i: (0, 0)),
                  _resident((D_MODEL, D_IN), lambda b, i: (0, 0)),
                  pl.BlockSpec((RG_CONV, W_A), lambda b, i: (0, 0)),
                  pl.BlockSpec((1, W_A), lambda b, i: (0, 0)),
                  pl.BlockSpec((HY_SHORT, 3 * W_C), lambda b, i: (0, 0)),
                  pl.BlockSpec((1, 3 * W_C), lambda b, i: (0, 0))],
        out_specs=[pl.BlockSpec((1, tm, w), lambda b, i: (b, i, 0)) for w in outs],
        out_shape=[jax.ShapeDtypeStruct((bn, n, w), F32) for w in outs],
        compiler_params=_params("arbitrary", "arbitrary"),
        name="proj",
    )(x, x, x, mod, g.reshape(1, D_MODEL), w_in, cwa, cba.reshape(1, W_A), cwh, cbh.reshape(1, 3 * W_C))


def _lru_coeffs(u, wg_ref, bg_ref, lam_ref, a_s, b_s):
    pre = jnp.dot(u.astype(BF16), wg_ref[...], preferred_element_type=F32) + bg_ref[...]
    r = jax.nn.sigmoid(pre[:, :W_A])
    ig = jax.nn.sigmoid(pre[:, W_A:])
    log_a = (-RG_C * r) * jax.nn.softplus(-lam_ref[...])
    a_s[...] = jnp.exp(log_a)
    th = jnp.tanh(log_a)
    b_s[...] = jnp.sqrt(-2.0 * th / (1.0 - th)) * (ig * u)


def _lru_scan_tile(a_s, b_s, carry, *, tm, reverse):
    groups = tm // V7X_SUBLANES
    rowi = lax.broadcasted_iota(jnp.int32, (V7X_SUBLANES, W_A), 0)

    def body(k, c):
        gi = (groups - 1 - k) if reverse else k
        r0 = pl.multiple_of(gi * V7X_SUBLANES, V7X_SUBLANES)
        a = a_s[pl.ds(r0, V7X_SUBLANES), :]
        b = b_s[pl.ds(r0, V7X_SUBLANES), :]
        for s in (1, 2, 4):
            if reverse:
                shift, m = V7X_SUBLANES - s, rowi < V7X_SUBLANES - s
            else:
                shift, m = s, rowi >= s
            a_sh = pltpu.roll(a, shift, 0)
            b_sh = pltpu.roll(b, shift, 0)
            b = jnp.where(m, a * b_sh + b, b)
            a = jnp.where(m, a * a_sh, a)
        h = a * c + b
        b_s[pl.ds(r0, V7X_SUBLANES), :] = h
        return h[0:1, :] if reverse else h[V7X_SUBLANES - 1:V7X_SUBLANES, :]

    return lax.fori_loop(0, groups, body, carry, unroll=4)


def _lru_fwd_kernel(u_ref, wg_ref, bg_ref, lam_ref, h0_ref, hf_ref, a_s, b_s, c_s, *, tm):
    @pl.when(pl.program_id(1) == 0)
    def _():
        c_s[...] = h0_ref[0]

    _lru_coeffs(u_ref[0], wg_ref, bg_ref, lam_ref, a_s, b_s)
    c_s[...] = _lru_scan_tile(a_s, b_s, c_s[...], tm=tm, reverse=False)
    hf_ref[0] = b_s[...]


def _lru_bwd_kernel(u_ref, wg_ref, bg_ref, lam_ref, h0_ref, hf_ref, ga_ref, ya_ref, hend_ref,
                    a_s, b_s, c_s, *, tm):
    @pl.when(pl.program_id(1) == 0)
    def _():
        c_s[...] = h0_ref[0]

    _lru_coeffs(u_ref[0], wg_ref, bg_ref, lam_ref, a_s, b_s)
    c_s[...] = _lru_scan_tile(a_s, b_s, c_s[...], tm=tm, reverse=True)
    hend_ref[0] = c_s[...]
    ya_ref[0] = (hf_ref[0] + b_s[...]) * jax.nn.gelu(ga_ref[0])


def _lru_common_specs(tm, tile_map):
    return [pl.BlockSpec((1, tm, W_A), tile_map),
            pl.BlockSpec((W_A, 2 * W_A), lambda b, i: (0, 0)),
            pl.BlockSpec((1, 2 * W_A), lambda b, i: (0, 0)),
            pl.BlockSpec((1, W_A), lambda b, i: (0, 0)),
            pl.BlockSpec((1, 1, W_A), lambda b, i: (b, 0, 0))]


def _lru_scratch(tm):
    return [pltpu.VMEM((tm, W_A), F32), pltpu.VMEM((tm, W_A), F32), pltpu.VMEM((1, W_A), F32)]


def _lru_fwd(u, wg, bg, lam, h0, *, tm):
    bn, n, _ = u.shape
    tile_map = lambda b, i: (b, i, 0)
    return pl.pallas_call(
        functools.partial(_lru_fwd_kernel, tm=tm),
        grid=(bn, n // tm),
        in_specs=_lru_common_specs(tm, tile_map),
        out_specs=pl.BlockSpec((1, tm, W_A), tile_map),
        out_shape=jax.ShapeDtypeStruct(u.shape, F32),
        scratch_shapes=_lru_scratch(tm),
        compiler_params=_params("arbitrary", "arbitrary"),
        name="lru_fwd",
    )(u, wg, bg, lam, h0.reshape(bn, 1, W_A))


def _lru_bwd(u, wg, bg, lam, h0, hf, ga, *, tm):
    bn, n, _ = u.shape
    last = n // tm - 1
    tile_map = lambda b, i: (b, last - i, 0)
    ya, hend = pl.pallas_call(
        functools.partial(_lru_bwd_kernel, tm=tm),
        grid=(bn, n // tm),
        in_specs=_lru_common_specs(tm, tile_map) + [pl.BlockSpec((1, tm, W_A), tile_map),
                                                    pl.BlockSpec((1, tm, W_A), tile_map)],
        out_specs=[pl.BlockSpec((1, tm, W_A), tile_map),
                   pl.BlockSpec((1, 1, W_A), lambda b, i: (b, 0, 0))],
        out_shape=[jax.ShapeDtypeStruct(u.shape, F32), jax.ShapeDtypeStruct((bn, 1, W_A), F32)],
        scratch_shapes=_lru_scratch(tm),
        compiler_params=_params("arbitrary", "arbitrary"),
        name="lru_bwd",
    )(u, wg, bg, lam, h0.reshape(bn, 1, W_A), hf, ga)
    return ya, hend.reshape(bn, W_A)


def _pool_tables(row_len):
    t = np.arange(POOL_TILE)
    p = t % row_len
    base = t - p
    band = np.zeros((len(POOL_WINDOWS), POOL_TILE, POOL_TILE), np.float32)
    inv = np.zeros((POOL_TILE, W_B), np.float32)
    for g, win in enumerate(POOL_WINDOWS):
        lo = np.clip(p - win // 2, 0, row_len)
        hi = np.clip(p + win - win // 2, 0, row_len)
        s = np.arange(POOL_TILE)[None, :]
        band[g] = ((s >= (base + lo)[:, None]) & (s < (base + hi)[:, None])).astype(np.float32)
        inv[:, g * POOL_GW:(g + 1) * POOL_GW] = (1.0 / (hi - lo).astype(np.float64))[:, None]
    return _bf16_table(band), jnp.asarray(inv, F32)


def _pool_kernel(u_ref, band_ref, inv_ref, w_ref, b_ref, s_ref, o_ref):
    x = u_ref[0]
    x_hi = x.astype(BF16)
    x_lo = (x - x_hi.astype(F32)).astype(BF16)
    col = lax.broadcasted_iota(jnp.int32, x.shape, 1) // POOL_GW
    tot = jnp.zeros_like(x)
    for g in range(len(POOL_WINDOWS)):
        sg = (jnp.dot(band_ref[g], x_hi, preferred_element_type=F32)
              + jnp.dot(band_ref[g], x_lo, preferred_element_type=F32))
        tot = jnp.where(col == g, sg, tot)
    pooled = tot * inv_ref[...] - x
    y = jnp.dot(pooled.astype(BF16), w_ref[...], preferred_element_type=F32)
    o_ref[0] = (y + b_ref[...]) * s_ref[...]


def _pool(up, band, inv, w, b, s):
    bn, n, _ = up.shape
    return pl.pallas_call(
        _pool_kernel,
        grid=(bn, n // POOL_TILE),
        in_specs=[pl.BlockSpec((1, POOL_TILE, W_B), lambda b_, i: (b_, i, 0)),
                  pl.BlockSpec(band.shape, lambda b_, i: (0, 0, 0)),
                  pl.BlockSpec(inv.shape, lambda b_, i: (0, 0)),
                  pl.BlockSpec((W_B, W_B), lambda b_, i: (0, 0)),
                  pl.BlockSpec((1, W_B), lambda b_, i: (0, 0)),
                  pl.BlockSpec((1, W_B), lambda b_, i: (0, 0))],
        out_specs=pl.BlockSpec((1, POOL_TILE, W_B), lambda b_, i: (b_, i, 0)),
        out_shape=jax.ShapeDtypeStruct(up.shape, F32),
        compiler_params=_params("arbitrary", "arbitrary"),
        name="pool",
    )(up, band, inv, w, b.reshape(1, W_B), s.reshape(1, W_B))


def _filter_kernel(f_ref, w1_ref, b1_ref, fr_ref, w2_ref, b2_ref, w3_ref, dl_ref, o_ref, *, tm):
    hp = lax.Precision.HIGHEST
    feats = f_ref[...]
    fr = fr_ref[...]
    hid = jnp.sin(fr * (jnp.dot(feats, w1_ref[...], precision=hp, preferred_element_type=F32) + b1_ref[...]))
    hid = jnp.sin(fr * (jnp.dot(hid, w2_ref[...], precision=hp, preferred_element_type=F32) + b2_ref[...]))
    filt = jnp.dot(hid, w3_ref[...], precision=hp, preferred_element_type=F32)
    filt = filt * jnp.exp(-feats[:, 0:1] * dl_ref[...])
    row = pl.program_id(0) * tm + lax.broadcasted_iota(jnp.int32, filt.shape, 0)
    bwd_col = (lax.broadcasted_iota(jnp.int32, filt.shape, 1) // W_C) % 2 == 1
    o_ref[...] = jnp.where((row == 0) & bwd_col, 0.0, filt)


def _filter_features(n):
    t = jnp.linspace(0.0, 1.0, n, dtype=F32)[:, None]
    wpos = 2.0 * math.pi * jnp.arange(n, dtype=F32)[:, None] / n
    bands = jnp.linspace(1e-4, HY_BANDS - 1, HY_BANDS, dtype=F32)[None, :]
    feats = jnp.concatenate([t, jnp.cos(bands * wpos), -jnp.sin(bands * wpos)], axis=-1)
    return jnp.pad(feats, ((0, 0), (0, HY_EMB_PAD - HY_EMB)))


def _filter_decay_rates():
    deltas = jnp.abs(jnp.linspace(math.log(HY_TARGET) / HY_FAST, math.log(HY_TARGET) / HY_SLOW, W_C, dtype=F32))
    return jnp.tile(deltas, 2 * HY_ORDER).reshape(1, 2 * HY_ORDER * W_C)


def _filters(n, w1, b1, freq, w2, b2, w3):
    tm = min(n, 512)
    nf = 2 * HY_ORDER * W_C
    const = lambda shape: pl.BlockSpec(shape, lambda i: (0, 0))
    return pl.pallas_call(
        functools.partial(_filter_kernel, tm=tm),
        grid=(n // tm,),
        in_specs=[pl.BlockSpec((tm, HY_EMB_PAD), lambda i: (i, 0)),
                  const((HY_EMB_PAD, HY_FH)), const((1, HY_FH)), const((1, HY_FH)),
                  const((HY_FH, HY_FH)), const((1, HY_FH)), const((HY_FH, nf)), const((1, nf))],
        out_specs=pl.BlockSpec((tm, nf), lambda i: (i, 0)),
        out_shape=jax.ShapeDtypeStruct((n, nf), F32),
        compiler_params=_params("arbitrary"),
        name="hyena_filter",
    )(_filter_features(n), jnp.pad(w1, ((0, HY_EMB_PAD - HY_EMB), (0, 0))), b1.reshape(1, HY_FH),
      freq.reshape(1, HY_FH), w2, b2.reshape(1, HY_FH), w3, _filter_decay_rates())


class _FftPlan:
    def __init__(self, n):
        assert n % (4 * FFT_MINOR) == 0
        self.n = n
        self.n_fft = 2 * n
        self.n1 = self.n_fft // FFT_MINOR
        self.t1 = self.n1 // 2
        self.k1 = self.n1 // 2 + 1
        self.k1p = -(-self.k1 // 16) * 16
        t1 = np.arange(self.t1)
        k1 = np.arange(self.k1)
        ang = 2.0 * np.pi * ((k1[:, None] * t1[None, :]) % self.n1) / self.n1
        fa = np.zeros((2 * self.k1p, self.t1))
        fa[:self.k1] = np.cos(ang)
        fa[self.k1p:self.k1p + self.k1] = -np.sin(ang)
        self.stage_a = _bf16_table(fa)
        wgt = np.where((k1 == 0) | (k1 == self.n1 // 2), 1.0, 2.0) / self.n_fft
        fc = np.zeros((self.t1, 2 * self.k1p))
        fc[:, :self.k1] = (np.cos(ang) * wgt[:, None]).T
        fc[:, self.k1p:self.k1p + self.k1] = (-np.sin(ang) * wgt[:, None]).T
        self.stage_a_inv = _bf16_table(fc)
        t2 = np.arange(FFT_MINOR)
        k2 = np.arange(FFT_MINOR)
        idx = (t2[None, None, :] * (k1[:, None, None] + self.n1 * k2[None, :, None])) % self.n_fft
        ang_b = 2.0 * np.pi * idx / self.n_fft
        gr, gi = np.cos(ang_b), -np.sin(ang_b)
        fwd = np.concatenate([np.concatenate([gr, -gi], axis=2), np.concatenate([gi, gr], axis=2)], axis=1)
        self.stage_b = _bf16_table(fwd)
        self.stage_b_inv = _bf16_table(np.transpose(fwd, (0, 2, 1)))


def _stage_a_forward(plan, x_refs, fa_ref, a_s):
    def body(t2, carry):
        xs = jnp.concatenate([xr[pl.ds(t2, plan.t1, stride=FFT_MINOR), :] for xr in x_refs], axis=1)
        av = jnp.dot(fa_ref[...], xs.astype(BF16), preferred_element_type=F32)
        for j in range(len(x_refs)):
            cols = slice(j * V7X_LANES, (j + 1) * V7X_LANES)
            a_s[j, pl.ds(t2, plan.k1p, stride=2 * FFT_MINOR), :] = av[:plan.k1p, cols]
            a_s[j, pl.ds(FFT_MINOR + t2, plan.k1p, stride=2 * FFT_MINOR), :] = av[plan.k1p:, cols]
        return carry
    lax.fori_loop(0, FFT_MINOR, body, 0)


def _stage_b_slab(a_s, k, count):
    r0 = pl.multiple_of(k * 2 * FFT_MINOR, 2 * FFT_MINOR)
    return jnp.concatenate([a_s[j, pl.ds(r0, 2 * FFT_MINOR), :] for j in range(count)], axis=1)


def _spectrum_kernel(x0_ref, x1_ref, fa_ref, fb_ref, o_ref, a_s, *, plan):
    k = pl.program_id(1)

    @pl.when(k == 0)
    def _():
        _stage_a_forward(plan, (x0_ref, x1_ref), fa_ref, a_s)

    o_ref[...] = jnp.dot(fb_ref[0], _stage_b_slab(a_s, k, 2).astype(BF16), preferred_element_type=F32)


def _filter_spectrum(plan, filt):
    n, nf = filt.shape
    rows = 2 * FFT_MINOR
    return pl.pallas_call(
        functools.partial(_spectrum_kernel, plan=plan),
        grid=(nf // (2 * V7X_LANES), plan.k1),
        in_specs=[_resident((n, V7X_LANES), lambda c, k: (0, 2 * c)),
                  _resident((n, V7X_LANES), lambda c, k: (0, 2 * c + 1)),
                  pl.BlockSpec(plan.stage_a.shape, lambda c, k: (0, 0)),
                  pl.BlockSpec((1, rows, rows), lambda c, k: (k, 0, 0))],
        out_specs=pl.BlockSpec((rows, 2 * V7X_LANES), lambda c, k: (k, c)),
        out_shape=jax.ShapeDtypeStruct((plan.k1 * rows, nf), F32),
        scratch_shapes=[pltpu.VMEM((2, plan.k1p * rows, V7X_LANES), F32)],
        compiler_params=_params("arbitrary", "arbitrary"),
        name="filter_spectrum",
    )(filt, filt, plan.stage_a, plan.stage_b)


def _conv_kernel(v_ref, g_ref, sf_ref, sg_ref, fa_ref, fb_ref, fbi_ref, fai_ref, sk_ref, o_ref, a_s, *, plan, bn):
    k = pl.program_id(1)
    half = FFT_MINOR

    @pl.when(k == 0)
    def _():
        _stage_a_forward(plan, [v_ref.at[b] for b in range(bn)], fa_ref, a_s)

    xs = jnp.dot(fb_ref[0], _stage_b_slab(a_s, k, bn).astype(BF16), preferred_element_type=F32)
    xr, xi = xs[:half], xs[half:]
    hr = sf_ref[:half, :] + sg_ref[:half, :]
    hi = sf_ref[half:, :] - sg_ref[half:, :]
    hr = jnp.concatenate([hr] * bn, axis=1)
    hi = jnp.concatenate([hi] * bn, axis=1)
    ys = jnp.concatenate([xr * hr - xi * hi, xr * hi + xi * hr], axis=0)
    ds = jnp.dot(fbi_ref[0], ys.astype(BF16), preferred_element_type=F32)
    r0 = pl.multiple_of(k * 2 * half, 2 * half)
    for b in range(bn):
        a_s[b, pl.ds(r0, 2 * half), :] = ds[:, b * V7X_LANES:(b + 1) * V7X_LANES]

    @pl.when(k == plan.k1 - 1)
    def _():
        def body(t2, carry):
            dsl = []
            for part in range(2):
                dsl.append(jnp.concatenate(
                    [a_s[b, pl.ds(part * half + t2, plan.k1p, stride=2 * half), :] for b in range(bn)], axis=1))
            y = jnp.dot(fai_ref[...], jnp.concatenate(dsl, axis=0).astype(BF16), preferred_element_type=F32)
            for b in range(bn):
                vv = v_ref[b, pl.ds(t2, plan.t1, stride=half), :]
                gg = g_ref[b, pl.ds(t2, plan.t1, stride=half), :]
                yb = y[:, b * V7X_LANES:(b + 1) * V7X_LANES]
                o_ref[b, pl.ds(t2, plan.t1, stride=half), :] = gg * (yb + sk_ref[...] * vv)
            return carry
        lax.fori_loop(0, half, body, 0)


def _hyena_conv(plan, v_arr, v_col, g_arr, g_col, spec, order, skip):
    bn, n, _ = v_arr.shape
    cb = V7X_LANES
    rows = 2 * FFT_MINOR
    n_cb = W_C // cb
    return pl.pallas_call(
        functools.partial(_conv_kernel, plan=plan, bn=bn),
        grid=(n_cb, plan.k1),
        in_specs=[_resident((bn, n, cb), lambda c, k: (0, 0, v_col + c)),
                  _resident((bn, n, cb), lambda c, k: (0, 0, g_col + c)),
                  pl.BlockSpec((rows, cb), lambda c, k: (k, (2 * order) * n_cb + c)),
                  pl.BlockSpec((rows, cb), lambda c, k: (k, (2 * order + 1) * n_cb + c)),
                  pl.BlockSpec(plan.stage_a.shape, lambda c, k: (0, 0)),
                  pl.BlockSpec((1, rows, rows), lambda c, k: (k, 0, 0)),
                  pl.BlockSpec((1, rows, rows), lambda c, k: (k, 0, 0)),
                  pl.BlockSpec(plan.stage_a_inv.shape, lambda c, k: (0, 0)),
                  pl.BlockSpec((1, cb), lambda c, k: (0, c))],
        out_specs=_resident((bn, n, cb), lambda c, k: (0, 0, c)),
        out_shape=jax.ShapeDtypeStruct((bn, n, W_C), F32),
        scratch_shapes=[pltpu.VMEM((bn, plan.k1p * rows, cb), F32)],
        compiler_params=_params("arbitrary", "arbitrary"),
        name="hyena_conv",
    )(v_arr, g_arr, spec, spec, plan.stage_a, plan.stage_b, plan.stage_b_inv, plan.stage_a_inv,
      skip.reshape(1, W_C))


def _hyena(z, n_fft_len, filt_args, skip):
    bn, n, _ = z.shape
    plan = _FftPlan(n_fft_len)
    filt = _filters(n, *filt_args)
    if n_fft_len != n:
        filt = jnp.pad(filt, ((0, n_fft_len - n), (0, 0)))
        z = jnp.pad(z, ((0, 0), (0, n_fft_len - n), (0, 0)))
    spec = _filter_spectrum(plan, filt)
    n_cb = W_C // V7X_LANES
    y1 = _hyena_conv(plan, z, 0, z, n_cb, spec, 0, skip[0])
    y2 = _hyena_conv(plan, y1, 0, z, 2 * n_cb, spec, 1, skip[1])
    return y2[:, :n]


def _mixout_kernel(x_ref, ya_ref, yb_ref, yc_ref, m_ref, w_ref, o_ref):
    y = jnp.dot(ya_ref[0].astype(BF16), w_ref[0:W_A, :], preferred_element_type=F32)
    y = y + jnp.dot(yb_ref[0].astype(BF16), w_ref[W_A:W_A + W_B, :], preferred_element_type=F32)
    y = y + jnp.dot(yc_ref[0].astype(BF16), w_ref[W_A + W_B:, :], preferred_element_type=F32)
    o_ref[0] = x_ref[0] + m_ref[0, 5:6, :] * y


def _mixout(x, ya, yb, yc, mod, w_out, *, tm):
    bn, n, _ = x.shape
    tile = lambda w: pl.BlockSpec((1, tm, w), lambda b, i: (b, i, 0))
    return pl.pallas_call(
        _mixout_kernel,
        grid=(bn, n // tm),
        in_specs=[tile(D_MODEL), tile(W_A), tile(W_B), tile(W_C),
                  pl.BlockSpec((1, N_MOD, D_MODEL), lambda b, i: (b, 0, 0)),
                  _resident((D_MODEL, D_MODEL), lambda b, i: (0, 0))],
        out_specs=tile(D_MODEL),
        out_shape=jax.ShapeDtypeStruct(x.shape, F32),
        compiler_params=_params("arbitrary", "arbitrary"),
        name="mixout",
    )(x, ya, yb, yc, mod, w_out)


def _dense_block_diag(w):
    h, c, d = w.shape[-3:]
    eye = jnp.eye(h, dtype=w.dtype)
    dense = jnp.einsum('...hcd,hk->...hckd', w, eye)
    return dense.reshape(w.shape[:-3] + (h * c, h * d))


def _tile_rows(n):
    return min(n, 512)


def kernel(x, c, ctx, c_ctx, w_mod, b_mod, norm_g, ffn_w_in, ffn_w_out, w_in, w_out, lru_conv_w, lru_conv_b,
           lru_gate_w, lru_gate_b, lru_lambda, pool_w, pool_b, pool_scale, hy_conv_w, hy_conv_b, hy_w1, hy_b1,
           hy_freq, hy_w2, hy_b2, hy_w3, hy_skip, final_g):
    bn, n, _ = x.shape
    n_ctx = ctx.shape[1]
    depth = w_mod.shape[0]
    assert bn + 1 <= V7X_SUBLANES and n % GRID_W == 0 and n % POOL_TILE == 0 and n_ctx == POOL_TILE
    tm_x, tm_c = _tile_rows(n), _tile_rows(n_ctx)

    cvec = jnp.zeros((V7X_SUBLANES, D_MODEL), F32).at[:bn].set(c).at[bn].set(c_ctx)
    mods = _modulation(cvec, w_mod, b_mod)
    band_x, inv_x = _pool_tables(GRID_W)
    band_c, inv_c = _pool_tables(n_ctx)
    zeros = jnp.zeros((bn, W_A), F32)

    xc = ctx
    for l in range(depth):
        last = l == depth - 1
        mx = mods[l, :bn].reshape(bn, N_MOD, D_MODEL)
        mc = jnp.broadcast_to(mods[l, bn].reshape(1, N_MOD, D_MODEL), (bn, N_MOD, D_MODEL))
        f_in = [ffn_w_in[l, k].astype(BF16) for k in range(2)]
        f_out = [ffn_w_out[l, k].astype(BF16) for k in range(2)]
        w_in_l = w_in[l].astype(BF16)
        w_out_l = w_out[l].astype(BF16)
        gate_dense = _dense_block_diag(lru_gate_w[l])
        wg = [jnp.concatenate([gate_dense[d, 0], gate_dense[d, 1]], axis=-1).astype(BF16) for d in range(2)]
        bg = [lru_gate_b[l, d].reshape(1, 2 * W_A) for d in range(2)]
        lam = [lru_lambda[l, d].reshape(1, W_A) for d in range(2)]
        pw = _dense_block_diag(pool_w[l]).astype(BF16)
        filt_args = (hy_w1[l], hy_b1[l], hy_freq[l], hy_w2[l], hy_b2[l], hy_w3[l])
        proj_args = (w_in_l, lru_conv_w[l], lru_conv_b[l], hy_conv_w[l], hy_conv_b[l])

        def mixer(xs, mod, tm, h0f, h0b, band, inv, fft_len, full):
            ua, ga, up, z = _proj(xs, mod, norm_g[l, 1], *proj_args, tm=tm)
            hf = _lru_fwd(ua, wg[0], bg[0], lam[0], h0f, tm=tm)
            ya, hend = _lru_bwd(ua, wg[1], bg[1], lam[1], h0b, hf, ga, tm=tm)
            if not full:
                return xs, hf[:, -1], hend
            yb = _pool(up, band, inv, pw, pool_b[l], pool_scale[l])
            yc = _hyena(z, fft_len, filt_args, hy_skip[l])
            return _mixout(xs, ya, yb, yc, mod, w_out_l, tm=tm), hf[:, -1], hend

        x = _ffn(x, mx, norm_g[l, 0], f_in[0], f_out[0], final_g, j0=0, final=False, tm=tm_x)
        xc = _ffn(xc, mc, norm_g[l, 0], f_in[0], f_out[0], final_g, j0=0, final=False, tm=tm_c)
        xc, h0f, h0b = mixer(xc, mc, tm_c, zeros, zeros, band_c, inv_c, CTX_FFT_LEN, not last)
        x, _, _ = mixer(x, mx, tm_x, h0f, h0b, band_x, inv_x, n, True)
        x = _ffn(x, mx, norm_g[l, 2], f_in[1], f_out[1], final_g, j0=6, final=last, tm=tm_x)
        if not last:
            xc = _ffn(xc, mc, norm_g[l, 2], f_in[1], f_out[1], final_g, j0=6, final=False, tm=tm_c)
    return x
```

```python
import functools
import math

import numpy as np
import jax
import jax.numpy as jnp
from jax import lax
from jax.experimental import pallas as pl
from jax.experimental.pallas import tpu as pltpu

F32 = jnp.float32
BF16 = jnp.bfloat16

D_MODEL = 1024
GRID_W = 64
N_MOD = 9
W_A, W_B, W_C = 512, 256, 256
D_IN = 2 * W_A + W_B + 3 * W_C
RG_HEADS, RG_HD, RG_CONV, RG_C = 8, 64, 4, 8.0
POOL_WINDOWS = (2, 4, 8, 16)
POOL_GW = W_B // len(POOL_WINDOWS)
HY_ORDER, HY_SHORT, HY_BANDS, HY_FH = 2, 3, 16, 64
HY_EMB = 2 * HY_BANDS + 1
HY_EMB_PAD = 40
HY_TARGET, HY_FAST, HY_SLOW = 1e-2, 0.3, 1.5
D_FF = 2816
EPS = 1e-6

V7X_LANES = 128
V7X_SUBLANES = 8
V7X_MXU_DIM = 256
V7X_VMEM_LIMIT_BYTES = 56 * 1024 * 1024

FFT_MINOR = V7X_LANES
HALO = V7X_SUBLANES
FF_CHUNKS = ((0, 768), (768, 768), (1536, 768), (2304, 512))
POOL_TILE = 256
CTX_FFT_LEN = 1024


def _params(*sem):
    return pltpu.CompilerParams(dimension_semantics=sem, vmem_limit_bytes=V7X_VMEM_LIMIT_BYTES)


def _resident(shape, index_map):
    return pl.BlockSpec(shape, index_map, pipeline_mode=pl.Buffered(1))


def _bf16_table(values):
    return jnp.asarray(values, F32).astype(BF16)


def _adanorm(x, g, shift, scale):
    ms = jnp.mean(x * x, axis=-1, keepdims=True)
    return (x * lax.rsqrt(ms + EPS)) * g * (1.0 + scale) + shift


def _mod_kernel(c_ref, w_ref, b_ref, o_ref):
    s = c_ref[...]
    s = s * jax.nn.sigmoid(s)
    o_ref[0] = jnp.dot(s.astype(BF16), w_ref[0].astype(BF16), preferred_element_type=F32) + b_ref[0]


def _modulation(cvec, w_mod, b_mod):
    depth, _, n = w_mod.shape
    tn = 1024
    return pl.pallas_call(
        _mod_kernel,
        grid=(depth, n // tn),
        in_specs=[pl.BlockSpec((V7X_SUBLANES, D_MODEL), lambda l, j: (0, 0)),
                  pl.BlockSpec((1, D_MODEL, tn), lambda l, j: (l, 0, j)),
                  pl.BlockSpec((1, 1, tn), lambda l, j: (l, 0, j))],
        out_specs=pl.BlockSpec((1, V7X_SUBLANES, tn), lambda l, j: (l, 0, j)),
        out_shape=jax.ShapeDtypeStruct((depth, V7X_SUBLANES, n), F32),
        compiler_params=_params("arbitrary", "arbitrary"),
        name="modulation",
    )(cvec, w_mod, b_mod.reshape(depth, 1, n))


def _ffn_kernel(x_ref, m_ref, g_ref, win_ref, wout_ref, fg_ref, o_ref, *, j0, final):
    x = x_ref[0]
    shift = m_ref[0, j0:j0 + 1, :]
    scale = m_ref[0, j0 + 1:j0 + 2, :]
    gate = m_ref[0, j0 + 2:j0 + 3, :]
    h = _adanorm(x, g_ref[...], shift, scale).astype(BF16)
    acc = None
    for s, w in FF_CHUNKS:
        gt = jnp.dot(h, win_ref[:, s:s + w], preferred_element_type=F32)
        up = jnp.dot(h, win_ref[:, D_FF + s:D_FF + s + w], preferred_element_type=F32)
        a = (gt * jax.nn.sigmoid(gt) * up).astype(BF16)
        p = jnp.dot(a, wout_ref[s:s + w, :], preferred_element_type=F32)
        acc = p if acc is None else acc + p
    y = x + (0.5 * gate) * acc
    if final:
        ms = jnp.mean(y * y, axis=-1, keepdims=True)
        y = (y * lax.rsqrt(ms + EPS)) * fg_ref[...]
    o_ref[0] = y


def _ffn(x, mod, g, w_in, w_out, final_g, *, j0, final, tm):
    bn, n, _ = x.shape
    return pl.pallas_call(
        functools.partial(_ffn_kernel, j0=j0, final=final),
        grid=(bn, n // tm),
        in_specs=[pl.BlockSpec((1, tm, D_MODEL), lambda b, i: (b, i, 0)),
                  pl.BlockSpec((1, N_MOD, D_MODEL), lambda b, i: (b, 0, 0)),
                  pl.BlockSpec((1, D_MODEL), lambda b, i: (0, 0)),
                  _resident((D_MODEL, 2 * D_FF), lambda b, i: (0, 0)),
                  _resident((D_FF, D_MODEL), lambda b, i: (0, 0)),
                  pl.BlockSpec((1, D_MODEL), lambda b, i: (0, 0))],
        out_specs=pl.BlockSpec((1, tm, D_MODEL), lambda b, i: (b, i, 0)),
        out_shape=jax.ShapeDtypeStruct(x.shape, F32),
        compiler_params=_params("arbitrary", "arbitrary"),
        name="ffn",
    )(x, mod, g.reshape(1, D_MODEL), w_in, w_out, final_g.reshape(1, D_MODEL))


def _proj_kernel(xp_ref, x_ref, xn_ref, m_ref, g_ref, w_ref, cwa_ref, cba_ref, cwh_ref, cbh_ref,
                 ua_ref, ga_ref, up_ref, z_ref, *, tm, n_tiles):
    i = pl.program_id(1)
    xe = jnp.concatenate([xp_ref[0], x_ref[0], xn_ref[0]], axis=0)
    h = _adanorm(xe, g_ref[...], m_ref[0, 3:4, :], m_ref[0, 4:5, :]).astype(BF16)
    proj = jnp.dot(h, w_ref[...], preferred_element_type=F32)
    row = lax.broadcasted_iota(jnp.int32, (tm + 2 * HALO, 1), 0)
    lo_ok = jnp.where(i > 0, 0, HALO)
    hi_ok = jnp.where(i < n_tiles - 1, tm + 2 * HALO, tm + HALO)
    proj = jnp.where((row >= lo_ok) & (row < hi_ok), proj, 0.0)

    xa = proj[:, :W_A]
    ua = cba_ref[...] + cwa_ref[0:1, :] * xa[HALO - 2:HALO - 2 + tm]
    for k in range(1, RG_CONV):
        ua = ua + cwa_ref[k:k + 1, :] * xa[HALO - 2 + k:HALO - 2 + k + tm]
    ua_ref[0] = ua
    ga_ref[0] = proj[HALO:HALO + tm, W_A:2 * W_A]
    up_ref[0] = proj[HALO:HALO + tm, 2 * W_A:2 * W_A + W_B]
    uh = proj[:, 2 * W_A + W_B:]
    z = cbh_ref[...] + cwh_ref[0:1, :] * uh[HALO - 1:HALO - 1 + tm]
    for k in range(1, HY_SHORT):
        z = z + cwh_ref[k:k + 1, :] * uh[HALO - 1 + k:HALO - 1 + k + tm]
    z_ref[0] = z


def _proj(x, mod, g, w_in, cwa, cba, cwh, cbh, *, tm):
    bn, n, _ = x.shape
    n_tiles = n // tm
    hb = tm // HALO
    last_hb = n // HALO - 1
    outs = (W_A, W_A, W_B, 3 * W_C)
    return pl.pallas_call(
        functools.partial(_proj_kernel, tm=tm, n_tiles=n_tiles),
        grid=(bn, n_tiles),
        in_specs=[pl.BlockSpec((1, HALO, D_MODEL), lambda b, i: (b, jnp.maximum(i * hb - 1, 0), 0)),
                  pl.BlockSpec((1, tm, D_MODEL), lambda b, i: (b, i, 0)),
                  pl.BlockSpec((1, HALO, D_MODEL), lambda b, i: (b, jnp.minimum((i + 1) * hb, last_hb), 0)),
                  pl.BlockSpec((1, N_MOD, D_MODEL), lambda b, i: (b, 0, 0)),
                  pl.BlockSpec((1, D_MODEL), lambda b, i: (0, 0)),
                  _resident((D_MODEL, D_IN), lambda b, i: (0, 0)),
                  pl.BlockSpec((RG_CONV, W_A), lambda b, i: (0, 0)),
                  pl.BlockSpec((1, W_A), lambda b, i: (0, 0)),
                  pl.BlockSpec((HY_SHORT, 3 * W_C), lambda b, i: (0, 0)),
                  pl.BlockSpec((1, 3 * W_C), lambda b, i: (0, 0))],
        out_specs=[pl.BlockSpec((1, tm, w), lambda b, i: (b, i, 0)) for w in outs],
        out_shape=[jax.ShapeDtypeStruct((bn, n, w), F32) for w in outs],
        compiler_params=_params("arbitrary", "arbitrary"),
        name="proj",
    )(x, x, x, mod, g.reshape(1, D_MODEL), w_in, cwa, cba.reshape(1, W_A), cwh, cbh.reshape(1, 3 * W_C))


def _lru_coeffs(u, wg_ref, bg_ref, lam_ref, a_s, b_s):
    pre = jnp.dot(u.astype(BF16), wg_ref[...], preferred_element_type=F32) + bg_ref[...]
    r = jax.nn.sigmoid(pre[:, :W_A])
    ig = jax.nn.sigmoid(pre[:, W_A:])
    log_a = (-RG_C * r) * jax.nn.softplus(-lam_ref[...])
    a_s[...] = jnp.exp(log_a)
    th = jnp.tanh(log_a)
    b_s[...] = jnp.sqrt(-2.0 * th / (1.0 - th)) * (ig * u)


def _lru_scan_tile(a_s, b_s, carry, *, tm, reverse):
    groups = tm // V7X_SUBLANES
    rowi = lax.broadcasted_iota(jnp.int32, (V7X_SUBLANES, W_A), 0)

    def body(k, c):
        gi = (groups - 1 - k) if reverse else k
        r0 = pl.multiple_of(gi * V7X_SUBLANES, V7X_SUBLANES)
        a = a_s[pl.ds(r0, V7X_SUBLANES), :]
        b = b_s[pl.ds(r0, V7X_SUBLANES), :]
        for s in (1, 2, 4):
            if reverse:
                shift, m = V7X_SUBLANES - s, rowi < V7X_SUBLANES - s
            else:
                shift, m = s, rowi >= s
            a_sh = pltpu.roll(a, shift, 0)
            b_sh = pltpu.roll(b, shift, 0)
            b = jnp.where(m, a * b_sh + b, b)
            a = jnp.where(m, a * a_sh, a)
        h = a * c + b
        b_s[pl.ds(r0, V7X_SUBLANES), :] = h
        return h[0:1, :] if reverse else h[V7X_SUBLANES - 1:V7X_SUBLANES, :]

    return lax.fori_loop(0, groups, body, carry, unroll=4)


def _lru_fwd_kernel(u_ref, wg_ref, bg_ref, lam_ref, h0_ref, hf_ref, a_s, b_s, c_s, *, tm):
    @pl.when(pl.program_id(1) == 0)
    def _():
        c_s[...] = h0_ref[0]

    _lru_coeffs(u_ref[0], wg_ref, bg_ref, lam_ref, a_s, b_s)
    c_s[...] = _lru_scan_tile(a_s, b_s, c_s[...], tm=tm, reverse=False)
    hf_ref[0] = b_s[...]


def _lru_bwd_kernel(u_ref, wg_ref, bg_ref, lam_ref, h0_ref, hf_ref, ga_ref, ya_ref, hend_ref,
                    a_s, b_s, c_s, *, tm):
    @pl.when(pl.program_id(1) == 0)
    def _():
        c_s[...] = h0_ref[0]

    _lru_coeffs(u_ref[0], wg_ref, bg_ref, lam_ref, a_s, b_s)
    c_s[...] = _lru_scan_tile(a_s, b_s, c_s[...], tm=tm, reverse=True)
    hend_ref[0] = c_s[...]
    ya_ref[0] = (hf_ref[0] + b_s[...]) * jax.nn.gelu(ga_ref[0])


def _lru_common_specs(tm, tile_map):
    return [pl.BlockSpec((1, tm, W_A), tile_map),
            pl.BlockSpec((W_A, 2 * W_A), lambda b, i: (0, 0)),
            pl.BlockSpec((1, 2 * W_A), lambda b, i: (0, 0)),
            pl.BlockSpec((1, W_A), lambda b, i: (0, 0)),
            pl.BlockSpec((1, 1, W_A), lambda b, i: (b, 0, 0))]


def _lru_scratch(tm):
    return [pltpu.VMEM((tm, W_A), F32), pltpu.VMEM((tm, W_A), F32), pltpu.VMEM((1, W_A), F32)]


def _lru_fwd(u, wg, bg, lam, h0, *, tm):
    bn, n, _ = u.shape
    tile_map = lambda b, i: (b, i, 0)
    return pl.pallas_call(
        functools.partial(_lru_fwd_kernel, tm=tm),
        grid=(bn, n // tm),
        in_specs=_lru_common_specs(tm, tile_map),
        out_specs=pl.BlockSpec((1, tm, W_A), tile_map),
        out_shape=jax.ShapeDtypeStruct(u.shape, F32),
        scratch_shapes=_lru_scratch(tm),
        compiler_params=_params("arbitrary", "arbitrary"),
        name="lru_fwd",
    )(u, wg, bg, lam, h0.reshape(bn, 1, W_A))


def _lru_bwd(u, wg, bg, lam, h0, hf, ga, *, tm):
    bn, n, _ = u.shape
    last = n // tm - 1
    tile_map = lambda b, i: (b, last - i, 0)
    ya, hend = pl.pallas_call(
        functools.partial(_lru_bwd_kernel, tm=tm),
        grid=(bn, n // tm),
        in_specs=_lru_common_specs(tm, tile_map) + [pl.BlockSpec((1, tm, W_A), tile_map),
                                                    pl.BlockSpec((1, tm, W_A), tile_map)],
        out_specs=[pl.BlockSpec((1, tm, W_A), tile_map),
                   pl.BlockSpec((1, 1, W_A), lambda b, i: (b, 0, 0))],
        out_shape=[jax.ShapeDtypeStruct(u.shape, F32), jax.ShapeDtypeStruct((bn, 1, W_A), F32)],
        scratch_shapes=_lru_scratch(tm),
        compiler_params=_params("arbitrary", "arbitrary"),
        name="lru_bwd",
    )(u, wg, bg, lam, h0.reshape(bn, 1, W_A), hf, ga)
    return ya, hend.reshape(bn, W_A)


def _pool_tables(row_len):
    t = np.arange(POOL_TILE)
    p = t % row_len
    base = t - p
    band = np.zeros((len(POOL_WINDOWS), POOL_TILE, POOL_TILE), np.float32)
    inv = np.zeros((POOL_TILE, W_B), np.float32)
    for g, win in enumerate(POOL_WINDOWS):
        lo = np.clip(p - win // 2, 0, row_len)
        hi = np.clip(p + win - win // 2, 0, row_len)
        s = np.arange(POOL_TILE)[None, :]
        band[g] = ((s >= (base + lo)[:, None]) & (s < (base + hi)[:, None])).astype(np.float32)
        inv[:, g * POOL_GW:(g + 1) * POOL_GW] = (1.0 / (hi - lo).astype(np.float64))[:, None]
    return _bf16_table(band), jnp.asarray(inv, F32)


def _pool_kernel(u_ref, band_ref, inv_ref, w_ref, b_ref, s_ref, o_ref):
    x = u_ref[0]
    x_hi = x.astype(BF16)
    x_lo = (x - x_hi.astype(F32)).astype(BF16)
    col = lax.broadcasted_iota(jnp.int32, x.shape, 1) // POOL_GW
    tot = jnp.zeros_like(x)
    for g in range(len(POOL_WINDOWS)):
        sg = (jnp.dot(band_ref[g], x_hi, preferred_element_type=F32)
              + jnp.dot(band_ref[g], x_lo, preferred_element_type=F32))
        tot = jnp.where(col == g, sg, tot)
    pooled = tot * inv_ref[...] - x
    y = jnp.dot(pooled.astype(BF16), w_ref[...], preferred_element_type=F32)
    o_ref[0] = (y + b_ref[...]) * s_ref[...]


def _pool(up, band, inv, w, b, s):
    bn, n, _ = up.shape
    return pl.pallas_call(
        _pool_kernel,
        grid=(bn, n // POOL_TILE),
        in_specs=[pl.BlockSpec((1, POOL_TILE, W_B), lambda b_, i: (b_, i, 0)),
                  pl.BlockSpec(band.shape, lambda b_, i: (0, 0, 0)),
                  pl.BlockSpec(inv.shape, lambda b_, i: (0, 0)),
                  pl.BlockSpec((W_B, W_B), lambda b_, i: (0, 0)),
                  pl.BlockSpec((1, W_B), lambda b_, i: (0, 0)),
                  pl.BlockSpec((1, W_B), lambda b_, i: (0, 0))],
        out_specs=pl.BlockSpec((1, POOL_TILE, W_B), lambda b_, i: (b_, i, 0)),
        out_shape=jax.ShapeDtypeStruct(up.shape, F32),
        compiler_params=_params("arbitrary", "arbitrary"),
        name="pool",
    )(up, band, inv, w, b.reshape(1, W_B), s.reshape(1, W_B))


def _filter_kernel(f_ref, w1_ref, b1_ref, fr_ref, w2_ref, b2_ref, w3_ref, dl_ref, o_ref, *, tm):
    hp = lax.Precision.HIGHEST
    feats = f_ref[...]
    fr = fr_ref[...]
    hid = jnp.sin(fr * (jnp.dot(feats, w1_ref[...], precision=hp, preferred_element_type=F32) + b1_ref[...]))
    hid = jnp.sin(fr * (jnp.dot(hid, w2_ref[...], precision=hp, preferred_element_type=F32) + b2_ref[...]))
    filt = jnp.dot(hid, w3_ref[...], precision=hp, preferred_element_type=F32)
    filt = filt * jnp.exp(-feats[:, 0:1] * dl_ref[...])
    row = pl.program_id(0) * tm + lax.broadcasted_iota(jnp.int32, filt.shape, 0)
    bwd_col = (lax.broadcasted_iota(jnp.int32, filt.shape, 1) // W_C) % 2 == 1
    o_ref[...] = jnp.where((row == 0) & bwd_col, 0.0, filt)


def _filter_features(n):
    t = jnp.linspace(0.0, 1.0, n, dtype=F32)[:, None]
    wpos = 2.0 * math.pi * jnp.arange(n, dtype=F32)[:, None] / n
    bands = jnp.linspace(1e-4, HY_BANDS - 1, HY_BANDS, dtype=F32)[None, :]
    feats = jnp.concatenate([t, jnp.cos(bands * wpos), -jnp.sin(bands * wpos)], axis=-1)
    return jnp.pad(feats, ((0, 0), (0, HY_EMB_PAD - HY_EMB)))


def _filter_decay_rates():
    deltas = jnp.abs(jnp.linspace(math.log(HY_TARGET) / HY_FAST, math.log(HY_TARGET) / HY_SLOW, W_C, dtype=F32))
    return jnp.tile(deltas, 2 * HY_ORDER).reshape(1, 2 * HY_ORDER * W_C)


def _filters(n, w1, b1, freq, w2, b2, w3):
    tm = min(n, 512)
    nf = 2 * HY_ORDER * W_C
    const = lambda shape: pl.BlockSpec(shape, lambda i: (0, 0))
    return pl.pallas_call(
        functools.partial(_filter_kernel, tm=tm),
        grid=(n // tm,),
        in_specs=[pl.BlockSpec((tm, HY_EMB_PAD), lambda i: (i, 0)),
                  const((HY_EMB_PAD, HY_FH)), const((1, HY_FH)), const((1, HY_FH)),
                  const((HY_FH, HY_FH)), const((1, HY_FH)), const((HY_FH, nf)), const((1, nf))],
        out_specs=pl.BlockSpec((tm, nf), lambda i: (i, 0)),
        out_shape=jax.ShapeDtypeStruct((n, nf), F32),
        compiler_params=_params("arbitrary"),
        name="hyena_filter",
    )(_filter_features(n), jnp.pad(w1, ((0, HY_EMB_PAD - HY_EMB), (0, 0))), b1.reshape(1, HY_FH),
      freq.reshape(1, HY_FH), w2, b2.reshape(1, HY_FH), w3, _filter_decay_rates())


T2_BLOCKS = FFT_MINOR // V7X_SUBLANES


class _FftPlan:
    def __init__(self, n):
        assert n % (4 * FFT_MINOR) == 0
        self.n = n
        self.n_fft = 2 * n
        self.n1 = self.n_fft // FFT_MINOR
        self.t1 = self.n1 // 2
        self.k1 = self.n1 // 2 + 1
        self.k_block = max(d for d in range(1, 14) if self.k1 % d == 0)
        t1 = np.arange(self.t1)
        k1 = np.arange(self.k1)
        eye = np.eye(V7X_SUBLANES)
        ang = 2.0 * np.pi * ((k1[:, None] * t1[None, :]) % self.n1) / self.n1
        fa = np.concatenate([np.cos(ang), -np.sin(ang)], axis=0)
        self.stage_a = _bf16_table(np.kron(fa, eye))
        wgt = np.where((k1 == 0) | (k1 == self.n1 // 2), 1.0, 2.0) / self.n_fft
        fc = np.concatenate([np.cos(ang) * wgt[:, None], -np.sin(ang) * wgt[:, None]], axis=0).T
        self.stage_a_inv = _bf16_table(np.kron(fc, eye))
        t2 = np.arange(FFT_MINOR)
        k2 = np.arange(FFT_MINOR)
        idx = (t2[None, None, :] * (k1[:, None, None] + self.n1 * k2[None, :, None])) % self.n_fft
        ang_b = 2.0 * np.pi * idx / self.n_fft
        gr, gi = np.cos(ang_b), -np.sin(ang_b)
        fwd = np.concatenate([np.concatenate([gr, -gi], axis=2), np.concatenate([gi, gr], axis=2)], axis=1)
        self.stage_b = _bf16_table(fwd)
        self.stage_b_inv = _bf16_table(np.transpose(fwd, (0, 2, 1)))

    def scratch(self, signals):
        return pltpu.VMEM((signals, 2, self.k1, T2_BLOCKS, V7X_SUBLANES, V7X_LANES), F32)


def _time_tiles(ref, tb, rows):
    return ref[:, pl.ds(tb, 1)].reshape(rows, V7X_LANES)


def _stage_a_forward(plan, x_refs, fa_ref, a_s):
    def body(tb, carry):
        xs = jnp.concatenate([_time_tiles(xr, tb, plan.t1 * V7X_SUBLANES) for xr in x_refs], axis=1)
        av = jnp.dot(fa_ref[...], xs.astype(BF16), preferred_element_type=F32)
        for j in range(len(x_refs)):
            part = av[:, j * V7X_LANES:(j + 1) * V7X_LANES]
            a_s[j, :, :, pl.ds(tb, 1)] = part.reshape(2, plan.k1, 1, V7X_SUBLANES, V7X_LANES)
        return carry
    lax.fori_loop(0, T2_BLOCKS, body, 0)


def _stage_b_slab(a_s, k, count):
    return jnp.concatenate(
        [a_s[j, :, pl.ds(k, 1)].reshape(2 * FFT_MINOR, V7X_LANES) for j in range(count)], axis=1)


def _spectrum_kernel(x0_ref, x1_ref, fa_ref, fb_ref, o_ref, a_s, *, plan):
    kb = pl.program_id(1)

    @pl.when(kb == 0)
    def _():
        _stage_a_forward(plan, (x0_ref, x1_ref), fa_ref, a_s)

    rows = 2 * FFT_MINOR
    for i in range(plan.k_block):
        slab = _stage_b_slab(a_s, kb * plan.k_block + i, 2).astype(BF16)
        o_ref[i * rows:(i + 1) * rows, :] = jnp.dot(fb_ref[i], slab, preferred_element_type=F32)


def _filter_spectrum(plan, filt):
    n, nf = filt.shape
    rows = 2 * FFT_MINOR
    kbs = plan.k_block
    filt = filt.reshape(plan.t1, T2_BLOCKS, V7X_SUBLANES, nf)
    sig = (plan.t1, T2_BLOCKS, V7X_SUBLANES, V7X_LANES)
    return pl.pallas_call(
        functools.partial(_spectrum_kernel, plan=plan),
        grid=(nf // (2 * V7X_LANES), plan.k1 // kbs),
        in_specs=[_resident(sig, lambda c, k: (0, 0, 0, 2 * c)),
                  _resident(sig, lambda c, k: (0, 0, 0, 2 * c + 1)),
                  pl.BlockSpec(plan.stage_a.shape, lambda c, k: (0, 0)),
                  pl.BlockSpec((kbs, rows, rows), lambda c, k: (k, 0, 0))],
        out_specs=pl.BlockSpec((kbs * rows, 2 * V7X_LANES), lambda c, k: (k, c)),
        out_shape=jax.ShapeDtypeStruct((plan.k1 * rows, nf), F32),
        scratch_shapes=[plan.scratch(2)],
        compiler_params=_params("arbitrary", "arbitrary"),
        name="filter_spectrum",
    )(filt, filt, plan.stage_a, plan.stage_b)


def _conv_kernel(v_ref, g_ref, sf_ref, sg_ref, fa_ref, fb_ref, fbi_ref, fai_ref, sk_ref, o_ref, a_s, *, plan, bn):
    kb = pl.program_id(1)
    half = FFT_MINOR
    rows = 2 * half

    @pl.when(kb == 0)
    def _():
        _stage_a_forward(plan, [v_ref.at[b] for b in range(bn)], fa_ref, a_s)

    for i in range(plan.k_block):
        k = kb * plan.k_block + i
        xs = jnp.dot(fb_ref[i], _stage_b_slab(a_s, k, bn).astype(BF16), preferred_element_type=F32)
        xr, xi = xs[:half], xs[half:]
        sf = sf_ref[i * rows:(i + 1) * rows, :]
        sg = sg_ref[i * rows:(i + 1) * rows, :]
        hr = jnp.concatenate([sf[:half] + sg[:half]] * bn, axis=1)
        hi = jnp.concatenate([sf[half:] - sg[half:]] * bn, axis=1)
        ys = jnp.concatenate([xr * hr - xi * hi, xr * hi + xi * hr], axis=0)
        ds = jnp.dot(fbi_ref[i], ys.astype(BF16), preferred_element_type=F32)
        for b in range(bn):
            part = ds[:, b * V7X_LANES:(b + 1) * V7X_LANES]
            a_s[b, :, pl.ds(k, 1)] = part.reshape(2, 1, T2_BLOCKS, V7X_SUBLANES, V7X_LANES)

    @pl.when(kb == pl.num_programs(1) - 1)
    def _():
        t_rows = plan.t1 * V7X_SUBLANES

        def body(tb, carry):
            ds = jnp.concatenate(
                [a_s[b, :, :, pl.ds(tb, 1)].reshape(2 * plan.k1 * V7X_SUBLANES, V7X_LANES) for b in range(bn)],
                axis=1)
            y = jnp.dot(fai_ref[...], ds.astype(BF16), preferred_element_type=F32)
            for b in range(bn):
                vv = _time_tiles(v_ref.at[b], tb, t_rows)
                gg = _time_tiles(g_ref.at[b], tb, t_rows)
                res = gg * (y[:, b * V7X_LANES:(b + 1) * V7X_LANES] + sk_ref[...] * vv)
                o_ref[b, :, pl.ds(tb, 1)] = res.reshape(plan.t1, 1, V7X_SUBLANES, V7X_LANES)
            return carry
        lax.fori_loop(0, T2_BLOCKS, body, 0)


def _hyena_conv(plan, v_arr, v_col, g_arr, g_col, spec, order, skip):
    bn = v_arr.shape[0]
    cb = V7X_LANES
    rows = 2 * FFT_MINOR
    n_cb = W_C // cb
    kbs = plan.k_block
    sig = (bn, plan.t1, T2_BLOCKS, V7X_SUBLANES, cb)
    return pl.pallas_call(
        functools.partial(_conv_kernel, plan=plan, bn=bn),
        grid=(n_cb, plan.k1 // kbs),
        in_specs=[_resident(sig, lambda c, k: (0, 0, 0, 0, v_col + c)),
                  _resident(sig, lambda c, k: (0, 0, 0, 0, g_col + c)),
                  pl.BlockSpec((kbs * rows, cb), lambda c, k: (k, (2 * order) * n_cb + c)),
                  pl.BlockSpec((kbs * rows, cb), lambda c, k: (k, (2 * order + 1) * n_cb + c)),
                  pl.BlockSpec(plan.stage_a.shape, lambda c, k: (0, 0)),
                  pl.BlockSpec((kbs, rows, rows), lambda c, k: (k, 0, 0)),
                  pl.BlockSpec((kbs, rows, rows), lambda c, k: (k, 0, 0)),
                  pl.BlockSpec(plan.stage_a_inv.shape, lambda c, k: (0, 0)),
                  pl.BlockSpec((1, cb), lambda c, k: (0, c))],
        out_specs=_resident(sig, lambda c, k: (0, 0, 0, 0, c)),
        out_shape=jax.ShapeDtypeStruct((bn, plan.t1, T2_BLOCKS, V7X_SUBLANES, W_C), F32),
        scratch_shapes=[plan.scratch(bn)],
        compiler_params=_params("arbitrary", "arbitrary"),
        name="hyena_conv",
    )(v_arr, g_arr, spec, spec, plan.stage_a, plan.stage_b, plan.stage_b_inv, plan.stage_a_inv,
      skip.reshape(1, W_C))


def _hyena(z, n_fft_len, filt_args, skip):
    bn, n, _ = z.shape
    plan = _FftPlan(n_fft_len)
    filt = _filters(n, *filt_args)
    if n_fft_len != n:
        filt = jnp.pad(filt, ((0, n_fft_len - n), (0, 0)))
        z = jnp.pad(z, ((0, 0), (0, n_fft_len - n), (0, 0)))
    spec = _filter_spectrum(plan, filt)
    n_cb = W_C // V7X_LANES
    z = z.reshape(bn, plan.t1, T2_BLOCKS, V7X_SUBLANES, 3 * W_C)
    y1 = _hyena_conv(plan, z, 0, z, n_cb, spec, 0, skip[0])
    y2 = _hyena_conv(plan, y1, 0, z, 2 * n_cb, spec, 1, skip[1])
    return y2.reshape(bn, n_fft_len, W_C)[:, :n]


def _mixout_kernel(x_ref, ya_ref, yb_ref, yc_ref, m_ref, w_ref, o_ref):
    y = jnp.dot(ya_ref[0].astype(BF16), w_ref[0:W_A, :], preferred_element_type=F32)
    y = y + jnp.dot(yb_ref[0].astype(BF16), w_ref[W_A:W_A + W_B, :], preferred_element_type=F32)
    y = y + jnp.dot(yc_ref[0].astype(BF16), w_ref[W_A + W_B:, :], preferred_element_type=F32)
    o_ref[0] = x_ref[0] + m_ref[0, 5:6, :] * y


def _mixout(x, ya, yb, yc, mod, w_out, *, tm):
    bn, n, _ = x.shape
    tile = lambda w: pl.BlockSpec((1, tm, w), lambda b, i: (b, i, 0))
    return pl.pallas_call(
        _mixout_kernel,
        grid=(bn, n // tm),
        in_specs=[tile(D_MODEL), tile(W_A), tile(W_B), tile(W_C),
                  pl.BlockSpec((1, N_MOD, D_MODEL), lambda b, i: (b, 0, 0)),
                  _resident((D_MODEL, D_MODEL), lambda b, i: (0, 0))],
        out_specs=tile(D_MODEL),
        out_shape=jax.ShapeDtypeStruct(x.shape, F32),
        compiler_params=_params("arbitrary", "arbitrary"),
        name="mixout",
    )(x, ya, yb, yc, mod, w_out)


def _dense_block_diag(w):
    h, c, d = w.shape[-3:]
    eye = jnp.eye(h, dtype=w.dtype)
    dense = jnp.einsum('...hcd,hk->...hckd', w, eye)
    return dense.reshape(w.shape[:-3] + (h * c, h * d))


def _tile_rows(n):
    return min(n, 512)


def kernel(x, c, ctx, c_ctx, w_mod, b_mod, norm_g, ffn_w_in, ffn_w_out, w_in, w_out, lru_conv_w, lru_conv_b,
           lru_gate_w, lru_gate_b, lru_lambda, pool_w, pool_b, pool_scale, hy_conv_w, hy_conv_b, hy_w1, hy_b1,
           hy_freq, hy_w2, hy_b2, hy_w3, hy_skip, final_g):
    bn, n, _ = x.shape
    n_ctx = ctx.shape[1]
    depth = w_mod.shape[0]
    assert bn + 1 <= V7X_SUBLANES and n % GRID_W == 0 and n % POOL_TILE == 0 and n_ctx == POOL_TILE
    tm_x, tm_c = _tile_rows(n), _tile_rows(n_ctx)

    cvec = jnp.zeros((V7X_SUBLANES, D_MODEL), F32).at[:bn].set(c).at[bn].set(c_ctx)
    mods = _modulation(cvec, w_mod, b_mod)
    band_x, inv_x = _pool_tables(GRID_W)
    band_c, inv_c = _pool_tables(n_ctx)
    zeros = jnp.zeros((bn, W_A), F32)

    xc = ctx
    for l in range(depth):
        last = l == depth - 1
        mx = mods[l, :bn].reshape(bn, N_MOD, D_MODEL)
        mc = jnp.broadcast_to(mods[l, bn].reshape(1, N_MOD, D_MODEL), (bn, N_MOD, D_MODEL))
        f_in = [ffn_w_in[l, k].astype(BF16) for k in range(2)]
        f_out = [ffn_w_out[l, k].astype(BF16) for k in range(2)]
        w_in_l = w_in[l].astype(BF16)
        w_out_l = w_out[l].astype(BF16)
        gate_dense = _dense_block_diag(lru_gate_w[l])
        wg = [jnp.concatenate([gate_dense[d, 0], gate_dense[d, 1]], axis=-1).astype(BF16) for d in range(2)]
        bg = [lru_gate_b[l, d].reshape(1, 2 * W_A) for d in range(2)]
        lam = [lru_lambda[l, d].reshape(1, W_A) for d in range(2)]
        pw = _dense_block_diag(pool_w[l]).astype(BF16)
        filt_args = (hy_w1[l], hy_b1[l], hy_freq[l], hy_w2[l], hy_b2[l], hy_w3[l])
        proj_args = (w_in_l, lru_conv_w[l], lru_conv_b[l], hy_conv_w[l], hy_conv_b[l])

        def mixer(xs, mod, tm, h0f, h0b, band, inv, fft_len, full):
            ua, ga, up, z = _proj(xs, mod, norm_g[l, 1], *proj_args, tm=tm)
            hf = _lru_fwd(ua, wg[0], bg[0], lam[0], h0f, tm=tm)
            ya, hend = _lru_bwd(ua, wg[1], bg[1], lam[1], h0b, hf, ga, tm=tm)
            if not full:
                return xs, hf[:, -1], hend
            yb = _pool(up, band, inv, pw, pool_b[l], pool_scale[l])
            yc = _hyena(z, fft_len, filt_args, hy_skip[l])
            return _mixout(xs, ya, yb, yc, mod, w_out_l, tm=tm), hf[:, -1], hend

        x = _ffn(x, mx, norm_g[l, 0], f_in[0], f_out[0], final_g, j0=0, final=False, tm=tm_x)
        xc = _ffn(xc, mc, norm_g[l, 0], f_in[0], f_out[0], final_g, j0=0, final=False, tm=tm_c)
        xc, h0f, h0b = mixer(xc, mc, tm_c, zeros, zeros, band_c, inv_c, CTX_FFT_LEN, not last)
        x, _, _ = mixer(x, mx, tm_x, h0f, h0b, band_x, inv_x, n, True)
        x = _ffn(x, mx, norm_g[l, 2], f_in[1], f_out[1], final_g, j0=6, final=last, tm=tm_x)
        if not last:
            xc = _ffn(xc, mc, norm_g[l, 2], f_in[1], f_out[1], final_g, j0=6, final=False, tm=tm_c)
    return x
```

```python
import functools
import math

import numpy as np
import jax
import jax.numpy as jnp
from jax import lax
from jax.experimental import pallas as pl
from jax.experimental.pallas import tpu as pltpu

F32 = jnp.float32
BF16 = jnp.bfloat16

D_MODEL = 1024
GRID_W = 64
N_MOD = 9
W_A, W_B, W_C = 512, 256, 256
D_IN = 2 * W_A + W_B + 3 * W_C
RG_HEADS, RG_HD, RG_CONV, RG_C = 8, 64, 4, 8.0
POOL_WINDOWS = (2, 4, 8, 16)
POOL_GW = W_B // len(POOL_WINDOWS)
HY_ORDER, HY_SHORT, HY_BANDS, HY_FH = 2, 3, 16, 64
HY_EMB = 2 * HY_BANDS + 1
HY_EMB_PAD = 40
HY_TARGET, HY_FAST, HY_SLOW = 1e-2, 0.3, 1.5
D_FF = 2816
EPS = 1e-6

V7X_LANES = 128
V7X_SUBLANES = 8
V7X_MXU_DIM = 256
V7X_VMEM_LIMIT_BYTES = 56 * 1024 * 1024

FFT_MINOR = V7X_LANES
HALO = V7X_SUBLANES
FF_CHUNKS = ((0, 768), (768, 768), (1536, 768), (2304, 512))
POOL_TILE = 256
CTX_FFT_LEN = 1024


def _params(*sem):
    return pltpu.CompilerParams(dimension_semantics=sem, vmem_limit_bytes=V7X_VMEM_LIMIT_BYTES)


def _resident(shape, index_map):
    return pl.BlockSpec(shape, index_map, pipeline_mode=pl.Buffered(1))


def _bf16_table(values):
    return jnp.asarray(values, F32).astype(BF16)


def _split_bf16(x):
    hi = x.astype(BF16)
    return hi, (x - hi.astype(F32)).astype(BF16)


def _dot_bf16x3(a, b):
    a_hi, a_lo = _split_bf16(a)
    b_hi, b_lo = _split_bf16(b)
    d = lambda p, q: jnp.dot(p, q, preferred_element_type=F32)
    return d(a_hi, b_hi) + (d(a_lo, b_hi) + d(a_hi, b_lo))


def _adanorm(x, g, shift, scale):
    ms = jnp.mean(x * x, axis=-1, keepdims=True)
    return (x * lax.rsqrt(ms + EPS)) * g * (1.0 + scale) + shift


def _mod_kernel(c_ref, w_ref, b_ref, o_ref):
    s = c_ref[...]
    s = s * jax.nn.sigmoid(s)
    o_ref[0] = jnp.dot(s.astype(BF16), w_ref[0].astype(BF16), preferred_element_type=F32) + b_ref[0]


def _modulation(cvec, w_mod, b_mod):
    depth, _, n = w_mod.shape
    tn = 1024
    return pl.pallas_call(
        _mod_kernel,
        grid=(depth, n // tn),
        in_specs=[pl.BlockSpec((V7X_SUBLANES, D_MODEL), lambda l, j: (0, 0)),
                  pl.BlockSpec((1, D_MODEL, tn), lambda l, j: (l, 0, j)),
                  pl.BlockSpec((1, 1, tn), lambda l, j: (l, 0, j))],
        out_specs=pl.BlockSpec((1, V7X_SUBLANES, tn), lambda l, j: (l, 0, j)),
        out_shape=jax.ShapeDtypeStruct((depth, V7X_SUBLANES, n), F32),
        compiler_params=_params("arbitrary", "arbitrary"),
        name="modulation",
    )(cvec, w_mod, b_mod.reshape(depth, 1, n))


def _pool_tile(x, band_ref, inv_ref, w_ref, b_ref, s_ref):
    xb = x.astype(BF16)
    col = lax.broadcasted_iota(jnp.int32, x.shape, 1) // POOL_GW
    tot = jnp.zeros_like(x)
    for g in range(len(POOL_WINDOWS)):
        tot = jnp.where(col == g, jnp.dot(band_ref[g], xb, preferred_element_type=F32), tot)
    pooled = tot * inv_ref[...] - x
    y = jnp.dot(pooled.astype(BF16), w_ref[...], preferred_element_type=F32)
    return (y + b_ref[...]) * s_ref[...]


def _ffn_kernel(x_ref, m_ref, g_ref, win_ref, wout_ref, fg_ref, *rest, j0, final, mix, tm):
    o_ref = rest[-1]
    x = x_ref[0]
    if mix:
        ya_ref, up_ref, yc_ref, band_ref, inv_ref, pw_ref, pb_ref, ps_ref, wo_ref = rest[:-1]
        yb = jnp.concatenate([_pool_tile(up_ref[0, r:r + POOL_TILE, :], band_ref, inv_ref, pw_ref, pb_ref, ps_ref)
                              for r in range(0, tm, POOL_TILE)], axis=0)
        y = jnp.dot(ya_ref[0].astype(BF16), wo_ref[0:W_A, :], preferred_element_type=F32)
        y = y + jnp.dot(yb.astype(BF16), wo_ref[W_A:W_A + W_B, :], preferred_element_type=F32)
        y = y + jnp.dot(yc_ref[0].astype(BF16), wo_ref[W_A + W_B:, :], preferred_element_type=F32)
        x = x + m_ref[0, 5:6, :] * y
    shift = m_ref[0, j0:j0 + 1, :]
    scale = m_ref[0, j0 + 1:j0 + 2, :]
    gate = m_ref[0, j0 + 2:j0 + 3, :]
    h = _adanorm(x, g_ref[...], shift, scale).astype(BF16)
    acc = None
    for s, w in FF_CHUNKS:
        gt = jnp.dot(h, win_ref[:, s:s + w], preferred_element_type=F32)
        up = jnp.dot(h, win_ref[:, D_FF + s:D_FF + s + w], preferred_element_type=F32)
        a = (gt * jax.nn.sigmoid(gt) * up).astype(BF16)
        p = jnp.dot(a, wout_ref[s:s + w, :], preferred_element_type=F32)
        acc = p if acc is None else acc + p
    y = x + (0.5 * gate) * acc
    if final:
        ms = jnp.mean(y * y, axis=-1, keepdims=True)
        y = (y * lax.rsqrt(ms + EPS)) * fg_ref[...]
    o_ref[0] = y


def _ffn(x, mod, g, w_in, w_out, final_g, mix=None, *, j0, final, tm):
    bn, n, _ = x.shape
    tile = lambda w: pl.BlockSpec((1, tm, w), lambda b, i: (b, i, 0))
    const = lambda shape: pl.BlockSpec(shape, lambda b, i: (0,) * len(shape))
    in_specs = [tile(D_MODEL),
                pl.BlockSpec((1, N_MOD, D_MODEL), lambda b, i: (b, 0, 0)),
                const((1, D_MODEL)),
                _resident((D_MODEL, 2 * D_FF), lambda b, i: (0, 0)),
                _resident((D_FF, D_MODEL), lambda b, i: (0, 0)),
                const((1, D_MODEL))]
    args = [x, mod, g.reshape(1, D_MODEL), w_in, w_out, final_g.reshape(1, D_MODEL)]
    if mix is not None:
        ya, up, yc, band, inv, pw, pb, ps, wo = mix
        in_specs += [tile(W_A), tile(W_B), tile(W_C), const(band.shape), const(inv.shape), const((W_B, W_B)),
                     const((1, W_B)), const((1, W_B)), _resident((D_MODEL, D_MODEL), lambda b, i: (0, 0))]
        args += [ya, up, yc, band, inv, pw, pb.reshape(1, W_B), ps.reshape(1, W_B), wo]
    return pl.pallas_call(
        functools.partial(_ffn_kernel, j0=j0, final=final, mix=mix is not None, tm=tm),
        grid=(bn, n // tm),
        in_specs=in_specs,
        out_specs=tile(D_MODEL),
        out_shape=jax.ShapeDtypeStruct(x.shape, F32),
        compiler_params=_params("arbitrary", "arbitrary"),
        name="mix_ffn" if mix is not None else "ffn",
    )(*args)


def _proj_kernel(xp_ref, x_ref, xn_ref, m_ref, g_ref, w_ref, cwa_ref, cba_ref, cwh_ref, cbh_ref,
                 ua_ref, ga_ref, up_ref, z_ref, *, tm, n_tiles):
    i = pl.program_id(1)
    norm = lambda v: _adanorm(v, g_ref[...], m_ref[0, 3:4, :], m_ref[0, 4:5, :])
    h_prev = jnp.where(i > 0, norm(xp_ref[0]), 0.0)
    h_next = jnp.where(i < n_tiles - 1, norm(xn_ref[0]), 0.0)
    h = jnp.concatenate([h_prev, norm(x_ref[0]), h_next], axis=0).astype(BF16)
    proj = jnp.dot(h, w_ref[...], preferred_element_type=F32)

    xa = proj[:, :W_A]
    ua = cba_ref[...] + cwa_ref[0:1, :] * xa[HALO - 2:HALO - 2 + tm]
    for k in range(1, RG_CONV):
        ua = ua + cwa_ref[k:k + 1, :] * xa[HALO - 2 + k:HALO - 2 + k + tm]
    ua_ref[0] = ua
    ga_ref[0] = proj[HALO:HALO + tm, W_A:2 * W_A]
    up_ref[0] = proj[HALO:HALO + tm, 2 * W_A:2 * W_A + W_B]
    uh = proj[:, 2 * W_A + W_B:]
    z = cbh_ref[...] + cwh_ref[0:1, :] * uh[HALO - 1:HALO - 1 + tm]
    for k in range(1, HY_SHORT):
        z = z + cwh_ref[k:k + 1, :] * uh[HALO - 1 + k:HALO - 1 + k + tm]
    z_ref[0] = z


def _proj(x, mod, g, w_in, cwa, cba, cwh, cbh, *, tm):
    bn, n, _ = x.shape
    n_tiles = n // tm
    hb = tm // HALO
    last_hb = n // HALO - 1
    outs = (W_A, W_A, W_B, 3 * W_C)
    return pl.pallas_call(
        functools.partial(_proj_kernel, tm=tm, n_tiles=n_tiles),
        grid=(bn, n_tiles),
        in_specs=[pl.BlockSpec((1, HALO, D_MODEL), lambda b, i: (b, jnp.maximum(i * hb - 1, 0), 0)),
                  pl.BlockSpec((1, tm, D_MODEL), lambda b, i: (b, i, 0)),
                  pl.BlockSpec((1, HALO, D_MODEL), lambda b, i: (b, jnp.minimum((i + 1) * hb, last_hb), 0)),
                  pl.BlockSpec((1, N_MOD, D_MODEL), lambda b, i: (b, 0, 0)),
                  pl.BlockSpec((1, D_MODEL), lambda b, i: (0, 0)),
                  _resident((D_MODEL, D_IN), lambda b, i: (0, 0)),
                  pl.BlockSpec((RG_CONV, W_A), lambda b, i: (0, 0)),
                  pl.BlockSpec((1, W_A), lambda b, i: (0, 0)),
                  pl.BlockSpec((HY_SHORT, 3 * W_C), lambda b, i: (0, 0)),
                  pl.BlockSpec((1, 3 * W_C), lambda b, i: (0, 0))],
        out_specs=[pl.BlockSpec((1, tm, w), lambda b, i: (b, i, 0)) for w in outs],
        out_shape=[jax.ShapeDtypeStruct((bn, n, w), F32) for w in outs],
        compiler_params=_params("arbitrary", "arbitrary"),
        name="proj",
    )(x, x, x, mod, g.reshape(1, D_MODEL), w_in, cwa, cba.reshape(1, W_A), cwh, cbh.reshape(1, 3 * W_C))


def _lru_coeffs(u, wg_ref, bg_ref, lam_ref, a_s, b_s):
    pre = jnp.dot(u.astype(BF16), wg_ref[...], preferred_element_type=F32) + bg_ref[...]
    sigmoid = lambda v: 0.5 * jnp.tanh(0.5 * v) + 0.5
    r = sigmoid(pre[:, :W_A])
    ig = sigmoid(pre[:, W_A:])
    log_a = (-RG_C * r) * jax.nn.softplus(-lam_ref[...])
    t = jnp.tanh(0.5 * log_a)
    inv = 1.0 / (1.0 - t)
    a_s[...] = (1.0 + t) * inv
    b_s[...] = (2.0 * jnp.sqrt(-t) * inv) * (ig * u)


def _lru_scan_tile(a_s, b_s, carry, *, tm, reverse):
    groups = tm // V7X_SUBLANES
    rowi = lax.broadcasted_iota(jnp.int32, (V7X_SUBLANES, W_A), 0)

    def body(k, c):
        gi = (groups - 1 - k) if reverse else k
        r0 = pl.multiple_of(gi * V7X_SUBLANES, V7X_SUBLANES)
        a = a_s[pl.ds(r0, V7X_SUBLANES), :]
        b = b_s[pl.ds(r0, V7X_SUBLANES), :]
        first = rowi == (V7X_SUBLANES - 1 if reverse else 0)
        b = jnp.where(first, a * c + b, b)
        for s in (1, 2, 4):
            if reverse:
                shift, m = V7X_SUBLANES - s, rowi < V7X_SUBLANES - s
            else:
                shift, m = s, rowi >= s
            b = jnp.where(m, a * pltpu.roll(b, shift, 0) + b, b)
            if s < V7X_SUBLANES // 2:
                a = jnp.where(m, a * pltpu.roll(a, shift, 0), a)
        b_s[pl.ds(r0, V7X_SUBLANES), :] = b
        return b[0:1, :] if reverse else b[V7X_SUBLANES - 1:V7X_SUBLANES, :]

    return lax.fori_loop(0, groups, body, carry, unroll=4)


def _lru_fwd_kernel(u_ref, wg_ref, bg_ref, lam_ref, h0_ref, hf_ref, a_s, b_s, c_s, *, tm):
    @pl.when(pl.program_id(1) == 0)
    def _():
        c_s[...] = h0_ref[0]

    _lru_coeffs(u_ref[0], wg_ref, bg_ref, lam_ref, a_s, b_s)
    c_s[...] = _lru_scan_tile(a_s, b_s, c_s[...], tm=tm, reverse=False)
    hf_ref[0] = b_s[...]


def _lru_bwd_kernel(u_ref, wg_ref, bg_ref, lam_ref, h0_ref, hf_ref, ga_ref, ya_ref, hend_ref,
                    a_s, b_s, c_s, *, tm):
    @pl.when(pl.program_id(1) == 0)
    def _():
        c_s[...] = h0_ref[0]

    _lru_coeffs(u_ref[0], wg_ref, bg_ref, lam_ref, a_s, b_s)
    c_s[...] = _lru_scan_tile(a_s, b_s, c_s[...], tm=tm, reverse=True)
    hend_ref[0] = c_s[...]
    ya_ref[0] = (hf_ref[0] + b_s[...]) * jax.nn.gelu(ga_ref[0])


def _lru_common_specs(tm, tile_map):
    return [pl.BlockSpec((1, tm, W_A), tile_map),
            pl.BlockSpec((W_A, 2 * W_A), lambda b, i: (0, 0)),
            pl.BlockSpec((1, 2 * W_A), lambda b, i: (0, 0)),
            pl.BlockSpec((1, W_A), lambda b, i: (0, 0)),
            pl.BlockSpec((1, 1, W_A), lambda b, i: (b, 0, 0))]


def _lru_scratch(tm):
    return [pltpu.VMEM((tm, W_A), F32), pltpu.VMEM((tm, W_A), F32), pltpu.VMEM((1, W_A), F32)]


def _lru_fwd(u, wg, bg, lam, h0, *, tm):
    bn, n, _ = u.shape
    tile_map = lambda b, i: (b, i, 0)
    return pl.pallas_call(
        functools.partial(_lru_fwd_kernel, tm=tm),
        grid=(bn, n // tm),
        in_specs=_lru_common_specs(tm, tile_map),
        out_specs=pl.BlockSpec((1, tm, W_A), tile_map),
        out_shape=jax.ShapeDtypeStruct(u.shape, F32),
        scratch_shapes=_lru_scratch(tm),
        compiler_params=_params("arbitrary", "arbitrary"),
        name="lru_fwd",
    )(u, wg, bg, lam, h0.reshape(bn, 1, W_A))


def _lru_bwd(u, wg, bg, lam, h0, hf, ga, *, tm):
    bn, n, _ = u.shape
    last = n // tm - 1
    tile_map = lambda b, i: (b, last - i, 0)
    ya, hend = pl.pallas_call(
        functools.partial(_lru_bwd_kernel, tm=tm),
        grid=(bn, n // tm),
        in_specs=_lru_common_specs(tm, tile_map) + [pl.BlockSpec((1, tm, W_A), tile_map),
                                                    pl.BlockSpec((1, tm, W_A), tile_map)],
        out_specs=[pl.BlockSpec((1, tm, W_A), tile_map),
                   pl.BlockSpec((1, 1, W_A), lambda b, i: (b, 0, 0))],
        out_shape=[jax.ShapeDtypeStruct(u.shape, F32), jax.ShapeDtypeStruct((bn, 1, W_A), F32)],
        scratch_shapes=_lru_scratch(tm),
        compiler_params=_params("arbitrary", "arbitrary"),
        name="lru_bwd",
    )(u, wg, bg, lam, h0.reshape(bn, 1, W_A), hf, ga)
    return ya, hend.reshape(bn, W_A)


def _pool_tables(row_len):
    t = np.arange(POOL_TILE)
    p = t % row_len
    base = t - p
    band = np.zeros((len(POOL_WINDOWS), POOL_TILE, POOL_TILE), np.float32)
    inv = np.zeros((POOL_TILE, W_B), np.float32)
    for g, win in enumerate(POOL_WINDOWS):
        lo = np.clip(p - win // 2, 0, row_len)
        hi = np.clip(p + win - win // 2, 0, row_len)
        s = np.arange(POOL_TILE)[None, :]
        band[g] = ((s >= (base + lo)[:, None]) & (s < (base + hi)[:, None])).astype(np.float32)
        inv[:, g * POOL_GW:(g + 1) * POOL_GW] = (1.0 / (hi - lo).astype(np.float64))[:, None]
    return _bf16_table(band), jnp.asarray(inv, F32)


def _filter_kernel(f_ref, w1_ref, b1_ref, fr_ref, w2_ref, b2_ref, w3_ref, dl_ref, o_ref, *, tm):
    feats = f_ref[...]
    fr = fr_ref[...]
    hid = jnp.sin(fr * (_dot_bf16x3(feats, w1_ref[...]) + b1_ref[...]))
    hid = jnp.sin(fr * (_dot_bf16x3(hid, w2_ref[...]) + b2_ref[...]))
    filt = _dot_bf16x3(hid, w3_ref[...])
    filt = filt * jnp.exp(-feats[:, 0:1] * dl_ref[...])
    row = pl.program_id(0) * tm + lax.broadcasted_iota(jnp.int32, filt.shape, 0)
    bwd_col = (lax.broadcasted_iota(jnp.int32, filt.shape, 1) // W_C) % 2 == 1
    o_ref[...] = jnp.where((row == 0) & bwd_col, 0.0, filt)


def _filter_features(n):
    t = jnp.linspace(0.0, 1.0, n, dtype=F32)[:, None]
    wpos = 2.0 * math.pi * jnp.arange(n, dtype=F32)[:, None] / n
    bands = jnp.linspace(1e-4, HY_BANDS - 1, HY_BANDS, dtype=F32)[None, :]
    feats = jnp.concatenate([t, jnp.cos(bands * wpos), -jnp.sin(bands * wpos)], axis=-1)
    return jnp.pad(feats, ((0, 0), (0, HY_EMB_PAD - HY_EMB)))


def _filter_decay_rates():
    deltas = jnp.abs(jnp.linspace(math.log(HY_TARGET) / HY_FAST, math.log(HY_TARGET) / HY_SLOW, W_C, dtype=F32))
    return jnp.tile(deltas, 2 * HY_ORDER).reshape(1, 2 * HY_ORDER * W_C)


def _filters(n, w1, b1, freq, w2, b2, w3):
    tm = min(n, 512)
    nf = 2 * HY_ORDER * W_C
    const = lambda shape: pl.BlockSpec(shape, lambda i: (0, 0))
    return pl.pallas_call(
        functools.partial(_filter_kernel, tm=tm),
        grid=(n // tm,),
        in_specs=[pl.BlockSpec((tm, HY_EMB_PAD), lambda i: (i, 0)),
                  const((HY_EMB_PAD, HY_FH)), const((1, HY_FH)), const((1, HY_FH)),
                  const((HY_FH, HY_FH)), const((1, HY_FH)), const((HY_FH, nf)), const((1, nf))],
        out_specs=pl.BlockSpec((tm, nf), lambda i: (i, 0)),
        out_shape=jax.ShapeDtypeStruct((n, nf), F32),
        compiler_params=_params("arbitrary"),
        name="hyena_filter",
    )(_filter_features(n), jnp.pad(w1, ((0, HY_EMB_PAD - HY_EMB), (0, 0))), b1.reshape(1, HY_FH),
      freq.reshape(1, HY_FH), w2, b2.reshape(1, HY_FH), w3, _filter_decay_rates())


T2_BLOCKS = FFT_MINOR // V7X_SUBLANES


class _FftPlan:
    def __init__(self, n):
        assert n % (4 * FFT_MINOR) == 0
        self.n = n
        self.n_fft = 2 * n
        self.n1 = self.n_fft // FFT_MINOR
        self.t1 = self.n1 // 2
        self.k1 = self.n1 // 2 + 1
        self.k_block = max(d for d in range(1, 14) if self.k1 % d == 0)
        t1 = np.arange(self.t1)
        k1 = np.arange(self.k1)
        eye = np.eye(V7X_SUBLANES)
        ang = 2.0 * np.pi * ((k1[:, None] * t1[None, :]) % self.n1) / self.n1
        fa = np.concatenate([np.cos(ang), -np.sin(ang)], axis=0)
        self.stage_a = _bf16_table(np.kron(fa, eye))
        wgt = np.where((k1 == 0) | (k1 == self.n1 // 2), 1.0, 2.0) / self.n_fft
        fc = np.concatenate([np.cos(ang) * wgt[:, None], -np.sin(ang) * wgt[:, None]], axis=0).T
        self.stage_a_inv = _bf16_table(np.kron(fc, eye))
        t2 = np.arange(FFT_MINOR)
        k2 = np.arange(FFT_MINOR)
        idx = (t2[None, None, :] * (k1[:, None, None] + self.n1 * k2[None, :, None])) % self.n_fft
        ang_b = 2.0 * np.pi * idx / self.n_fft
        gr, gi = np.cos(ang_b), -np.sin(ang_b)
        fwd = np.concatenate([np.concatenate([gr, -gi], axis=2), np.concatenate([gi, gr], axis=2)], axis=1)
        self.stage_b = _bf16_table(fwd)
        self.stage_b_inv = _bf16_table(np.transpose(fwd, (0, 2, 1)))

    def scratch(self, signals):
        return pltpu.VMEM((signals, 2, self.k1, T2_BLOCKS, V7X_SUBLANES, V7X_LANES), F32)


def _time_tiles(ref, tb, rows):
    return ref[:, pl.ds(tb, 1)].reshape(rows, V7X_LANES)


def _stage_a_forward(plan, x_refs, fa_ref, a_s):
    def body(tb, carry):
        xs = jnp.concatenate([_time_tiles(xr, tb, plan.t1 * V7X_SUBLANES) for xr in x_refs], axis=1)
        av = jnp.dot(fa_ref[...], xs.astype(BF16), preferred_element_type=F32)
        for j in range(len(x_refs)):
            part = av[:, j * V7X_LANES:(j + 1) * V7X_LANES]
            a_s[j, :, :, pl.ds(tb, 1)] = part.reshape(2, plan.k1, 1, V7X_SUBLANES, V7X_LANES)
        return carry
    lax.fori_loop(0, T2_BLOCKS, body, 0)


def _stage_b_slab(a_s, k, count):
    return jnp.concatenate(
        [a_s[j, :, pl.ds(k, 1)].reshape(2 * FFT_MINOR, V7X_LANES) for j in range(count)], axis=1)


def _spectrum_kernel(x0_ref, x1_ref, fa_ref, fb_ref, o_ref, a_s, *, plan):
    kb = pl.program_id(1)

    @pl.when(kb == 0)
    def _():
        _stage_a_forward(plan, (x0_ref, x1_ref), fa_ref, a_s)

    half = FFT_MINOR
    rows = 2 * half
    for i in range(plan.k_block):
        slab = _stage_b_slab(a_s, kb * plan.k_block + i, 2).astype(BF16)
        xs = jnp.dot(fb_ref[i], slab, preferred_element_type=F32)
        o_ref[i * rows:i * rows + half, :] = xs[:half, :V7X_LANES] + xs[:half, V7X_LANES:]
        o_ref[i * rows + half:(i + 1) * rows, :] = xs[half:, :V7X_LANES] - xs[half:, V7X_LANES:]


def _filter_spectrum(plan, filt):
    n, nf = filt.shape
    rows = 2 * FFT_MINOR
    kbs = plan.k_block
    n_cb = W_C // V7X_LANES
    filt = filt.reshape(plan.t1, T2_BLOCKS, V7X_SUBLANES, nf)
    sig = (plan.t1, T2_BLOCKS, V7X_SUBLANES, V7X_LANES)
    return pl.pallas_call(
        functools.partial(_spectrum_kernel, plan=plan),
        grid=(HY_ORDER * n_cb, plan.k1 // kbs),
        in_specs=[_resident(sig, lambda c, k: (0, 0, 0, 2 * (c // n_cb) * n_cb + c % n_cb)),
                  _resident(sig, lambda c, k: (0, 0, 0, (2 * (c // n_cb) + 1) * n_cb + c % n_cb)),
                  pl.BlockSpec(plan.stage_a.shape, lambda c, k: (0, 0)),
                  pl.BlockSpec((kbs, rows, rows), lambda c, k: (k, 0, 0))],
        out_specs=pl.BlockSpec((kbs * rows, V7X_LANES), lambda c, k: (k, c)),
        out_shape=jax.ShapeDtypeStruct((plan.k1 * rows, HY_ORDER * W_C), F32),
        scratch_shapes=[plan.scratch(2)],
        compiler_params=_params("arbitrary", "arbitrary"),
        name="filter_spectrum",
    )(filt, filt, plan.stage_a, plan.stage_b)


def _conv_kernel(v_ref, g_ref, h_ref, fa_ref, fb_ref, fbi_ref, fai_ref, sk_ref, o_ref, a_s, *, plan, bn):
    kb = pl.program_id(1)
    half = FFT_MINOR
    rows = 2 * half

    @pl.when(kb == 0)
    def _():
        _stage_a_forward(plan, [v_ref.at[b] for b in range(bn)], fa_ref, a_s)

    for i in range(plan.k_block):
        k = kb * plan.k_block + i
        xs = jnp.dot(fb_ref[i], _stage_b_slab(a_s, k, bn).astype(BF16), preferred_element_type=F32)
        xr, xi = xs[:half], xs[half:]
        hr = jnp.concatenate([h_ref[i * rows:i * rows + half, :]] * bn, axis=1)
        hi = jnp.concatenate([h_ref[i * rows + half:(i + 1) * rows, :]] * bn, axis=1)
        ys = jnp.concatenate([xr * hr - xi * hi, xr * hi + xi * hr], axis=0)
        ds = jnp.dot(fbi_ref[i], ys.astype(BF16), preferred_element_type=F32)
        for b in range(bn):
            part = ds[:, b * V7X_LANES:(b + 1) * V7X_LANES]
            a_s[b, :, pl.ds(k, 1)] = part.reshape(2, 1, T2_BLOCKS, V7X_SUBLANES, V7X_LANES)

    @pl.when(kb == pl.num_programs(1) - 1)
    def _():
        t_rows = plan.t1 * V7X_SUBLANES

        def body(tb, carry):
            ds = jnp.concatenate(
                [a_s[b, :, :, pl.ds(tb, 1)].reshape(2 * plan.k1 * V7X_SUBLANES, V7X_LANES) for b in range(bn)],
                axis=1)
            y = jnp.dot(fai_ref[...], ds.astype(BF16), preferred_element_type=F32)
            for b in range(bn):
                vv = _time_tiles(v_ref.at[b], tb, t_rows)
                gg = _time_tiles(g_ref.at[b], tb, t_rows)
                res = gg * (y[:, b * V7X_LANES:(b + 1) * V7X_LANES] + sk_ref[...] * vv)
                o_ref[b, :, pl.ds(tb, 1)] = res.reshape(plan.t1, 1, V7X_SUBLANES, V7X_LANES)
            return carry
        lax.fori_loop(0, T2_BLOCKS, body, 0)


def _hyena_conv(plan, v_arr, v_col, g_arr, g_col, spec, order, skip):
    bn = v_arr.shape[0]
    cb = V7X_LANES
    rows = 2 * FFT_MINOR
    n_cb = W_C // cb
    kbs = plan.k_block
    sig = (bn, plan.t1, T2_BLOCKS, V7X_SUBLANES, cb)
    return pl.pallas_call(
        functools.partial(_conv_kernel, plan=plan, bn=bn),
        grid=(n_cb, plan.k1 // kbs),
        in_specs=[_resident(sig, lambda c, k: (0, 0, 0, 0, v_col + c)),
                  _resident(sig, lambda c, k: (0, 0, 0, 0, g_col + c)),
                  pl.BlockSpec((kbs * rows, cb), lambda c, k: (k, order * n_cb + c)),
                  pl.BlockSpec(plan.stage_a.shape, lambda c, k: (0, 0)),
                  pl.BlockSpec((kbs, rows, rows), lambda c, k: (k, 0, 0)),
                  pl.BlockSpec((kbs, rows, rows), lambda c, k: (k, 0, 0)),
                  pl.BlockSpec(plan.stage_a_inv.shape, lambda c, k: (0, 0)),
                  pl.BlockSpec((1, cb), lambda c, k: (0, c))],
        out_specs=_resident(sig, lambda c, k: (0, 0, 0, 0, c)),
        out_shape=jax.ShapeDtypeStruct((bn, plan.t1, T2_BLOCKS, V7X_SUBLANES, W_C), F32),
        scratch_shapes=[plan.scratch(bn)],
        compiler_params=_params("arbitrary", "arbitrary"),
        name="hyena_conv",
    )(v_arr, g_arr, spec, plan.stage_a, plan.stage_b, plan.stage_b_inv, plan.stage_a_inv,
      skip.reshape(1, W_C))


def _hyena(z, n_fft_len, filt_args, skip):
    bn, n, _ = z.shape
    plan = _FftPlan(n_fft_len)
    filt = _filters(n, *filt_args)
    if n_fft_len != n:
        filt = jnp.pad(filt, ((0, n_fft_len - n), (0, 0)))
        z = jnp.pad(z, ((0, 0), (0, n_fft_len - n), (0, 0)))
    spec = _filter_spectrum(plan, filt)
    n_cb = W_C // V7X_LANES
    z = z.reshape(bn, plan.t1, T2_BLOCKS, V7X_SUBLANES, 3 * W_C)
    y1 = _hyena_conv(plan, z, 0, z, n_cb, spec, 0, skip[0])
    y2 = _hyena_conv(plan, y1, 0, z, 2 * n_cb, spec, 1, skip[1])
    return y2.reshape(bn, n_fft_len, W_C)[:, :n]


def _dense_block_diag(w):
    h, c, d = w.shape[-3:]
    eye = jnp.eye(h, dtype=w.dtype)
    dense = jnp.einsum('...hcd,hk->...hckd', w, eye)
    return dense.reshape(w.shape[:-3] + (h * c, h * d))


def _tile_rows(n):
    return min(n, 512)


def kernel(x, c, ctx, c_ctx, w_mod, b_mod, norm_g, ffn_w_in, ffn_w_out, w_in, w_out, lru_conv_w, lru_conv_b,
           lru_gate_w, lru_gate_b, lru_lambda, pool_w, pool_b, pool_scale, hy_conv_w, hy_conv_b, hy_w1, hy_b1,
           hy_freq, hy_w2, hy_b2, hy_w3, hy_skip, final_g):
    bn, n, _ = x.shape
    n_ctx = ctx.shape[1]
    depth = w_mod.shape[0]
    assert bn + 1 <= V7X_SUBLANES and n % GRID_W == 0 and n % POOL_TILE == 0 and n_ctx == POOL_TILE
    tm_x, tm_c = _tile_rows(n), _tile_rows(n_ctx)

    cvec = jnp.zeros((V7X_SUBLANES, D_MODEL), F32).at[:bn].set(c).at[bn].set(c_ctx)
    mods = _modulation(cvec, w_mod, b_mod)
    band_x, inv_x = _pool_tables(GRID_W)
    band_c, inv_c = _pool_tables(n_ctx)
    zeros = jnp.zeros((bn, W_A), F32)

    xc = ctx
    for l in range(depth):
        last = l == depth - 1
        mx = mods[l, :bn].reshape(bn, N_MOD, D_MODEL)
        mc = jnp.broadcast_to(mods[l, bn].reshape(1, N_MOD, D_MODEL), (bn, N_MOD, D_MODEL))
        f_in = [ffn_w_in[l, k].astype(BF16) for k in range(2)]
        f_out = [ffn_w_out[l, k].astype(BF16) for k in range(2)]
        w_in_l = w_in[l].astype(BF16)
        w_out_l = w_out[l].astype(BF16)
        gate_dense = _dense_block_diag(lru_gate_w[l])
        wg = [jnp.concatenate([gate_dense[d, 0], gate_dense[d, 1]], axis=-1).astype(BF16) for d in range(2)]
        bg = [lru_gate_b[l, d].reshape(1, 2 * W_A) for d in range(2)]
        lam = [lru_lambda[l, d].reshape(1, W_A) for d in range(2)]
        pw = _dense_block_diag(pool_w[l]).astype(BF16)
        filt_args = (hy_w1[l], hy_b1[l], hy_freq[l], hy_w2[l], hy_b2[l], hy_w3[l])
        proj_args = (w_in_l, lru_conv_w[l], lru_conv_b[l], hy_conv_w[l], hy_conv_b[l])

        def mixer(xs, mod, tm, h0f, h0b, band, inv, fft_len, full):
            ua, ga, up, z = _proj(xs, mod, norm_g[l, 1], *proj_args, tm=tm)
            hf = _lru_fwd(ua, wg[0], bg[0], lam[0], h0f, tm=tm)
            ya, hend = _lru_bwd(ua, wg[1], bg[1], lam[1], h0b, hf, ga, tm=tm)
            if not full:
                return None, hf[:, -1], hend
            yc = _hyena(z, fft_len, filt_args, hy_skip[l])
            return (ya, up, yc, band, inv, pw, pool_b[l], pool_scale[l], w_out_l), hf[:, -1], hend

        x = _ffn(x, mx, norm_g[l, 0], f_in[0], f_out[0], final_g, j0=0, final=False, tm=tm_x)
        xc = _ffn(xc, mc, norm_g[l, 0], f_in[0], f_out[0], final_g, j0=0, final=False, tm=tm_c)
        mix_c, h0f, h0b = mixer(xc, mc, tm_c, zeros, zeros, band_c, inv_c, CTX_FFT_LEN, not last)
        mix_x, _, _ = mixer(x, mx, tm_x, h0f, h0b, band_x, inv_x, n, True)
        x = _ffn(x, mx, norm_g[l, 2], f_in[1], f_out[1], final_g, mix_x, j0=6, final=last, tm=tm_x)
        if not last:
            xc = _ffn(xc, mc, norm_g[l, 2], f_in[1], f_out[1], final_g, mix_c, j0=6, final=False, tm=tm_c)
    return x
```

```python
import functools
import math

import numpy as np
import jax
import jax.numpy as jnp
from jax import lax
from jax.experimental import pallas as pl
from jax.experimental.pallas import tpu as pltpu

F32 = jnp.float32
BF16 = jnp.bfloat16

D_MODEL = 1024
GRID_W = 64
N_MOD = 9
W_A, W_B, W_C = 512, 256, 256
D_IN = 2 * W_A + W_B + 3 * W_C
RG_HEADS, RG_HD, RG_CONV, RG_C = 8, 64, 4, 8.0
POOL_WINDOWS = (2, 4, 8, 16)
POOL_GW = W_B // len(POOL_WINDOWS)
HY_ORDER, HY_SHORT, HY_BANDS, HY_FH = 2, 3, 16, 64
HY_EMB = 2 * HY_BANDS + 1
HY_EMB_PAD = 40
HY_TARGET, HY_FAST, HY_SLOW = 1e-2, 0.3, 1.5
D_FF = 2816
EPS = 1e-6

V7X_LANES = 128
V7X_SUBLANES = 8
V7X_MXU_DIM = 256
V7X_VMEM_LIMIT_BYTES = 56 * 1024 * 1024

FFT_MINOR = V7X_LANES
HALO = V7X_SUBLANES
FF_CHUNKS = ((0, 768), (768, 768), (1536, 768), (2304, 512))
POOL_TILE = 256
CTX_FFT_LEN = 1024


def _params(*sem):
    return pltpu.CompilerParams(dimension_semantics=sem, vmem_limit_bytes=V7X_VMEM_LIMIT_BYTES)


def _resident(shape, index_map):
    return pl.BlockSpec(shape, index_map, pipeline_mode=pl.Buffered(1))


def _bf16_table(values):
    return jnp.asarray(values, F32).astype(BF16)


def _split_bf16(x):
    hi = x.astype(BF16)
    return hi, (x - hi.astype(F32)).astype(BF16)


def _dot_bf16x3(a, b):
    a_hi, a_lo = _split_bf16(a)
    b_hi, b_lo = _split_bf16(b)
    d = lambda p, q: jnp.dot(p, q, preferred_element_type=F32)
    return d(a_hi, b_hi) + (d(a_lo, b_hi) + d(a_hi, b_lo))


def _adanorm(x, g, shift, scale):
    ms = jnp.mean(x * x, axis=-1, keepdims=True)
    return (x * lax.rsqrt(ms + EPS)) * g * (1.0 + scale) + shift


def _mod_kernel(c_ref, w_ref, b_ref, o_ref):
    s = c_ref[...]
    s = s * jax.nn.sigmoid(s)
    o_ref[0] = jnp.dot(s.astype(BF16), w_ref[0].astype(BF16), preferred_element_type=F32) + b_ref[0]


def _modulation(cvec, w_mod, b_mod):
    depth, _, n = w_mod.shape
    tn = 1024
    return pl.pallas_call(
        _mod_kernel,
        grid=(depth, n // tn),
        in_specs=[pl.BlockSpec((V7X_SUBLANES, D_MODEL), lambda l, j: (0, 0)),
                  pl.BlockSpec((1, D_MODEL, tn), lambda l, j: (l, 0, j)),
                  pl.BlockSpec((1, 1, tn), lambda l, j: (l, 0, j))],
        out_specs=pl.BlockSpec((1, V7X_SUBLANES, tn), lambda l, j: (l, 0, j)),
        out_shape=jax.ShapeDtypeStruct((depth, V7X_SUBLANES, n), F32),
        compiler_params=_params("arbitrary", "arbitrary"),
        name="modulation",
    )(cvec, w_mod, b_mod.reshape(depth, 1, n))


def _pool_tile(x, band_ref, inv_ref, w_ref, b_ref, s_ref):
    xb = x.astype(BF16)
    col = lax.broadcasted_iota(jnp.int32, x.shape, 1) // POOL_GW
    tot = jnp.zeros_like(x)
    for g in range(len(POOL_WINDOWS)):
        tot = jnp.where(col == g, jnp.dot(band_ref[g], xb, preferred_element_type=F32), tot)
    pooled = tot * inv_ref[...] - x
    y = jnp.dot(pooled.astype(BF16), w_ref[...], preferred_element_type=F32)
    return (y + b_ref[...]) * s_ref[...]


def _ffn_kernel(x_ref, m_ref, g_ref, win_ref, wout_ref, fg_ref, *rest, j0, final, mix, tm):
    o_ref = rest[-1]
    x = x_ref[0]
    if mix:
        ya_ref, up_ref, yc_ref, band_ref, inv_ref, pw_ref, pb_ref, ps_ref, wo_ref = rest[:-1]
        yb = jnp.concatenate([_pool_tile(up_ref[0, r:r + POOL_TILE, :], band_ref, inv_ref, pw_ref, pb_ref, ps_ref)
                              for r in range(0, tm, POOL_TILE)], axis=0)
        y = jnp.dot(ya_ref[0].astype(BF16), wo_ref[0:W_A, :], preferred_element_type=F32)
        y = y + jnp.dot(yb.astype(BF16), wo_ref[W_A:W_A + W_B, :], preferred_element_type=F32)
        y = y + jnp.dot(yc_ref[0].astype(BF16), wo_ref[W_A + W_B:, :], preferred_element_type=F32)
        x = x + m_ref[0, 5:6, :] * y
    shift = m_ref[0, j0:j0 + 1, :]
    scale = m_ref[0, j0 + 1:j0 + 2, :]
    gate = m_ref[0, j0 + 2:j0 + 3, :]
    h = _adanorm(x, g_ref[...], shift, scale).astype(BF16)
    acc = None
    for s, w in FF_CHUNKS:
        gt = jnp.dot(h, win_ref[:, s:s + w], preferred_element_type=F32)
        up = jnp.dot(h, win_ref[:, D_FF + s:D_FF + s + w], preferred_element_type=F32)
        a = (gt * jax.nn.sigmoid(gt) * up).astype(BF16)
        p = jnp.dot(a, wout_ref[s:s + w, :], preferred_element_type=F32)
        acc = p if acc is None else acc + p
    y = x + (0.5 * gate) * acc
    if final:
        ms = jnp.mean(y * y, axis=-1, keepdims=True)
        y = (y * lax.rsqrt(ms + EPS)) * fg_ref[...]
    o_ref[0] = y


def _ffn(x, mod, g, w_in, w_out, wsel, final_g, mix=None, *, j0, final, tm):
    bn, n, _ = x.shape
    tile = lambda w: pl.BlockSpec((1, tm, w), lambda b, i: (b, i, 0))
    const = lambda shape: pl.BlockSpec(shape, lambda b, i: (0,) * len(shape))
    in_specs = [tile(D_MODEL),
                pl.BlockSpec((1, N_MOD, D_MODEL), lambda b, i: (b, 0, 0)),
                const((1, D_MODEL)),
                _resident((None, None, D_MODEL, 2 * D_FF), lambda b, i: wsel + (0, 0)),
                _resident((None, None, D_FF, D_MODEL), lambda b, i: wsel + (0, 0)),
                const((1, D_MODEL))]
    args = [x, mod, g.reshape(1, D_MODEL), w_in, w_out, final_g.reshape(1, D_MODEL)]
    if mix is not None:
        ya, up, yc, band, inv, pw, pb, ps, wo = mix
        in_specs += [tile(W_A), tile(W_B), tile(W_C), const(band.shape), const(inv.shape), const((W_B, W_B)),
                     const((1, W_B)), const((1, W_B)), _resident((D_MODEL, D_MODEL), lambda b, i: (0, 0))]
        args += [ya, up, yc, band, inv, pw, pb.reshape(1, W_B), ps.reshape(1, W_B), wo]
    return pl.pallas_call(
        functools.partial(_ffn_kernel, j0=j0, final=final, mix=mix is not None, tm=tm),
        grid=(bn, n // tm),
        in_specs=in_specs,
        out_specs=tile(D_MODEL),
        out_shape=jax.ShapeDtypeStruct(x.shape, F32),
        compiler_params=_params("arbitrary", "arbitrary"),
        name="mix_ffn" if mix is not None else "ffn",
    )(*args)


def _proj_kernel(xp_ref, x_ref, xn_ref, m_ref, g_ref, w_ref, cwa_ref, cba_ref, cwh_ref, cbh_ref,
                 ua_ref, ga_ref, up_ref, z_ref, pa_s, ph_s, *, tm, n_tiles):
    i = pl.program_id(1)
    norm = lambda v: _adanorm(v, g_ref[...], m_ref[0, 3:4, :], m_ref[0, 4:5, :])
    h_prev = jnp.where(i > 0, norm(xp_ref[0]), 0.0)
    h_next = jnp.where(i < n_tiles - 1, norm(xn_ref[0]), 0.0)
    h = jnp.concatenate([h_prev, norm(x_ref[0]), h_next], axis=0).astype(BF16)
    proj = lambda lo, hi: jnp.dot(h, w_ref[:, lo:hi], preferred_element_type=F32)

    def short_conv(lo, hi, w_conv_ref, b_conv_ref, taps, pad_left, p_s, out_ref):
        p_s[...] = proj(lo, hi)
        acc = b_conv_ref[...] + w_conv_ref[0:1, :] * p_s[pl.ds(HALO - pad_left, tm), :]
        for k in range(1, taps):
            acc = acc + w_conv_ref[k:k + 1, :] * p_s[pl.ds(HALO - pad_left + k, tm), :]
        out_ref[0] = acc

    short_conv(2 * W_A + W_B, D_IN, cwh_ref, cbh_ref, HY_SHORT, HY_SHORT // 2, ph_s, z_ref)
    short_conv(0, W_A, cwa_ref, cba_ref, RG_CONV, RG_CONV // 2, pa_s, ua_ref)
    rest = proj(W_A, 2 * W_A + W_B)
    ga_ref[0] = rest[HALO:HALO + tm, :W_A]
    up_ref[0] = rest[HALO:HALO + tm, W_A:]


def _proj(x, mod, g, w_in, cwa, cba, cwh, cbh, *, tm):
    bn, n, _ = x.shape
    n_tiles = n // tm
    hb = tm // HALO
    last_hb = n // HALO - 1
    outs = (W_A, W_A, W_B, 3 * W_C)
    return pl.pallas_call(
        functools.partial(_proj_kernel, tm=tm, n_tiles=n_tiles),
        grid=(bn, n_tiles),
        in_specs=[pl.BlockSpec((1, HALO, D_MODEL), lambda b, i: (b, jnp.maximum(i * hb - 1, 0), 0)),
                  pl.BlockSpec((1, tm, D_MODEL), lambda b, i: (b, i, 0)),
                  pl.BlockSpec((1, HALO, D_MODEL), lambda b, i: (b, jnp.minimum((i + 1) * hb, last_hb), 0)),
                  pl.BlockSpec((1, N_MOD, D_MODEL), lambda b, i: (b, 0, 0)),
                  pl.BlockSpec((1, D_MODEL), lambda b, i: (0, 0)),
                  _resident((D_MODEL, D_IN), lambda b, i: (0, 0)),
                  pl.BlockSpec((RG_CONV, W_A), lambda b, i: (0, 0)),
                  pl.BlockSpec((1, W_A), lambda b, i: (0, 0)),
                  pl.BlockSpec((HY_SHORT, 3 * W_C), lambda b, i: (0, 0)),
                  pl.BlockSpec((1, 3 * W_C), lambda b, i: (0, 0))],
        out_specs=[pl.BlockSpec((1, tm, w), lambda b, i: (b, i, 0)) for w in outs],
        out_shape=[jax.ShapeDtypeStruct((bn, n, w), F32) for w in outs],
        scratch_shapes=[pltpu.VMEM((tm + 2 * HALO, W_A), F32), pltpu.VMEM((tm + 2 * HALO, 3 * W_C), F32)],
        compiler_params=_params("arbitrary", "arbitrary"),
        name="proj",
    )(x, x, x, mod, g.reshape(1, D_MODEL), w_in, cwa, cba.reshape(1, W_A), cwh, cbh.reshape(1, 3 * W_C))


def _lru_coeffs(u, wg_ref, bg_ref, lam_ref, a_s, b_s):
    pre = jnp.dot(u.astype(BF16), wg_ref[...], preferred_element_type=F32) + bg_ref[...]
    q = (-0.25 * RG_C) * jax.nn.softplus(-lam_ref[...])
    t = jnp.tanh(q * jnp.tanh(0.5 * pre[:, :W_A]) + q)
    inv = 1.0 / (1.0 - t)
    a_s[...] = (1.0 + t) * inv
    b_s[...] = (jnp.sqrt(-t) * inv) * ((jnp.tanh(0.5 * pre[:, W_A:]) + 1.0) * u)


def _lru_scan_tile(a_s, b_s, carry, *, tm, reverse):
    groups = tm // V7X_SUBLANES
    rowi = lax.broadcasted_iota(jnp.int32, (V7X_SUBLANES, W_A), 0)

    def body(k, c):
        gi = (groups - 1 - k) if reverse else k
        r0 = pl.multiple_of(gi * V7X_SUBLANES, V7X_SUBLANES)
        a = a_s[pl.ds(r0, V7X_SUBLANES), :]
        b = b_s[pl.ds(r0, V7X_SUBLANES), :]
        first = rowi == (V7X_SUBLANES - 1 if reverse else 0)
        b = jnp.where(first, a * c + b, b)
        for s in (1, 2, 4):
            if reverse:
                shift, m = V7X_SUBLANES - s, rowi < V7X_SUBLANES - s
            else:
                shift, m = s, rowi >= s
            b = jnp.where(m, a * pltpu.roll(b, shift, 0) + b, b)
            if s < V7X_SUBLANES // 2:
                a = jnp.where(m, a * pltpu.roll(a, shift, 0), a)
        b_s[pl.ds(r0, V7X_SUBLANES), :] = b
        return b[0:1, :] if reverse else b[V7X_SUBLANES - 1:V7X_SUBLANES, :]

    return lax.fori_loop(0, groups, body, carry, unroll=4)


def _lru_fwd_kernel(u_ref, wg_ref, bg_ref, lam_ref, h0_ref, hf_ref, a_s, b_s, c_s, *, tm):
    @pl.when(pl.program_id(1) == 0)
    def _():
        c_s[...] = h0_ref[0]

    _lru_coeffs(u_ref[0], wg_ref, bg_ref, lam_ref, a_s, b_s)
    c_s[...] = _lru_scan_tile(a_s, b_s, c_s[...], tm=tm, reverse=False)
    hf_ref[0] = b_s[...]


def _lru_bwd_kernel(u_ref, wg_ref, bg_ref, lam_ref, h0_ref, hf_ref, ga_ref, ya_ref, hend_ref,
                    a_s, b_s, c_s, *, tm):
    @pl.when(pl.program_id(1) == 0)
    def _():
        c_s[...] = h0_ref[0]

    _lru_coeffs(u_ref[0], wg_ref, bg_ref, lam_ref, a_s, b_s)
    c_s[...] = _lru_scan_tile(a_s, b_s, c_s[...], tm=tm, reverse=True)
    hend_ref[0] = c_s[...]
    ya_ref[0] = (hf_ref[0] + b_s[...]) * jax.nn.gelu(ga_ref[0])


def _lru_common_specs(tm, tile_map):
    return [pl.BlockSpec((1, tm, W_A), tile_map),
            pl.BlockSpec((W_A, 2 * W_A), lambda b, i: (0, 0)),
            pl.BlockSpec((1, 2 * W_A), lambda b, i: (0, 0)),
            pl.BlockSpec((1, W_A), lambda b, i: (0, 0)),
            pl.BlockSpec((1, 1, W_A), lambda b, i: (b, 0, 0))]


def _lru_scratch(tm):
    return [pltpu.VMEM((tm, W_A), F32), pltpu.VMEM((tm, W_A), F32), pltpu.VMEM((1, W_A), F32)]


def _lru_fwd(u, wg, bg, lam, h0, *, tm):
    bn, n, _ = u.shape
    tile_map = lambda b, i: (b, i, 0)
    return pl.pallas_call(
        functools.partial(_lru_fwd_kernel, tm=tm),
        grid=(bn, n // tm),
        in_specs=_lru_common_specs(tm, tile_map),
        out_specs=pl.BlockSpec((1, tm, W_A), tile_map),
        out_shape=jax.ShapeDtypeStruct(u.shape, F32),
        scratch_shapes=_lru_scratch(tm),
        compiler_params=_params("arbitrary", "arbitrary"),
        name="lru_fwd",
    )(u, wg, bg, lam, h0.reshape(bn, 1, W_A))


def _lru_bwd(u, wg, bg, lam, h0, hf, ga, *, tm):
    bn, n, _ = u.shape
    last = n // tm - 1
    tile_map = lambda b, i: (b, last - i, 0)
    ya, hend = pl.pallas_call(
        functools.partial(_lru_bwd_kernel, tm=tm),
        grid=(bn, n // tm),
        in_specs=_lru_common_specs(tm, tile_map) + [pl.BlockSpec((1, tm, W_A), tile_map),
                                                    pl.BlockSpec((1, tm, W_A), tile_map)],
        out_specs=[pl.BlockSpec((1, tm, W_A), tile_map),
                   pl.BlockSpec((1, 1, W_A), lambda b, i: (b, 0, 0))],
        out_shape=[jax.ShapeDtypeStruct(u.shape, F32), jax.ShapeDtypeStruct((bn, 1, W_A), F32)],
        scratch_shapes=_lru_scratch(tm),
        compiler_params=_params("arbitrary", "arbitrary"),
        name="lru_bwd",
    )(u, wg, bg, lam, h0.reshape(bn, 1, W_A), hf, ga)
    return ya, hend.reshape(bn, W_A)


def _pool_tables(row_len):
    t = np.arange(POOL_TILE)
    p = t % row_len
    base = t - p
    band = np.zeros((len(POOL_WINDOWS), POOL_TILE, POOL_TILE), np.float32)
    inv = np.zeros((POOL_TILE, W_B), np.float32)
    for g, win in enumerate(POOL_WINDOWS):
        lo = np.clip(p - win // 2, 0, row_len)
        hi = np.clip(p + win - win // 2, 0, row_len)
        s = np.arange(POOL_TILE)[None, :]
        band[g] = ((s >= (base + lo)[:, None]) & (s < (base + hi)[:, None])).astype(np.float32)
        inv[:, g * POOL_GW:(g + 1) * POOL_GW] = (1.0 / (hi - lo).astype(np.float64))[:, None]
    return _bf16_table(band), jnp.asarray(inv, F32)


def _filter_kernel(f_ref, w1_ref, b1_ref, fr_ref, w2_ref, b2_ref, w3_ref, dl_ref, o_ref, *, tm):
    feats = f_ref[...]
    fr = fr_ref[...]
    hid = jnp.sin(fr * (_dot_bf16x3(feats, w1_ref[...]) + b1_ref[...]))
    hid = jnp.sin(fr * (_dot_bf16x3(hid, w2_ref[...]) + b2_ref[...]))
    filt = _dot_bf16x3(hid, w3_ref[...])
    filt = filt * jnp.exp(-feats[:, 0:1] * dl_ref[...])
    row = pl.program_id(0) * tm + lax.broadcasted_iota(jnp.int32, filt.shape, 0)
    bwd_col = (lax.broadcasted_iota(jnp.int32, filt.shape, 1) // W_C) % 2 == 1
    o_ref[...] = jnp.where((row == 0) & bwd_col, 0.0, filt)


def _filter_features(n):
    t = jnp.linspace(0.0, 1.0, n, dtype=F32)[:, None]
    wpos = 2.0 * math.pi * jnp.arange(n, dtype=F32)[:, None] / n
    bands = jnp.linspace(1e-4, HY_BANDS - 1, HY_BANDS, dtype=F32)[None, :]
    feats = jnp.concatenate([t, jnp.cos(bands * wpos), -jnp.sin(bands * wpos)], axis=-1)
    return jnp.pad(feats, ((0, 0), (0, HY_EMB_PAD - HY_EMB)))


def _filter_decay_rates():
    deltas = jnp.abs(jnp.linspace(math.log(HY_TARGET) / HY_FAST, math.log(HY_TARGET) / HY_SLOW, W_C, dtype=F32))
    return jnp.tile(deltas, 2 * HY_ORDER).reshape(1, 2 * HY_ORDER * W_C)


def _filters(n, w1, b1, freq, w2, b2, w3):
    tm = min(n, 512)
    nf = 2 * HY_ORDER * W_C
    const = lambda shape: pl.BlockSpec(shape, lambda i: (0, 0))
    return pl.pallas_call(
        functools.partial(_filter_kernel, tm=tm),
        grid=(n // tm,),
        in_specs=[pl.BlockSpec((tm, HY_EMB_PAD), lambda i: (i, 0)),
                  const((HY_EMB_PAD, HY_FH)), const((1, HY_FH)), const((1, HY_FH)),
                  const((HY_FH, HY_FH)), const((1, HY_FH)), const((HY_FH, nf)), const((1, nf))],
        out_specs=pl.BlockSpec((tm, nf), lambda i: (i, 0)),
        out_shape=jax.ShapeDtypeStruct((n, nf), F32),
        compiler_params=_params("arbitrary"),
        name="hyena_filter",
    )(_filter_features(n), jnp.pad(w1, ((0, HY_EMB_PAD - HY_EMB), (0, 0))), b1.reshape(1, HY_FH),
      freq.reshape(1, HY_FH), w2, b2.reshape(1, HY_FH), w3, _filter_decay_rates())


T2_BLOCKS = FFT_MINOR // V7X_SUBLANES


class _FftPlan:
    def __init__(self, n):
        assert n % (4 * FFT_MINOR) == 0
        self.n = n
        self.n_fft = 2 * n
        self.n1 = self.n_fft // FFT_MINOR
        self.t1 = self.n1 // 2
        self.k1 = self.n1 // 2 + 1
        self.k_block = max(d for d in range(1, 14) if self.k1 % d == 0)
        t1 = np.arange(self.t1)
        k1 = np.arange(self.k1)
        eye = np.eye(V7X_SUBLANES)
        ang = 2.0 * np.pi * ((k1[:, None] * t1[None, :]) % self.n1) / self.n1
        fa = np.concatenate([np.cos(ang), -np.sin(ang)], axis=0)
        self.stage_a = _bf16_table(np.kron(fa, eye))
        wgt = np.where((k1 == 0) | (k1 == self.n1 // 2), 1.0, 2.0) / self.n_fft
        fc = np.concatenate([np.cos(ang) * wgt[:, None], -np.sin(ang) * wgt[:, None]], axis=0).T
        self.stage_a_inv = _bf16_table(np.kron(fc, eye))
        t2 = np.arange(FFT_MINOR)
        k2 = np.arange(FFT_MINOR)
        idx = (t2[None, None, :] * (k1[:, None, None] + self.n1 * k2[None, :, None])) % self.n_fft
        ang_b = 2.0 * np.pi * idx / self.n_fft
        gr, gi = np.cos(ang_b), -np.sin(ang_b)
        fwd = np.concatenate([np.concatenate([gr, -gi], axis=2), np.concatenate([gi, gr], axis=2)], axis=1)
        self.stage_b = _bf16_table(fwd)
        self.stage_b_inv = _bf16_table(np.transpose(fwd, (0, 2, 1)))

    def scratch(self, signals):
        return pltpu.VMEM((signals, 2, self.k1, T2_BLOCKS, V7X_SUBLANES, V7X_LANES), F32)


def _time_tiles(ref, tb, rows):
    return ref[:, pl.ds(tb, 1)].reshape(rows, V7X_LANES)


def _stage_a_forward(plan, x_refs, fa_ref, a_s):
    def body(tb, carry):
        xs = jnp.concatenate([_time_tiles(xr, tb, plan.t1 * V7X_SUBLANES) for xr in x_refs], axis=1)
        av = jnp.dot(fa_ref[...], xs.astype(BF16), preferred_element_type=F32)
        for j in range(len(x_refs)):
            part = av[:, j * V7X_LANES:(j + 1) * V7X_LANES]
            a_s[j, :, :, pl.ds(tb, 1)] = part.reshape(2, plan.k1, 1, V7X_SUBLANES, V7X_LANES)
        return carry
    lax.fori_loop(0, T2_BLOCKS, body, 0)


def _stage_b_slab(a_s, k, count):
    return jnp.concatenate(
        [a_s[j, :, pl.ds(k, 1)].reshape(2 * FFT_MINOR, V7X_LANES) for j in range(count)], axis=1)


def _spectrum_kernel(x0_ref, x1_ref, fa_ref, fb_ref, o_ref, a_s, *, plan):
    kb = pl.program_id(1)

    @pl.when(kb == 0)
    def _():
        _stage_a_forward(plan, (x0_ref, x1_ref), fa_ref, a_s)

    half = FFT_MINOR
    rows = 2 * half
    for i in range(plan.k_block):
        slab = _stage_b_slab(a_s, kb * plan.k_block + i, 2).astype(BF16)
        xs = jnp.dot(fb_ref[i], slab, preferred_element_type=F32)
        o_ref[i * rows:i * rows + half, :] = xs[:half, :V7X_LANES] + xs[:half, V7X_LANES:]
        o_ref[i * rows + half:(i + 1) * rows, :] = xs[half:, :V7X_LANES] - xs[half:, V7X_LANES:]


def _filter_spectrum(plan, filt):
    n, nf = filt.shape
    rows = 2 * FFT_MINOR
    kbs = plan.k_block
    n_cb = W_C // V7X_LANES
    filt = filt.reshape(plan.t1, T2_BLOCKS, V7X_SUBLANES, nf)
    sig = (plan.t1, T2_BLOCKS, V7X_SUBLANES, V7X_LANES)
    return pl.pallas_call(
        functools.partial(_spectrum_kernel, plan=plan),
        grid=(HY_ORDER * n_cb, plan.k1 // kbs),
        in_specs=[_resident(sig, lambda c, k: (0, 0, 0, 2 * (c // n_cb) * n_cb + c % n_cb)),
                  _resident(sig, lambda c, k: (0, 0, 0, (2 * (c // n_cb) + 1) * n_cb + c % n_cb)),
                  pl.BlockSpec(plan.stage_a.shape, lambda c, k: (0, 0)),
                  pl.BlockSpec((kbs, rows, rows), lambda c, k: (k, 0, 0))],
        out_specs=pl.BlockSpec((kbs * rows, V7X_LANES), lambda c, k: (k, c)),
        out_shape=jax.ShapeDtypeStruct((plan.k1 * rows, HY_ORDER * W_C), F32),
        scratch_shapes=[plan.scratch(2)],
        compiler_params=_params("arbitrary", "arbitrary"),
        name="filter_spectrum",
    )(filt, filt, plan.stage_a, plan.stage_b)


def _conv_kernel(v_ref, g_ref, h_ref, fa_ref, fb_ref, fbi_ref, fai_ref, sk_ref, o_ref, a_s, *, plan, bn):
    kb = pl.program_id(1)
    half = FFT_MINOR
    rows = 2 * half

    @pl.when(kb == 0)
    def _():
        _stage_a_forward(plan, [v_ref.at[b] for b in range(bn)], fa_ref, a_s)

    for i in range(plan.k_block):
        k = kb * plan.k_block + i
        xs = jnp.dot(fb_ref[i], _stage_b_slab(a_s, k, bn).astype(BF16), preferred_element_type=F32)
        xr, xi = xs[:half], xs[half:]
        hr = jnp.concatenate([h_ref[i * rows:i * rows + half, :]] * bn, axis=1)
        hi = jnp.concatenate([h_ref[i * rows + half:(i + 1) * rows, :]] * bn, axis=1)
        ys = jnp.concatenate([xr * hr - xi * hi, xr * hi + xi * hr], axis=0)
        ds = jnp.dot(fbi_ref[i], ys.astype(BF16), preferred_element_type=F32)
        for b in range(bn):
            part = ds[:, b * V7X_LANES:(b + 1) * V7X_LANES]
            a_s[b, :, pl.ds(k, 1)] = part.reshape(2, 1, T2_BLOCKS, V7X_SUBLANES, V7X_LANES)

    @pl.when(kb == pl.num_programs(1) - 1)
    def _():
        t_rows = plan.t1 * V7X_SUBLANES

        def body(tb, carry):
            ds = jnp.concatenate(
                [a_s[b, :, :, pl.ds(tb, 1)].reshape(2 * plan.k1 * V7X_SUBLANES, V7X_LANES) for b in range(bn)],
                axis=1)
            y = jnp.dot(fai_ref[...], ds.astype(BF16), preferred_element_type=F32)
            for b in range(bn):
                vv = _time_tiles(v_ref.at[b], tb, t_rows)
                gg = _time_tiles(g_ref.at[b], tb, t_rows)
                res = gg * (y[:, b * V7X_LANES:(b + 1) * V7X_LANES] + sk_ref[...] * vv)
                o_ref[b, :, pl.ds(tb, 1)] = res.reshape(plan.t1, 1, V7X_SUBLANES, V7X_LANES)
            return carry
        lax.fori_loop(0, T2_BLOCKS, body, 0)


def _hyena_conv(plan, v_arr, v_col, g_arr, g_col, spec, order, skip):
    bn = v_arr.shape[0]
    cb = V7X_LANES
    rows = 2 * FFT_MINOR
    n_cb = W_C // cb
    kbs = plan.k_block
    sig = (bn, plan.t1, T2_BLOCKS, V7X_SUBLANES, cb)
    return pl.pallas_call(
        functools.partial(_conv_kernel, plan=plan, bn=bn),
        grid=(n_cb, plan.k1 // kbs),
        in_specs=[_resident(sig, lambda c, k: (0, 0, 0, 0, v_col + c)),
                  _resident(sig, lambda c, k: (0, 0, 0, 0, g_col + c)),
                  pl.BlockSpec((kbs * rows, cb), lambda c, k: (k, order * n_cb + c)),
                  pl.BlockSpec(plan.stage_a.shape, lambda c, k: (0, 0)),
                  pl.BlockSpec((kbs, rows, rows), lambda c, k: (k, 0, 0)),
                  pl.BlockSpec((kbs, rows, rows), lambda c, k: (k, 0, 0)),
                  pl.BlockSpec(plan.stage_a_inv.shape, lambda c, k: (0, 0)),
                  pl.BlockSpec((1, cb), lambda c, k: (0, c))],
        out_specs=_resident(sig, lambda c, k: (0, 0, 0, 0, c)),
        out_shape=jax.ShapeDtypeStruct((bn, plan.t1, T2_BLOCKS, V7X_SUBLANES, W_C), F32),
        scratch_shapes=[plan.scratch(bn)],
        compiler_params=_params("arbitrary", "arbitrary"),
        name="hyena_conv",
    )(v_arr, g_arr, spec, plan.stage_a, plan.stage_b, plan.stage_b_inv, plan.stage_a_inv,
      skip.reshape(1, W_C))


def _hyena(z, n_fft_len, filt_args, skip):
    bn, n, _ = z.shape
    plan = _FftPlan(n_fft_len)
    filt = _filters(n, *filt_args)
    if n_fft_len != n:
        filt = jnp.pad(filt, ((0, n_fft_len - n), (0, 0)))
        z = jnp.pad(z, ((0, 0), (0, n_fft_len - n), (0, 0)))
    spec = _filter_spectrum(plan, filt)
    n_cb = W_C // V7X_LANES
    z = z.reshape(bn, plan.t1, T2_BLOCKS, V7X_SUBLANES, 3 * W_C)
    y1 = _hyena_conv(plan, z, 0, z, n_cb, spec, 0, skip[0])
    y2 = _hyena_conv(plan, y1, 0, z, 2 * n_cb, spec, 1, skip[1])
    return y2.reshape(bn, n_fft_len, W_C)[:, :n]


def _dense_block_diag(w):
    h, c, d = w.shape[-3:]
    eye = jnp.eye(h, dtype=w.dtype)
    dense = jnp.einsum('...hcd,hk->...hckd', w, eye)
    return dense.reshape(w.shape[:-3] + (h * c, h * d))


def _tile_rows(n):
    return min(n, 512)


def kernel(x, c, ctx, c_ctx, w_mod, b_mod, norm_g, ffn_w_in, ffn_w_out, w_in, w_out, lru_conv_w, lru_conv_b,
           lru_gate_w, lru_gate_b, lru_lambda, pool_w, pool_b, pool_scale, hy_conv_w, hy_conv_b, hy_w1, hy_b1,
           hy_freq, hy_w2, hy_b2, hy_w3, hy_skip, final_g):
    bn, n, _ = x.shape
    n_ctx = ctx.shape[1]
    depth = w_mod.shape[0]
    assert bn + 1 <= V7X_SUBLANES and n % GRID_W == 0 and n % POOL_TILE == 0 and n_ctx == POOL_TILE
    tm_x, tm_c = _tile_rows(n), _tile_rows(n_ctx)

    cvec = jnp.zeros((V7X_SUBLANES, D_MODEL), F32).at[:bn].set(c).at[bn].set(c_ctx)
    mods = _modulation(cvec, w_mod, b_mod)
    band_x, inv_x = _pool_tables(GRID_W)
    band_c, inv_c = _pool_tables(n_ctx)
    zeros = jnp.zeros((bn, W_A), F32)
    f_in = ffn_w_in.astype(BF16)
    f_out = ffn_w_out.astype(BF16)

    xc = ctx
    for l in range(depth):
        last = l == depth - 1
        mx = mods[l, :bn].reshape(bn, N_MOD, D_MODEL)
        mc = jnp.broadcast_to(mods[l, bn].reshape(1, N_MOD, D_MODEL), (bn, N_MOD, D_MODEL))
        w_in_l = w_in[l].astype(BF16)
        w_out_l = w_out[l].astype(BF16)
        gate_dense = _dense_block_diag(lru_gate_w[l])
        wg = [jnp.concatenate([gate_dense[d, 0], gate_dense[d, 1]], axis=-1).astype(BF16) for d in range(2)]
        bg = [lru_gate_b[l, d].reshape(1, 2 * W_A) for d in range(2)]
        lam = [lru_lambda[l, d].reshape(1, W_A) for d in range(2)]
        pw = _dense_block_diag(pool_w[l]).astype(BF16)
        filt_args = (hy_w1[l], hy_b1[l], hy_freq[l], hy_w2[l], hy_b2[l], hy_w3[l])
        proj_args = (w_in_l, lru_conv_w[l], lru_conv_b[l], hy_conv_w[l], hy_conv_b[l])

        def mixer(xs, mod, tm, h0f, h0b, band, inv, fft_len, full):
            ua, ga, up, z = _proj(xs, mod, norm_g[l, 1], *proj_args, tm=tm)
            hf = _lru_fwd(ua, wg[0], bg[0], lam[0], h0f, tm=tm)
            ya, hend = _lru_bwd(ua, wg[1], bg[1], lam[1], h0b, hf, ga, tm=tm)
            if not full:
                return None, hf[:, -1], hend
            yc = _hyena(z, fft_len, filt_args, hy_skip[l])
            return (ya, up, yc, band, inv, pw, pool_b[l], pool_scale[l], w_out_l), hf[:, -1], hend

        x = _ffn(x, mx, norm_g[l, 0], f_in, f_out, (l, 0), final_g, j0=0, final=False, tm=tm_x)
        xc = _ffn(xc, mc, norm_g[l, 0], f_in, f_out, (l, 0), final_g, j0=0, final=False, tm=tm_c)
        mix_c, h0f, h0b = mixer(xc, mc, tm_c, zeros, zeros, band_c, inv_c, CTX_FFT_LEN, not last)
        mix_x, _, _ = mixer(x, mx, tm_x, h0f, h0b, band_x, inv_x, n, True)
        x = _ffn(x, mx, norm_g[l, 2], f_in, f_out, (l, 1), final_g, mix_x, j0=6, final=last, tm=tm_x)
        if not last:
            xc = _ffn(xc, mc, norm_g[l, 2], f_in, f_out, (l, 1), final_g, mix_c, j0=6, final=False, tm=tm_c)
    return x
```

```python
import functools
import math

import numpy as np
import jax
import jax.numpy as jnp
from jax import lax
from jax.experimental import pallas as pl
from jax.experimental.pallas import tpu as pltpu

F32 = jnp.float32
BF16 = jnp.bfloat16

D_MODEL = 1024
GRID_W = 64
N_MOD = 9
W_A, W_B, W_C = 512, 256, 256
D_IN = 2 * W_A + W_B + 3 * W_C
RG_HEADS, RG_HD, RG_CONV, RG_C = 8, 64, 4, 8.0
POOL_WINDOWS = (2, 4, 8, 16)
POOL_GW = W_B // len(POOL_WINDOWS)
HY_ORDER, HY_SHORT, HY_BANDS, HY_FH = 2, 3, 16, 64
HY_EMB = 2 * HY_BANDS + 1
HY_EMB_PAD = 40
HY_TARGET, HY_FAST, HY_SLOW = 1e-2, 0.3, 1.5
D_FF = 2816
EPS = 1e-6

V7X_LANES = 128
V7X_SUBLANES = 8
V7X_MXU_DIM = 256
V7X_VMEM_LIMIT_BYTES = 56 * 1024 * 1024

FFT_MINOR = V7X_LANES
HALO = V7X_SUBLANES
FF_CHUNKS = ((0, 768), (768, 768), (1536, 768), (2304, 512))
POOL_TILE = 256


def _params(*sem):
    return pltpu.CompilerParams(dimension_semantics=sem, vmem_limit_bytes=V7X_VMEM_LIMIT_BYTES)


def _resident(shape, index_map):
    return pl.BlockSpec(shape, index_map, pipeline_mode=pl.Buffered(1))


def _bf16_table(values):
    return jnp.asarray(values, F32).astype(BF16)


def _split_bf16(x):
    hi = x.astype(BF16)
    return hi, (x - hi.astype(F32)).astype(BF16)


def _dot_bf16x3(a, b):
    a_hi, a_lo = _split_bf16(a)
    b_hi, b_lo = _split_bf16(b)
    d = lambda p, q: jnp.dot(p, q, preferred_element_type=F32)
    return d(a_hi, b_hi) + (d(a_lo, b_hi) + d(a_hi, b_lo))


def _adanorm(x, g, shift, scale):
    ms = jnp.mean(x * x, axis=-1, keepdims=True)
    return (x * lax.rsqrt(ms + EPS)) * g * (1.0 + scale) + shift


def _mod_kernel(c_ref, w_ref, b_ref, o_ref):
    s = c_ref[...]
    s = s * jax.nn.sigmoid(s)
    o_ref[0] = jnp.dot(s.astype(BF16), w_ref[0].astype(BF16), preferred_element_type=F32) + b_ref[0]


def _modulation(cvec, w_mod, b_mod):
    depth, _, n = w_mod.shape
    tn = 1024
    return pl.pallas_call(
        _mod_kernel,
        grid=(depth, n // tn),
        in_specs=[pl.BlockSpec((V7X_SUBLANES, D_MODEL), lambda l, j: (0, 0)),
                  pl.BlockSpec((1, D_MODEL, tn), lambda l, j: (l, 0, j)),
                  pl.BlockSpec((1, 1, tn), lambda l, j: (l, 0, j))],
        out_specs=pl.BlockSpec((1, V7X_SUBLANES, tn), lambda l, j: (l, 0, j)),
        out_shape=jax.ShapeDtypeStruct((depth, V7X_SUBLANES, n), F32),
        compiler_params=_params("arbitrary", "arbitrary"),
        name="modulation",
    )(cvec, w_mod, b_mod.reshape(depth, 1, n))


def _pool_tile(x, band_ref, inv_ref, w_ref, b_ref, s_ref):
    xb = x.astype(BF16)
    col = lax.broadcasted_iota(jnp.int32, x.shape, 1) // POOL_GW
    tot = jnp.zeros_like(x)
    for g in range(len(POOL_WINDOWS)):
        tot = jnp.where(col == g, jnp.dot(band_ref[g], xb, preferred_element_type=F32), tot)
    pooled = tot * inv_ref[...] - x
    y = jnp.dot(pooled.astype(BF16), w_ref[...], preferred_element_type=F32)
    return (y + b_ref[...]) * s_ref[...]


def _ffn_kernel(x_ref, m_ref, g_ref, win_ref, wout_ref, fg_ref, *rest, j0, final, mix, tm):
    x = x_ref[0]
    if mix:
        (u_ref, wg_ref, bg_ref, lam_ref, h0_ref, hf_ref, ga_ref, up_ref, yc_ref, band_ref, inv_ref, pw_ref, pb_ref,
         ps_ref, wo_ref, o_ref, hend_ref, a_s, b_s, c_s) = rest

        @pl.when(pl.program_id(1) == 0)
        def _():
            c_s[...] = h0_ref[0]

        _lru_coeffs(u_ref[0], wg_ref, bg_ref, lam_ref, a_s, b_s)
        c_s[...] = _lru_scan_tile(a_s, b_s, c_s[...], tm=tm, reverse=True, inline=True)
        hend_ref[0] = c_s[...]
        ya = (hf_ref[0] + b_s[...]) * jax.nn.gelu(ga_ref[0])
        yb = jnp.concatenate([_pool_tile(up_ref[0, r:r + POOL_TILE, :], band_ref, inv_ref, pw_ref, pb_ref, ps_ref)
                              for r in range(0, tm, POOL_TILE)], axis=0)
        y = jnp.dot(ya.astype(BF16), wo_ref[0:W_A, :], preferred_element_type=F32)
        y = y + jnp.dot(yb.astype(BF16), wo_ref[W_A:W_A + W_B, :], preferred_element_type=F32)
        y = y + jnp.dot(yc_ref[0].astype(BF16), wo_ref[W_A + W_B:, :], preferred_element_type=F32)
        x = x + m_ref[0, 5:6, :] * y
    else:
        (o_ref,) = rest
    shift = m_ref[0, j0:j0 + 1, :]
    scale = m_ref[0, j0 + 1:j0 + 2, :]
    gate = m_ref[0, j0 + 2:j0 + 3, :]
    h = _adanorm(x, g_ref[...], shift, scale).astype(BF16)
    acc = None
    for s, w in FF_CHUNKS:
        gt = jnp.dot(h, win_ref[:, s:s + w], preferred_element_type=F32)
        up = jnp.dot(h, win_ref[:, D_FF + s:D_FF + s + w], preferred_element_type=F32)
        a = (gt * jax.nn.sigmoid(gt) * up).astype(BF16)
        p = jnp.dot(a, wout_ref[s:s + w, :], preferred_element_type=F32)
        acc = p if acc is None else acc + p
    y = x + (0.5 * gate) * acc
    if final:
        ms = jnp.mean(y * y, axis=-1, keepdims=True)
        y = (y * lax.rsqrt(ms + EPS)) * fg_ref[...]
    o_ref[0] = y


def _ffn(x, mod, g, w_in, w_out, wsel, final_g, mix=None, *, j0, final, tm):
    bn, n, _ = x.shape
    last = n // tm - 1
    tile_map = (lambda b, i: (b, last - i, 0)) if mix is not None else (lambda b, i: (b, i, 0))
    tile = lambda w: pl.BlockSpec((1, tm, w), tile_map)
    const = lambda shape: pl.BlockSpec(shape, lambda b, i: (0,) * len(shape))
    in_specs = [tile(D_MODEL),
                pl.BlockSpec((1, N_MOD, D_MODEL), lambda b, i: (b, 0, 0)),
                const((1, D_MODEL)),
                _resident((None, None, D_MODEL, 2 * D_FF), lambda b, i: wsel + (0, 0)),
                _resident((None, None, D_FF, D_MODEL), lambda b, i: wsel + (0, 0)),
                const((1, D_MODEL))]
    args = [x, mod, g.reshape(1, D_MODEL), w_in, w_out, final_g.reshape(1, D_MODEL)]
    x_shape = jax.ShapeDtypeStruct(x.shape, F32)
    if mix is None:
        return pl.pallas_call(
            functools.partial(_ffn_kernel, j0=j0, final=final, mix=False, tm=tm),
            grid=(bn, n // tm), in_specs=in_specs, out_specs=tile(D_MODEL), out_shape=x_shape,
            compiler_params=_params("arbitrary", "arbitrary"), name="ffn",
        )(*args)
    u, wg, bg, lam, h0, hf, ga, up, yc, band, inv, pw, pb, ps, wo = mix
    in_specs += ([tile(W_A)] + _lru_param_specs()
                 + [tile(W_A), tile(W_A), tile(W_B), tile(W_C), const(band.shape), const(inv.shape),
                    const((W_B, W_B)), const((1, W_B)), const((1, W_B)),
                    _resident((D_MODEL, D_MODEL), lambda b, i: (0, 0))])
    args += [u, wg, bg, lam, h0.reshape(bn, 1, W_A), hf, ga, up, yc, band, inv, pw, pb.reshape(1, W_B),
             ps.reshape(1, W_B), wo]
    out, hend = pl.pallas_call(
        functools.partial(_ffn_kernel, j0=j0, final=final, mix=True, tm=tm),
        grid=(bn, n // tm),
        in_specs=in_specs,
        out_specs=[tile(D_MODEL), pl.BlockSpec((1, 1, W_A), lambda b, i: (b, 0, 0))],
        out_shape=[x_shape, jax.ShapeDtypeStruct((bn, 1, W_A), F32)],
        scratch_shapes=_lru_scratch(tm),
        compiler_params=_params("arbitrary", "arbitrary"),
        name="mix_ffn",
    )(*args)
    return out, hend.reshape(bn, W_A)


def _proj_kernel(xp_ref, x_ref, xn_ref, m_ref, g_ref, w_ref, cwa_ref, cba_ref, cwh_ref, cbh_ref,
                 ua_ref, ga_ref, up_ref, z_ref, pa_s, ph_s, *, tm, n_tiles):
    i = pl.program_id(1)
    norm = lambda v: _adanorm(v, g_ref[...], m_ref[0, 3:4, :], m_ref[0, 4:5, :])
    h_prev = jnp.where(i > 0, norm(xp_ref[0]), 0.0)
    h_next = jnp.where(i < n_tiles - 1, norm(xn_ref[0]), 0.0)
    h = jnp.concatenate([h_prev, norm(x_ref[0]), h_next], axis=0).astype(BF16)
    proj = lambda lo, hi: jnp.dot(h, w_ref[:, lo:hi], preferred_element_type=F32)

    def short_conv(lo, hi, w_conv_ref, b_conv_ref, taps, pad_left, p_s, out_ref):
        p_s[...] = proj(lo, hi)
        acc = b_conv_ref[...] + w_conv_ref[0:1, :] * p_s[pl.ds(HALO - pad_left, tm), :]
        for k in range(1, taps):
            acc = acc + w_conv_ref[k:k + 1, :] * p_s[pl.ds(HALO - pad_left + k, tm), :]
        out_ref[0] = acc

    short_conv(2 * W_A + W_B, D_IN, cwh_ref, cbh_ref, HY_SHORT, HY_SHORT // 2, ph_s, z_ref)
    short_conv(0, W_A, cwa_ref, cba_ref, RG_CONV, RG_CONV // 2, pa_s, ua_ref)
    rest = proj(W_A, 2 * W_A + W_B)
    ga_ref[0] = rest[HALO:HALO + tm, :W_A]
    up_ref[0] = rest[HALO:HALO + tm, W_A:]


def _proj(x, mod, g, w_in, cwa, cba, cwh, cbh, *, tm):
    bn, n, _ = x.shape
    n_tiles = n // tm
    hb = tm // HALO
    last_hb = n // HALO - 1
    outs = (W_A, W_A, W_B, 3 * W_C)
    return pl.pallas_call(
        functools.partial(_proj_kernel, tm=tm, n_tiles=n_tiles),
        grid=(bn, n_tiles),
        in_specs=[pl.BlockSpec((1, HALO, D_MODEL), lambda b, i: (b, jnp.maximum(i * hb - 1, 0), 0)),
                  pl.BlockSpec((1, tm, D_MODEL), lambda b, i: (b, i, 0)),
                  pl.BlockSpec((1, HALO, D_MODEL), lambda b, i: (b, jnp.minimum((i + 1) * hb, last_hb), 0)),
                  pl.BlockSpec((1, N_MOD, D_MODEL), lambda b, i: (b, 0, 0)),
                  pl.BlockSpec((1, D_MODEL), lambda b, i: (0, 0)),
                  _resident((D_MODEL, D_IN), lambda b, i: (0, 0)),
                  pl.BlockSpec((RG_CONV, W_A), lambda b, i: (0, 0)),
                  pl.BlockSpec((1, W_A), lambda b, i: (0, 0)),
                  pl.BlockSpec((HY_SHORT, 3 * W_C), lambda b, i: (0, 0)),
                  pl.BlockSpec((1, 3 * W_C), lambda b, i: (0, 0))],
        out_specs=[pl.BlockSpec((1, tm, w), lambda b, i: (b, i, 0)) for w in outs],
        out_shape=[jax.ShapeDtypeStruct((bn, n, w), F32) for w in outs],
        scratch_shapes=[pltpu.VMEM((tm + 2 * HALO, W_A), F32), pltpu.VMEM((tm + 2 * HALO, 3 * W_C), F32)],
        compiler_params=_params("arbitrary", "arbitrary"),
        name="proj",
    )(x, x, x, mod, g.reshape(1, D_MODEL), w_in, cwa, cba.reshape(1, W_A), cwh, cbh.reshape(1, 3 * W_C))


def _lru_coeffs(u, wg_ref, bg_ref, lam_ref, a_s, b_s):
    pre = jnp.dot(u.astype(BF16), wg_ref[...], preferred_element_type=F32) + bg_ref[...]
    q = (-0.25 * RG_C) * jax.nn.softplus(-lam_ref[...])
    t = jnp.tanh(q * jnp.tanh(0.5 * pre[:, :W_A]) + q)
    inv = 1.0 / (1.0 - t)
    a_s[...] = (1.0 + t) * inv
    b_s[...] = (jnp.sqrt(-t) * inv) * ((jnp.tanh(0.5 * pre[:, W_A:]) + 1.0) * u)


def _lru_scan_tile(a_s, b_s, carry, *, tm, reverse, inline=False):
    groups = tm // V7X_SUBLANES
    rowi = lax.broadcasted_iota(jnp.int32, (V7X_SUBLANES, W_A), 0)

    def body(k, c):
        gi = (groups - 1 - k) if reverse else k
        r0 = gi * V7X_SUBLANES if inline else pl.multiple_of(gi * V7X_SUBLANES, V7X_SUBLANES)
        a = a_s[pl.ds(r0, V7X_SUBLANES), :]
        b = b_s[pl.ds(r0, V7X_SUBLANES), :]
        first = rowi == (V7X_SUBLANES - 1 if reverse else 0)
        b = jnp.where(first, a * c + b, b)
        for s in (1, 2, 4):
            if reverse:
                shift, m = V7X_SUBLANES - s, rowi < V7X_SUBLANES - s
            else:
                shift, m = s, rowi >= s
            b = jnp.where(m, a * pltpu.roll(b, shift, 0) + b, b)
            if s < V7X_SUBLANES // 2:
                a = jnp.where(m, a * pltpu.roll(a, shift, 0), a)
        b_s[pl.ds(r0, V7X_SUBLANES), :] = b
        return b[0:1, :] if reverse else b[V7X_SUBLANES - 1:V7X_SUBLANES, :]

    if inline:
        for k in range(groups):
            carry = body(k, carry)
        return carry
    return lax.fori_loop(0, groups, body, carry, unroll=4)


def _lru_scan_kernel(u_ref, wg_ref, bg_ref, lam_ref, h0_ref, h_ref, a_s, b_s, c_s, *, tm, reverse):
    @pl.when(pl.program_id(1) == 0)
    def _():
        c_s[...] = h0_ref[0]

    _lru_coeffs(u_ref[0], wg_ref, bg_ref, lam_ref, a_s, b_s)
    c_s[...] = _lru_scan_tile(a_s, b_s, c_s[...], tm=tm, reverse=reverse)
    h_ref[0] = b_s[...]


def _lru_param_specs():
    const = lambda shape: pl.BlockSpec(shape, lambda b, i: (0,) * len(shape))
    return [const((W_A, 2 * W_A)), const((1, 2 * W_A)), const((1, W_A)),
            pl.BlockSpec((1, 1, W_A), lambda b, i: (b, 0, 0))]


def _lru_scratch(tm):
    return [pltpu.VMEM((tm, W_A), F32), pltpu.VMEM((tm, W_A), F32), pltpu.VMEM((1, W_A), F32)]


def _lru_scan(u, wg, bg, lam, h0, *, tm, reverse):
    bn, n, _ = u.shape
    last = n // tm - 1
    tile_map = (lambda b, i: (b, last - i, 0)) if reverse else (lambda b, i: (b, i, 0))
    return pl.pallas_call(
        functools.partial(_lru_scan_kernel, tm=tm, reverse=reverse),
        grid=(bn, n // tm),
        in_specs=[pl.BlockSpec((1, tm, W_A), tile_map)] + _lru_param_specs(),
        out_specs=pl.BlockSpec((1, tm, W_A), tile_map),
        out_shape=jax.ShapeDtypeStruct(u.shape, F32),
        scratch_shapes=_lru_scratch(tm),
        compiler_params=_params("arbitrary", "arbitrary"),
        name="lru_scan",
    )(u, wg, bg, lam, h0.reshape(bn, 1, W_A))


def _pool_tables(row_len):
    t = np.arange(POOL_TILE)
    p = t % row_len
    base = t - p
    band = np.zeros((len(POOL_WINDOWS), POOL_TILE, POOL_TILE), np.float32)
    inv = np.zeros((POOL_TILE, W_B), np.float32)
    for g, win in enumerate(POOL_WINDOWS):
        lo = np.clip(p - win // 2, 0, row_len)
        hi = np.clip(p + win - win // 2, 0, row_len)
        s = np.arange(POOL_TILE)[None, :]
        band[g] = ((s >= (base + lo)[:, None]) & (s < (base + hi)[:, None])).astype(np.float32)
        inv[:, g * POOL_GW:(g + 1) * POOL_GW] = (1.0 / (hi - lo).astype(np.float64))[:, None]
    return _bf16_table(band), jnp.asarray(inv, F32)


def _filter_kernel(f_ref, w1_ref, b1_ref, fr_ref, w2_ref, b2_ref, w3_ref, dl_ref, o_ref, *, tm):
    feats = f_ref[...]
    fr = fr_ref[...]
    hid = jnp.sin(fr * (_dot_bf16x3(feats, w1_ref[...]) + b1_ref[...]))
    hid = jnp.sin(fr * (_dot_bf16x3(hid, w2_ref[...]) + b2_ref[...]))
    filt = _dot_bf16x3(hid, w3_ref[...])
    filt = filt * jnp.exp(-feats[:, 0:1] * dl_ref[...])
    row = pl.program_id(0) * tm + lax.broadcasted_iota(jnp.int32, filt.shape, 0)
    bwd_col = (lax.broadcasted_iota(jnp.int32, filt.shape, 1) // W_C) % 2 == 1
    o_ref[...] = jnp.where((row == 0) & bwd_col, 0.0, filt)


def _filter_features(n):
    t = jnp.linspace(0.0, 1.0, n, dtype=F32)[:, None]
    wpos = 2.0 * math.pi * jnp.arange(n, dtype=F32)[:, None] / n
    bands = jnp.linspace(1e-4, HY_BANDS - 1, HY_BANDS, dtype=F32)[None, :]
    feats = jnp.concatenate([t, jnp.cos(bands * wpos), -jnp.sin(bands * wpos)], axis=-1)
    return jnp.pad(feats, ((0, 0), (0, HY_EMB_PAD - HY_EMB)))


def _filter_decay_rates():
    deltas = jnp.abs(jnp.linspace(math.log(HY_TARGET) / HY_FAST, math.log(HY_TARGET) / HY_SLOW, W_C, dtype=F32))
    return jnp.tile(deltas, 2 * HY_ORDER).reshape(1, 2 * HY_ORDER * W_C)


def _filters(n, w1, b1, freq, w2, b2, w3):
    tm = min(n, 512)
    nf = 2 * HY_ORDER * W_C
    const = lambda shape: pl.BlockSpec(shape, lambda i: (0, 0))
    return pl.pallas_call(
        functools.partial(_filter_kernel, tm=tm),
        grid=(n // tm,),
        in_specs=[pl.BlockSpec((tm, HY_EMB_PAD), lambda i: (i, 0)),
                  const((HY_EMB_PAD, HY_FH)), const((1, HY_FH)), const((1, HY_FH)),
                  const((HY_FH, HY_FH)), const((1, HY_FH)), const((HY_FH, nf)), const((1, nf))],
        out_specs=pl.BlockSpec((tm, nf), lambda i: (i, 0)),
        out_shape=jax.ShapeDtypeStruct((n, nf), F32),
        compiler_params=_params("arbitrary"),
        name="hyena_filter",
    )(_filter_features(n), jnp.pad(w1, ((0, HY_EMB_PAD - HY_EMB), (0, 0))), b1.reshape(1, HY_FH),
      freq.reshape(1, HY_FH), w2, b2.reshape(1, HY_FH), w3, _filter_decay_rates())


T2_BLOCKS = FFT_MINOR // V7X_SUBLANES


class _FftPlan:
    def __init__(self, n):
        assert n % (2 * FFT_MINOR) == 0
        self.n = n
        self.n_fft = 2 * n
        self.n1 = self.n_fft // FFT_MINOR
        self.t1 = self.n1 // 2
        self.k1 = self.n1 // 2 + 1
        self.k_block = max(d for d in range(1, 14) if self.k1 % d == 0)
        t1 = np.arange(self.t1)
        k1 = np.arange(self.k1)
        eye = np.eye(V7X_SUBLANES)
        ang = 2.0 * np.pi * ((k1[:, None] * t1[None, :]) % self.n1) / self.n1
        fa = np.concatenate([np.cos(ang), -np.sin(ang)], axis=0)
        self.stage_a = _bf16_table(np.kron(fa, eye))
        wgt = np.where((k1 == 0) | (k1 == self.n1 // 2), 1.0, 2.0) / self.n_fft
        fc = np.concatenate([np.cos(ang) * wgt[:, None], -np.sin(ang) * wgt[:, None]], axis=0).T
        self.stage_a_inv = _bf16_table(np.kron(fc, eye))
        t2 = np.arange(FFT_MINOR)
        k2 = np.arange(FFT_MINOR)
        idx = (t2[None, None, :] * (k1[:, None, None] + self.n1 * k2[None, :, None])) % self.n_fft
        ang_b = 2.0 * np.pi * idx / self.n_fft
        gr, gi = np.cos(ang_b), -np.sin(ang_b)
        fwd = np.concatenate([np.concatenate([gr, -gi], axis=2), np.concatenate([gi, gr], axis=2)], axis=1)
        self.stage_b = _bf16_table(fwd)
        self.stage_b_inv = _bf16_table(np.transpose(fwd, (0, 2, 1)))

    def scratch(self, signals):
        return pltpu.VMEM((signals, 2, self.k1, T2_BLOCKS, V7X_SUBLANES, V7X_LANES), F32)


def _time_tiles(ref, tb, rows):
    return ref[:, pl.ds(tb, 1)].reshape(rows, V7X_LANES)


def _stage_a_forward(plan, x_refs, fa_ref, a_s):
    def body(tb, carry):
        xs = jnp.concatenate([_time_tiles(xr, tb, plan.t1 * V7X_SUBLANES) for xr in x_refs], axis=1)
        av = jnp.dot(fa_ref[...], xs.astype(BF16), preferred_element_type=F32)
        for j in range(len(x_refs)):
            part = av[:, j * V7X_LANES:(j + 1) * V7X_LANES]
            a_s[j, :, :, pl.ds(tb, 1)] = part.reshape(2, plan.k1, 1, V7X_SUBLANES, V7X_LANES)
        return carry
    lax.fori_loop(0, T2_BLOCKS, body, 0)


def _stage_b_slab(a_s, k, count):
    return jnp.concatenate(
        [a_s[j, :, pl.ds(k, 1)].reshape(2 * FFT_MINOR, V7X_LANES) for j in range(count)], axis=1)


def _spectrum_kernel(x0_ref, x1_ref, fa_ref, fb_ref, o_ref, a_s, *, plan):
    kb = pl.program_id(1)

    @pl.when(kb == 0)
    def _():
        _stage_a_forward(plan, (x0_ref, x1_ref), fa_ref, a_s)

    half = FFT_MINOR
    rows = 2 * half
    for i in range(plan.k_block):
        slab = _stage_b_slab(a_s, kb * plan.k_block + i, 2).astype(BF16)
        xs = jnp.dot(fb_ref[i], slab, preferred_element_type=F32)
        o_ref[i * rows:i * rows + half, :] = xs[:half, :V7X_LANES] + xs[:half, V7X_LANES:]
        o_ref[i * rows + half:(i + 1) * rows, :] = xs[half:, :V7X_LANES] - xs[half:, V7X_LANES:]


def _filter_spectrum(plan, filt):
    n, nf = filt.shape
    rows = 2 * FFT_MINOR
    kbs = plan.k_block
    n_cb = W_C // V7X_LANES
    filt = filt.reshape(plan.t1, T2_BLOCKS, V7X_SUBLANES, nf)
    sig = (plan.t1, T2_BLOCKS, V7X_SUBLANES, V7X_LANES)
    return pl.pallas_call(
        functools.partial(_spectrum_kernel, plan=plan),
        grid=(HY_ORDER * n_cb, plan.k1 // kbs),
        in_specs=[_resident(sig, lambda c, k: (0, 0, 0, 2 * (c // n_cb) * n_cb + c % n_cb)),
                  _resident(sig, lambda c, k: (0, 0, 0, (2 * (c // n_cb) + 1) * n_cb + c % n_cb)),
                  pl.BlockSpec(plan.stage_a.shape, lambda c, k: (0, 0)),
                  pl.BlockSpec((kbs, rows, rows), lambda c, k: (k, 0, 0))],
        out_specs=pl.BlockSpec((kbs * rows, V7X_LANES), lambda c, k: (k, c)),
        out_shape=jax.ShapeDtypeStruct((plan.k1 * rows, HY_ORDER * W_C), F32),
        scratch_shapes=[plan.scratch(2)],
        compiler_params=_params("arbitrary", "arbitrary"),
        name="filter_spectrum",
    )(filt, filt, plan.stage_a, plan.stage_b)


def _conv_kernel(v_ref, g_ref, h_ref, fa_ref, fb_ref, fbi_ref, fai_ref, sk_ref, o_ref, a_s, *, plan, bn):
    kb = pl.program_id(1)
    half = FFT_MINOR
    rows = 2 * half

    @pl.when(kb == 0)
    def _():
        _stage_a_forward(plan, [v_ref.at[b] for b in range(bn)], fa_ref, a_s)

    for i in range(plan.k_block):
        k = kb * plan.k_block + i
        xs = jnp.dot(fb_ref[i], _stage_b_slab(a_s, k, bn).astype(BF16), preferred_element_type=F32)
        xr, xi = xs[:half], xs[half:]
        hr = jnp.concatenate([h_ref[i * rows:i * rows + half, :]] * bn, axis=1)
        hi = jnp.concatenate([h_ref[i * rows + half:(i + 1) * rows, :]] * bn, axis=1)
        ys = jnp.concatenate([xr * hr - xi * hi, xr * hi + xi * hr], axis=0)
        ds = jnp.dot(fbi_ref[i], ys.astype(BF16), preferred_element_type=F32)
        for b in range(bn):
            part = ds[:, b * V7X_LANES:(b + 1) * V7X_LANES]
            a_s[b, :, pl.ds(k, 1)] = part.reshape(2, 1, T2_BLOCKS, V7X_SUBLANES, V7X_LANES)

    @pl.when(kb == pl.num_programs(1) - 1)
    def _():
        t_rows = plan.t1 * V7X_SUBLANES

        def body(tb, carry):
            ds = jnp.concatenate(
                [a_s[b, :, :, pl.ds(tb, 1)].reshape(2 * plan.k1 * V7X_SUBLANES, V7X_LANES) for b in range(bn)],
                axis=1)
            y = jnp.dot(fai_ref[...], ds.astype(BF16), preferred_element_type=F32)
            for b in range(bn):
                vv = _time_tiles(v_ref.at[b], tb, t_rows)
                gg = _time_tiles(g_ref.at[b], tb, t_rows)
                res = gg * (y[:, b * V7X_LANES:(b + 1) * V7X_LANES] + sk_ref[...] * vv)
                o_ref[b, :, pl.ds(tb, 1)] = res.reshape(plan.t1, 1, V7X_SUBLANES, V7X_LANES)
            return carry
        lax.fori_loop(0, T2_BLOCKS, body, 0)


def _hyena_conv(plan, v_arr, v_col, g_arr, g_col, spec, order, skip):
    bn = v_arr.shape[0]
    cb = V7X_LANES
    rows = 2 * FFT_MINOR
    n_cb = W_C // cb
    kbs = plan.k_block
    sig = (bn, plan.t1, T2_BLOCKS, V7X_SUBLANES, cb)
    return pl.pallas_call(
        functools.partial(_conv_kernel, plan=plan, bn=bn),
        grid=(n_cb, plan.k1 // kbs),
        in_specs=[_resident(sig, lambda c, k: (0, 0, 0, 0, v_col + c)),
                  _resident(sig, lambda c, k: (0, 0, 0, 0, g_col + c)),
                  pl.BlockSpec((kbs * rows, cb), lambda c, k: (k, order * n_cb + c)),
                  pl.BlockSpec(plan.stage_a.shape, lambda c, k: (0, 0)),
                  pl.BlockSpec((kbs, rows, rows), lambda c, k: (k, 0, 0)),
                  pl.BlockSpec((kbs, rows, rows), lambda c, k: (k, 0, 0)),
                  pl.BlockSpec(plan.stage_a_inv.shape, lambda c, k: (0, 0)),
                  pl.BlockSpec((1, cb), lambda c, k: (0, c))],
        out_specs=_resident(sig, lambda c, k: (0, 0, 0, 0, c)),
        out_shape=jax.ShapeDtypeStruct((bn, plan.t1, T2_BLOCKS, V7X_SUBLANES, W_C), F32),
        scratch_shapes=[plan.scratch(bn)],
        compiler_params=_params("arbitrary", "arbitrary"),
        name="hyena_conv",
    )(v_arr, g_arr, spec, plan.stage_a, plan.stage_b, plan.stage_b_inv, plan.stage_a_inv,
      skip.reshape(1, W_C))


def _hyena(z, n_fft_len, filt_args, skip):
    bn, n, _ = z.shape
    plan = _FftPlan(n_fft_len)
    filt = _filters(n, *filt_args)
    if n_fft_len != n:
        filt = jnp.pad(filt, ((0, n_fft_len - n), (0, 0)))
        z = jnp.pad(z, ((0, 0), (0, n_fft_len - n), (0, 0)))
    spec = _filter_spectrum(plan, filt)
    n_cb = W_C // V7X_LANES
    z = z.reshape(bn, plan.t1, T2_BLOCKS, V7X_SUBLANES, 3 * W_C)
    y1 = _hyena_conv(plan, z, 0, z, n_cb, spec, 0, skip[0])
    y2 = _hyena_conv(plan, y1, 0, z, 2 * n_cb, spec, 1, skip[1])
    return y2.reshape(bn, n_fft_len, W_C)[:, :n]


def _dense_block_diag(w):
    h, c, d = w.shape[-3:]
    eye = jnp.eye(h, dtype=w.dtype)
    dense = jnp.einsum('...hcd,hk->...hckd', w, eye)
    return dense.reshape(w.shape[:-3] + (h * c, h * d))


def _tile_rows(n):
    return min(n, 512)


def kernel(x, c, ctx, c_ctx, w_mod, b_mod, norm_g, ffn_w_in, ffn_w_out, w_in, w_out, lru_conv_w, lru_conv_b,
           lru_gate_w, lru_gate_b, lru_lambda, pool_w, pool_b, pool_scale, hy_conv_w, hy_conv_b, hy_w1, hy_b1,
           hy_freq, hy_w2, hy_b2, hy_w3, hy_skip, final_g):
    bn, n, _ = x.shape
    n_ctx = ctx.shape[1]
    depth = w_mod.shape[0]
    assert bn + 1 <= V7X_SUBLANES and n % GRID_W == 0 and n % POOL_TILE == 0 and n_ctx == POOL_TILE
    tm_x, tm_c = _tile_rows(n), _tile_rows(n_ctx)

    cvec = jnp.zeros((V7X_SUBLANES, D_MODEL), F32).at[:bn].set(c).at[bn].set(c_ctx)
    mods = _modulation(cvec, w_mod, b_mod)
    band_x, inv_x = _pool_tables(GRID_W)
    band_c, inv_c = _pool_tables(n_ctx)
    zeros = jnp.zeros((bn, W_A), F32)
    f_in = ffn_w_in.astype(BF16)
    f_out = ffn_w_out.astype(BF16)

    xc = ctx
    for l in range(depth):
        last = l == depth - 1
        mx = mods[l, :bn].reshape(bn, N_MOD, D_MODEL)
        mc = jnp.broadcast_to(mods[l, bn].reshape(1, N_MOD, D_MODEL), (bn, N_MOD, D_MODEL))
        w_in_l = w_in[l].astype(BF16)
        w_out_l = w_out[l].astype(BF16)
        gate_dense = _dense_block_diag(lru_gate_w[l])
        wg = [jnp.concatenate([gate_dense[d, 0], gate_dense[d, 1]], axis=-1).astype(BF16) for d in range(2)]
        bg = [lru_gate_b[l, d].reshape(1, 2 * W_A) for d in range(2)]
        lam = [lru_lambda[l, d].reshape(1, W_A) for d in range(2)]
        pw = _dense_block_diag(pool_w[l]).astype(BF16)
        filt_args = (hy_w1[l], hy_b1[l], hy_freq[l], hy_w2[l], hy_b2[l], hy_w3[l])
        proj_args = (w_in_l, lru_conv_w[l], lru_conv_b[l], hy_conv_w[l], hy_conv_b[l])

        def mixer(xs, mod, tm, h0f, band, inv, fft_len):
            ua, ga, up, z = _proj(xs, mod, norm_g[l, 1], *proj_args, tm=tm)
            hf = _lru_scan(ua, wg[0], bg[0], lam[0], h0f, tm=tm, reverse=False)
            yc = _hyena(z, fft_len, filt_args, hy_skip[l])
            return lambda h0b: (ua, wg[1], bg[1], lam[1], h0b, hf, ga, up, yc, band, inv, pw, pool_b[l],
                                pool_scale[l], w_out_l), hf[:, -1]

        x = _ffn(x, mx, norm_g[l, 0], f_in, f_out, (l, 0), final_g, j0=0, final=False, tm=tm_x)
        xc = _ffn(xc, mc, norm_g[l, 0], f_in, f_out, (l, 0), final_g, j0=0, final=False, tm=tm_c)
        if last:
            ua_c = _proj(xc, mc, norm_g[l, 1], *proj_args, tm=tm_c)[0]
            h0f = _lru_scan(ua_c, wg[0], bg[0], lam[0], zeros, tm=tm_c, reverse=False)[:, -1]
            h0b = _lru_scan(ua_c, wg[1], bg[1], lam[1], zeros, tm=tm_c, reverse=True)[:, 0]
        else:
            mix_c, h0f = mixer(xc, mc, tm_c, zeros, band_c, inv_c, n_ctx)
            xc, h0b = _ffn(xc, mc, norm_g[l, 2], f_in, f_out, (l, 1), final_g, mix_c(zeros), j0=6, final=False,
                           tm=tm_c)
        mix_x, _ = mixer(x, mx, tm_x, h0f, band_x, inv_x, n)
        x, _ = _ffn(x, mx, norm_g[l, 2], f_in, f_out, (l, 1), final_g, mix_x(h0b), j0=6, final=last, tm=tm_x)
    return x
```

```python
import functools
import math

import numpy as np
import jax
import jax.numpy as jnp
from jax import lax
from jax.experimental import pallas as pl
from jax.experimental.pallas import tpu as pltpu

F32 = jnp.float32
BF16 = jnp.bfloat16

D_MODEL = 1024
GRID_W = 64
N_MOD = 9
W_A, W_B, W_C = 512, 256, 256
D_IN = 2 * W_A + W_B + 3 * W_C
RG_HEADS, RG_HD, RG_CONV, RG_C = 8, 64, 4, 8.0
POOL_WINDOWS = (2, 4, 8, 16)
POOL_GW = W_B // len(POOL_WINDOWS)
HY_ORDER, HY_SHORT, HY_BANDS, HY_FH = 2, 3, 16, 64
HY_EMB = 2 * HY_BANDS + 1
HY_EMB_PAD = 40
HY_TARGET, HY_FAST, HY_SLOW = 1e-2, 0.3, 1.5
D_FF = 2816
EPS = 1e-6

V7X_LANES = 128
V7X_SUBLANES = 8
V7X_MXU_DIM = 256
V7X_VMEM_LIMIT_BYTES = 56 * 1024 * 1024

FFT_MINOR = V7X_LANES
HALO = V7X_SUBLANES
FF_CHUNKS = ((0, 768), (768, 768), (1536, 768), (2304, 512))
POOL_TILE = 256


def _params(*sem):
    return pltpu.CompilerParams(dimension_semantics=sem, vmem_limit_bytes=V7X_VMEM_LIMIT_BYTES)


def _resident(shape, index_map):
    return pl.BlockSpec(shape, index_map, pipeline_mode=pl.Buffered(1))


def _bf16_table(values):
    return jnp.asarray(values, F32).astype(BF16)


def _split_bf16(x):
    hi = x.astype(BF16)
    return hi, (x - hi.astype(F32)).astype(BF16)


def _dot_bf16x3(a, b):
    a_hi, a_lo = _split_bf16(a)
    b_hi, b_lo = _split_bf16(b)
    d = lambda p, q: jnp.dot(p, q, preferred_element_type=F32)
    return d(a_hi, b_hi) + (d(a_lo, b_hi) + d(a_hi, b_lo))


def _adanorm(x, g, shift, scale):
    ms = jnp.mean(x * x, axis=-1, keepdims=True)
    return (x * lax.rsqrt(ms + EPS)) * g * (1.0 + scale) + shift


def _mod_kernel(c_ref, w_ref, b_ref, o_ref):
    s = c_ref[...]
    s = s * jax.nn.sigmoid(s)
    o_ref[0] = jnp.dot(s.astype(BF16), w_ref[0].astype(BF16), preferred_element_type=F32) + b_ref[0]


def _modulation(cvec, w_mod, b_mod):
    depth, _, n = w_mod.shape
    tn = 1024
    return pl.pallas_call(
        _mod_kernel,
        grid=(depth, n // tn),
        in_specs=[pl.BlockSpec((V7X_SUBLANES, D_MODEL), lambda l, j: (0, 0)),
                  pl.BlockSpec((1, D_MODEL, tn), lambda l, j: (l, 0, j)),
                  pl.BlockSpec((1, 1, tn), lambda l, j: (l, 0, j))],
        out_specs=pl.BlockSpec((1, V7X_SUBLANES, tn), lambda l, j: (l, 0, j)),
        out_shape=jax.ShapeDtypeStruct((depth, V7X_SUBLANES, n), F32),
        compiler_params=_params("arbitrary", "arbitrary"),
        name="modulation",
    )(cvec, w_mod, b_mod.reshape(depth, 1, n))


def _pool_tile(x, band_ref, inv_ref, w_ref, b_ref, s_ref):
    xb = x.astype(BF16)
    col = lax.broadcasted_iota(jnp.int32, x.shape, 1) // POOL_GW
    tot = jnp.zeros_like(x)
    for g in range(len(POOL_WINDOWS)):
        tot = jnp.where(col == g, jnp.dot(band_ref[g], xb, preferred_element_type=F32), tot)
    pooled = tot * inv_ref[...] - x
    y = jnp.dot(pooled.astype(BF16), w_ref[...], preferred_element_type=F32)
    return (y + b_ref[...]) * s_ref[...]


def _ffn_kernel(x_ref, m_ref, g_ref, win_ref, wout_ref, fg_ref, *rest, j0, final, mix, tm):
    x = x_ref[0]
    if mix:
        (u0_ref, hf0_ref, ga0_ref, u_ref, hf_ref, ga_ref, wg_ref, bg_ref, lam_ref, h0_ref, up_ref, yc_ref, band_ref,
         inv_ref, pw_ref, pb_ref, ps_ref, wo_ref, o_ref, hend_ref, a_s, b_s, c_s, ya_s) = rest
        i = pl.program_id(1)

        parts = len(FF_CHUNKS)
        part_rows = tm // parts

        @pl.when(i == 0)
        def _():
            _lru_coeffs(u0_ref[0], wg_ref, bg_ref, lam_ref, a_s, b_s)
            c_s[...] = _lru_scan_tile(a_s, b_s, h0_ref[0], tm=tm, reverse=True)
            ya_s[0] = (hf0_ref[0] + b_s[...]) * jax.nn.gelu(ga0_ref[0])

        def next_tile_part(j, carry):
            rows = slice(tm - (j + 1) * part_rows, tm - j * part_rows)
            per = part_rows // V7X_SUBLANES
            _lru_coeffs(u_ref[0, rows, :], wg_ref, bg_ref, lam_ref, a_s, b_s, rows)
            carry = _lru_scan_tile(a_s, b_s, carry, tm=tm, reverse=True, inline=range(j * per, (j + 1) * per))
            ya_s[(i + 1) % 2, rows, :] = (hf_ref[0, rows, :] + b_s[rows, :]) * jax.nn.gelu(ga_ref[0, rows, :])
            return carry

        ya = ya_s[i % 2]
        yb = jnp.concatenate([_pool_tile(up_ref[0, r:r + POOL_TILE, :], band_ref, inv_ref, pw_ref, pb_ref, ps_ref)
                              for r in range(0, tm, POOL_TILE)], axis=0)
        y = jnp.dot(ya.astype(BF16), wo_ref[0:W_A, :], preferred_element_type=F32)
        y = y + jnp.dot(yb.astype(BF16), wo_ref[W_A:W_A + W_B, :], preferred_element_type=F32)
        y = y + jnp.dot(yc_ref[0].astype(BF16), wo_ref[W_A + W_B:, :], preferred_element_type=F32)
        x = x + m_ref[0, 5:6, :] * y
    else:
        (o_ref,) = rest
    shift = m_ref[0, j0:j0 + 1, :]
    scale = m_ref[0, j0 + 1:j0 + 2, :]
    gate = m_ref[0, j0 + 2:j0 + 3, :]
    h = _adanorm(x, g_ref[...], shift, scale).astype(BF16)
    acc = None
    carry = c_s[...] if mix else None
    for j, (s, w) in enumerate(FF_CHUNKS):
        if mix:
            carry = next_tile_part(j, carry)
        gt = jnp.dot(h, win_ref[:, s:s + w], preferred_element_type=F32)
        up = jnp.dot(h, win_ref[:, D_FF + s:D_FF + s + w], preferred_element_type=F32)
        a = (gt * jax.nn.sigmoid(gt) * up).astype(BF16)
        p = jnp.dot(a, wout_ref[s:s + w, :], preferred_element_type=F32)
        acc = p if acc is None else acc + p
    if mix:
        c_s[...] = jnp.where(i + 1 < pl.num_programs(1), carry, c_s[...])
        hend_ref[0] = c_s[...]
    y = x + (0.5 * gate) * acc
    if final:
        ms = jnp.mean(y * y, axis=-1, keepdims=True)
        y = (y * lax.rsqrt(ms + EPS)) * fg_ref[...]
    o_ref[0] = y


def _ffn(x, mod, g, w_in, w_out, wsel, final_g, mix=None, *, j0, final, tm):
    bn, n, _ = x.shape
    last = n // tm - 1
    tile_map = (lambda b, i: (b, last - i, 0)) if mix is not None else (lambda b, i: (b, i, 0))
    tile = lambda w: pl.BlockSpec((1, tm, w), tile_map)
    const = lambda shape: pl.BlockSpec(shape, lambda b, i: (0,) * len(shape))
    in_specs = [tile(D_MODEL),
                pl.BlockSpec((1, N_MOD, D_MODEL), lambda b, i: (b, 0, 0)),
                const((1, D_MODEL)),
                _resident((None, None, D_MODEL, 2 * D_FF), lambda b, i: wsel + (0, 0)),
                _resident((None, None, D_FF, D_MODEL), lambda b, i: wsel + (0, 0)),
                const((1, D_MODEL))]
    args = [x, mod, g.reshape(1, D_MODEL), w_in, w_out, final_g.reshape(1, D_MODEL)]
    x_shape = jax.ShapeDtypeStruct(x.shape, F32)
    if mix is None:
        return pl.pallas_call(
            functools.partial(_ffn_kernel, j0=j0, final=final, mix=False, tm=tm),
            grid=(bn, n // tm), in_specs=in_specs, out_specs=tile(D_MODEL), out_shape=x_shape,
            compiler_params=_params("arbitrary", "arbitrary"), name="ffn",
        )(*args)
    u, wg, bg, lam, h0, hf, ga, up, yc, band, inv, pw, pb, ps, wo = mix
    first = _resident((1, tm, W_A), lambda b, i: (b, last, 0))
    nxt = pl.BlockSpec((1, tm, W_A), lambda b, i: (b, last - jnp.minimum(i + 1, last), 0))
    in_specs += ([first] * 3 + [nxt] * 3 + _lru_param_specs()
                 + [tile(W_B), tile(W_C), const(band.shape), const(inv.shape),
                    const((W_B, W_B)), const((1, W_B)), const((1, W_B)),
                    _resident((D_MODEL, D_MODEL), lambda b, i: (0, 0))])
    args += [u, hf, ga, u, hf, ga, wg, bg, lam, h0.reshape(bn, 1, W_A), up, yc, band, inv, pw, pb.reshape(1, W_B),
             ps.reshape(1, W_B), wo]
    out, hend = pl.pallas_call(
        functools.partial(_ffn_kernel, j0=j0, final=final, mix=True, tm=tm),
        grid=(bn, n // tm),
        in_specs=in_specs,
        out_specs=[tile(D_MODEL), pl.BlockSpec((1, 1, W_A), lambda b, i: (b, 0, 0))],
        out_shape=[x_shape, jax.ShapeDtypeStruct((bn, 1, W_A), F32)],
        scratch_shapes=_lru_scratch(tm) + [pltpu.VMEM((2, tm, W_A), F32)],
        compiler_params=_params("arbitrary", "arbitrary"),
        name="mix_ffn",
    )(*args)
    return out, hend.reshape(bn, W_A)


def _proj_kernel(xp_ref, x_ref, xn_ref, m_ref, g_ref, w_ref, cwa_ref, cba_ref, cwh_ref, cbh_ref,
                 ua_ref, ga_ref, up_ref, z_ref, pa_s, ph_s, *, tm, n_tiles):
    i = pl.program_id(1)
    norm = lambda v: _adanorm(v, g_ref[...], m_ref[0, 3:4, :], m_ref[0, 4:5, :])
    h_prev = jnp.where(i > 0, norm(xp_ref[0]), 0.0)
    h_next = jnp.where(i < n_tiles - 1, norm(xn_ref[0]), 0.0)
    h = jnp.concatenate([h_prev, norm(x_ref[0]), h_next], axis=0).astype(BF16)
    proj = lambda lo, hi: jnp.dot(h, w_ref[:, lo:hi], preferred_element_type=F32)

    def short_conv(lo, hi, w_conv_ref, b_conv_ref, taps, pad_left, p_s, out_ref):
        p_s[...] = proj(lo, hi)
        acc = b_conv_ref[...] + w_conv_ref[0:1, :] * p_s[pl.ds(HALO - pad_left, tm), :]
        for k in range(1, taps):
            acc = acc + w_conv_ref[k:k + 1, :] * p_s[pl.ds(HALO - pad_left + k, tm), :]
        out_ref[0] = acc

    short_conv(2 * W_A + W_B, D_IN, cwh_ref, cbh_ref, HY_SHORT, HY_SHORT // 2, ph_s, z_ref)
    short_conv(0, W_A, cwa_ref, cba_ref, RG_CONV, RG_CONV // 2, pa_s, ua_ref)
    rest = proj(W_A, 2 * W_A + W_B)
    ga_ref[0] = rest[HALO:HALO + tm, :W_A]
    up_ref[0] = rest[HALO:HALO + tm, W_A:]


def _proj(x, mod, g, w_in, cwa, cba, cwh, cbh, *, tm):
    bn, n, _ = x.shape
    n_tiles = n // tm
    hb = tm // HALO
    last_hb = n // HALO - 1
    outs = (W_A, W_A, W_B, 3 * W_C)
    return pl.pallas_call(
        functools.partial(_proj_kernel, tm=tm, n_tiles=n_tiles),
        grid=(bn, n_tiles),
        in_specs=[pl.BlockSpec((1, HALO, D_MODEL), lambda b, i: (b, jnp.maximum(i * hb - 1, 0), 0)),
                  pl.BlockSpec((1, tm, D_MODEL), lambda b, i: (b, i, 0)),
                  pl.BlockSpec((1, HALO, D_MODEL), lambda b, i: (b, jnp.minimum((i + 1) * hb, last_hb), 0)),
                  pl.BlockSpec((1, N_MOD, D_MODEL), lambda b, i: (b, 0, 0)),
                  pl.BlockSpec((1, D_MODEL), lambda b, i: (0, 0)),
                  _resident((D_MODEL, D_IN), lambda b, i: (0, 0)),
                  pl.BlockSpec((RG_CONV, W_A), lambda b, i: (0, 0)),
                  pl.BlockSpec((1, W_A), lambda b, i: (0, 0)),
                  pl.BlockSpec((HY_SHORT, 3 * W_C), lambda b, i: (0, 0)),
                  pl.BlockSpec((1, 3 * W_C), lambda b, i: (0, 0))],
        out_specs=[pl.BlockSpec((1, tm, w), lambda b, i: (b, i, 0)) for w in outs],
        out_shape=[jax.ShapeDtypeStruct((bn, n, w), F32) for w in outs],
        scratch_shapes=[pltpu.VMEM((tm + 2 * HALO, W_A), F32), pltpu.VMEM((tm + 2 * HALO, 3 * W_C), F32)],
        compiler_params=_params("arbitrary", "arbitrary"),
        name="proj",
    )(x, x, x, mod, g.reshape(1, D_MODEL), w_in, cwa, cba.reshape(1, W_A), cwh, cbh.reshape(1, 3 * W_C))


def _lru_coeffs(u, wg_ref, bg_ref, lam_ref, a_s, b_s, rows=slice(None)):
    pre = jnp.dot(u.astype(BF16), wg_ref[...], preferred_element_type=F32) + bg_ref[...]
    q = (-0.25 * RG_C) * jax.nn.softplus(-lam_ref[...])
    t = jnp.tanh(q * jnp.tanh(0.5 * pre[:, :W_A]) + q)
    inv = 1.0 / (1.0 - t)
    a_s[rows, :] = (1.0 + t) * inv
    b_s[rows, :] = (jnp.sqrt(-t) * inv) * ((jnp.tanh(0.5 * pre[:, W_A:]) + 1.0) * u)


def _lru_scan_tile(a_s, b_s, carry, *, tm, reverse, inline=None):
    groups = tm // V7X_SUBLANES
    rowi = lax.broadcasted_iota(jnp.int32, (V7X_SUBLANES, W_A), 0)

    def body(k, c):
        gi = (groups - 1 - k) if reverse else k
        r0 = gi * V7X_SUBLANES if inline is not None else pl.multiple_of(gi * V7X_SUBLANES, V7X_SUBLANES)
        a = a_s[pl.ds(r0, V7X_SUBLANES), :]
        b = b_s[pl.ds(r0, V7X_SUBLANES), :]
        first = rowi == (V7X_SUBLANES - 1 if reverse else 0)
        b = jnp.where(first, a * c + b, b)
        for s in (1, 2, 4):
            if reverse:
                shift, m = V7X_SUBLANES - s, rowi < V7X_SUBLANES - s
            else:
                shift, m = s, rowi >= s
            b = jnp.where(m, a * pltpu.roll(b, shift, 0) + b, b)
            if s < V7X_SUBLANES // 2:
                a = jnp.where(m, a * pltpu.roll(a, shift, 0), a)
        b_s[pl.ds(r0, V7X_SUBLANES), :] = b
        return b[0:1, :] if reverse else b[V7X_SUBLANES - 1:V7X_SUBLANES, :]

    if inline is not None:
        for k in inline:
            carry = body(k, carry)
        return carry
    return lax.fori_loop(0, groups, body, carry, unroll=4)


def _lru_scan_kernel(u_ref, wg_ref, bg_ref, lam_ref, h0_ref, h_ref, a_s, b_s, c_s, *, tm, reverse):
    @pl.when(pl.program_id(1) == 0)
    def _():
        c_s[...] = h0_ref[0]

    _lru_coeffs(u_ref[0], wg_ref, bg_ref, lam_ref, a_s, b_s)
    c_s[...] = _lru_scan_tile(a_s, b_s, c_s[...], tm=tm, reverse=reverse)
    h_ref[0] = b_s[...]


def _lru_param_specs():
    const = lambda shape: pl.BlockSpec(shape, lambda b, i: (0,) * len(shape))
    return [const((W_A, 2 * W_A)), const((1, 2 * W_A)), const((1, W_A)),
            pl.BlockSpec((1, 1, W_A), lambda b, i: (b, 0, 0))]


def _lru_scratch(tm):
    return [pltpu.VMEM((tm, W_A), F32), pltpu.VMEM((tm, W_A), F32), pltpu.VMEM((1, W_A), F32)]


def _lru_scan(u, wg, bg, lam, h0, *, tm, reverse):
    bn, n, _ = u.shape
    last = n // tm - 1
    tile_map = (lambda b, i: (b, last - i, 0)) if reverse else (lambda b, i: (b, i, 0))
    return pl.pallas_call(
        functools.partial(_lru_scan_kernel, tm=tm, reverse=reverse),
        grid=(bn, n // tm),
        in_specs=[pl.BlockSpec((1, tm, W_A), tile_map)] + _lru_param_specs(),
        out_specs=pl.BlockSpec((1, tm, W_A), tile_map),
        out_shape=jax.ShapeDtypeStruct(u.shape, F32),
        scratch_shapes=_lru_scratch(tm),
        compiler_params=_params("arbitrary", "arbitrary"),
        name="lru_scan",
    )(u, wg, bg, lam, h0.reshape(bn, 1, W_A))


def _pool_tables(row_len):
    t = np.arange(POOL_TILE)
    p = t % row_len
    base = t - p
    band = np.zeros((len(POOL_WINDOWS), POOL_TILE, POOL_TILE), np.float32)
    inv = np.zeros((POOL_TILE, W_B), np.float32)
    for g, win in enumerate(POOL_WINDOWS):
        lo = np.clip(p - win // 2, 0, row_len)
        hi = np.clip(p + win - win // 2, 0, row_len)
        s = np.arange(POOL_TILE)[None, :]
        band[g] = ((s >= (base + lo)[:, None]) & (s < (base + hi)[:, None])).astype(np.float32)
        inv[:, g * POOL_GW:(g + 1) * POOL_GW] = (1.0 / (hi - lo).astype(np.float64))[:, None]
    return _bf16_table(band), jnp.asarray(inv, F32)


def _filter_kernel(f_ref, w1_ref, b1_ref, fr_ref, w2_ref, b2_ref, w3_ref, dl_ref, o_ref, *, tm):
    feats = f_ref[...]
    fr = fr_ref[...]
    hid = jnp.sin(fr * (_dot_bf16x3(feats, w1_ref[...]) + b1_ref[...]))
    hid = jnp.sin(fr * (_dot_bf16x3(hid, w2_ref[...]) + b2_ref[...]))
    filt = _dot_bf16x3(hid, w3_ref[...])
    filt = filt * jnp.exp(-feats[:, 0:1] * dl_ref[...])
    row = pl.program_id(0) * tm + lax.broadcasted_iota(jnp.int32, filt.shape, 0)
    bwd_col = (lax.broadcasted_iota(jnp.int32, filt.shape, 1) // W_C) % 2 == 1
    o_ref[...] = jnp.where((row == 0) & bwd_col, 0.0, filt)


def _filter_features(n):
    t = jnp.linspace(0.0, 1.0, n, dtype=F32)[:, None]
    wpos = 2.0 * math.pi * jnp.arange(n, dtype=F32)[:, None] / n
    bands = jnp.linspace(1e-4, HY_BANDS - 1, HY_BANDS, dtype=F32)[None, :]
    feats = jnp.concatenate([t, jnp.cos(bands * wpos), -jnp.sin(bands * wpos)], axis=-1)
    return jnp.pad(feats, ((0, 0), (0, HY_EMB_PAD - HY_EMB)))


def _filter_decay_rates():
    deltas = jnp.abs(jnp.linspace(math.log(HY_TARGET) / HY_FAST, math.log(HY_TARGET) / HY_SLOW, W_C, dtype=F32))
    return jnp.tile(deltas, 2 * HY_ORDER).reshape(1, 2 * HY_ORDER * W_C)


def _filters(n, w1, b1, freq, w2, b2, w3):
    tm = min(n, 512)
    nf = 2 * HY_ORDER * W_C
    const = lambda shape: pl.BlockSpec(shape, lambda i: (0, 0))
    return pl.pallas_call(
        functools.partial(_filter_kernel, tm=tm),
        grid=(n // tm,),
        in_specs=[pl.BlockSpec((tm, HY_EMB_PAD), lambda i: (i, 0)),
                  const((HY_EMB_PAD, HY_FH)), const((1, HY_FH)), const((1, HY_FH)),
                  const((HY_FH, HY_FH)), const((1, HY_FH)), const((HY_FH, nf)), const((1, nf))],
        out_specs=pl.BlockSpec((tm, nf), lambda i: (i, 0)),
        out_shape=jax.ShapeDtypeStruct((n, nf), F32),
        compiler_params=_params("arbitrary"),
        name="hyena_filter",
    )(_filter_features(n), jnp.pad(w1, ((0, HY_EMB_PAD - HY_EMB), (0, 0))), b1.reshape(1, HY_FH),
      freq.reshape(1, HY_FH), w2, b2.reshape(1, HY_FH), w3, _filter_decay_rates())


T2_BLOCKS = FFT_MINOR // V7X_SUBLANES


class _FftPlan:
    def __init__(self, n):
        assert n % (2 * FFT_MINOR) == 0
        self.n = n
        self.n_fft = 2 * n
        self.n1 = self.n_fft // FFT_MINOR
        self.t1 = self.n1 // 2
        self.k1 = self.n1 // 2 + 1
        self.k_block = max(d for d in range(1, 14) if self.k1 % d == 0)
        t1 = np.arange(self.t1)
        k1 = np.arange(self.k1)
        eye = np.eye(V7X_SUBLANES)
        ang = 2.0 * np.pi * ((k1[:, None] * t1[None, :]) % self.n1) / self.n1
        fa = np.concatenate([np.cos(ang), -np.sin(ang)], axis=0)
        self.stage_a = _bf16_table(np.kron(fa, eye))
        wgt = np.where((k1 == 0) | (k1 == self.n1 // 2), 1.0, 2.0) / self.n_fft
        fc = np.concatenate([np.cos(ang) * wgt[:, None], -np.sin(ang) * wgt[:, None]], axis=0).T
        self.stage_a_inv = _bf16_table(np.kron(fc, eye))
        t2 = np.arange(FFT_MINOR)
        k2 = np.arange(FFT_MINOR)
        idx = (t2[None, None, :] * (k1[:, None, None] + self.n1 * k2[None, :, None])) % self.n_fft
        ang_b = 2.0 * np.pi * idx / self.n_fft
        gr, gi = np.cos(ang_b), -np.sin(ang_b)
        fwd = np.concatenate([np.concatenate([gr, -gi], axis=2), np.concatenate([gi, gr], axis=2)], axis=1)
        self.stage_b = _bf16_table(fwd)
        self.stage_b_inv = _bf16_table(np.transpose(fwd, (0, 2, 1)))

    def scratch(self, signals):
        return pltpu.VMEM((signals, 2, self.k1, T2_BLOCKS, V7X_SUBLANES, V7X_LANES), F32)


def _time_tiles(ref, tb, rows):
    return ref[:, pl.ds(tb, 1)].reshape(rows, V7X_LANES)


def _stage_a_forward(plan, x_refs, fa_ref, a_s):
    def body(tb, carry):
        xs = jnp.concatenate([_time_tiles(xr, tb, plan.t1 * V7X_SUBLANES) for xr in x_refs], axis=1)
        av = jnp.dot(fa_ref[...], xs.astype(BF16), preferred_element_type=F32)
        for j in range(len(x_refs)):
            part = av[:, j * V7X_LANES:(j + 1) * V7X_LANES]
            a_s[j, :, :, pl.ds(tb, 1)] = part.reshape(2, plan.k1, 1, V7X_SUBLANES, V7X_LANES)
        return carry
    lax.fori_loop(0, T2_BLOCKS, body, 0)


def _stage_b_slab(a_s, k, count):
    return jnp.concatenate(
        [a_s[j, :, pl.ds(k, 1)].reshape(2 * FFT_MINOR, V7X_LANES) for j in range(count)], axis=1)


def _spectrum_kernel(x0_ref, x1_ref, fa_ref, fb_ref, o_ref, a_s, *, plan):
    kb = pl.program_id(1)

    @pl.when(kb == 0)
    def _():
        _stage_a_forward(plan, (x0_ref, x1_ref), fa_ref, a_s)

    half = FFT_MINOR
    rows = 2 * half
    for i in range(plan.k_block):
        slab = _stage_b_slab(a_s, kb * plan.k_block + i, 2).astype(BF16)
        xs = jnp.dot(fb_ref[i], slab, preferred_element_type=F32)
        o_ref[i * rows:i * rows + half, :] = xs[:half, :V7X_LANES] + xs[:half, V7X_LANES:]
        o_ref[i * rows + half:(i + 1) * rows, :] = xs[half:, :V7X_LANES] - xs[half:, V7X_LANES:]


def _filter_spectrum(plan, filt):
    n, nf = filt.shape
    rows = 2 * FFT_MINOR
    kbs = plan.k_block
    n_cb = W_C // V7X_LANES
    filt = filt.reshape(plan.t1, T2_BLOCKS, V7X_SUBLANES, nf)
    sig = (plan.t1, T2_BLOCKS, V7X_SUBLANES, V7X_LANES)
    return pl.pallas_call(
        functools.partial(_spectrum_kernel, plan=plan),
        grid=(HY_ORDER * n_cb, plan.k1 // kbs),
        in_specs=[_resident(sig, lambda c, k: (0, 0, 0, 2 * (c // n_cb) * n_cb + c % n_cb)),
                  _resident(sig, lambda c, k: (0, 0, 0, (2 * (c // n_cb) + 1) * n_cb + c % n_cb)),
                  pl.BlockSpec(plan.stage_a.shape, lambda c, k: (0, 0)),
                  pl.BlockSpec((kbs, rows, rows), lambda c, k: (k, 0, 0))],
        out_specs=pl.BlockSpec((kbs * rows, V7X_LANES), lambda c, k: (k, c)),
        out_shape=jax.ShapeDtypeStruct((plan.k1 * rows, HY_ORDER * W_C), F32),
        scratch_shapes=[plan.scratch(2)],
        compiler_params=_params("arbitrary", "arbitrary"),
        name="filter_spectrum",
    )(filt, filt, plan.stage_a, plan.stage_b)


def _conv_kernel(v_ref, g_ref, h_ref, fa_ref, fb_ref, fbi_ref, fai_ref, sk_ref, o_ref, a_s, *, plan, bn):
    kb = pl.program_id(1)
    half = FFT_MINOR
    rows = 2 * half

    @pl.when(kb == 0)
    def _():
        _stage_a_forward(plan, [v_ref.at[b] for b in range(bn)], fa_ref, a_s)

    for i in range(plan.k_block):
        k = kb * plan.k_block + i
        xs = jnp.dot(fb_ref[i], _stage_b_slab(a_s, k, bn).astype(BF16), preferred_element_type=F32)
        xr, xi = xs[:half], xs[half:]
        hr = jnp.concatenate([h_ref[i * rows:i * rows + half, :]] * bn, axis=1)
        hi = jnp.concatenate([h_ref[i * rows + half:(i + 1) * rows, :]] * bn, axis=1)
        ys = jnp.concatenate([xr * hr - xi * hi, xr * hi + xi * hr], axis=0)
        ds = jnp.dot(fbi_ref[i], ys.astype(BF16), preferred_element_type=F32)
        for b in range(bn):
            part = ds[:, b * V7X_LANES:(b + 1) * V7X_LANES]
            a_s[b, :, pl.ds(k, 1)] = part.reshape(2, 1, T2_BLOCKS, V7X_SUBLANES, V7X_LANES)

    @pl.when(kb == pl.num_programs(1) - 1)
    def _():
        t_rows = plan.t1 * V7X_SUBLANES

        def body(tb, carry):
            ds = jnp.concatenate(
                [a_s[b, :, :, pl.ds(tb, 1)].reshape(2 * plan.k1 * V7X_SUBLANES, V7X_LANES) for b in range(bn)],
                axis=1)
            y = jnp.dot(fai_ref[...], ds.astype(BF16), preferred_element_type=F32)
            for b in range(bn):
                vv = _time_tiles(v_ref.at[b], tb, t_rows)
                gg = _time_tiles(g_ref.at[b], tb, t_rows)
                res = gg * (y[:, b * V7X_LANES:(b + 1) * V7X_LANES] + sk_ref[...] * vv)
                o_ref[b, :, pl.ds(tb, 1)] = res.reshape(plan.t1, 1, V7X_SUBLANES, V7X_LANES)
            return carry
        lax.fori_loop(0, T2_BLOCKS, body, 0)


def _hyena_conv(plan, v_arr, v_col, g_arr, g_col, spec, order, skip):
    bn = v_arr.shape[0]
    cb = V7X_LANES
    rows = 2 * FFT_MINOR
    n_cb = W_C // cb
    kbs = plan.k_block
    sig = (bn, plan.t1, T2_BLOCKS, V7X_SUBLANES, cb)
    return pl.pallas_call(
        functools.partial(_conv_kernel, plan=plan, bn=bn),
        grid=(n_cb, plan.k1 // kbs),
        in_specs=[_resident(sig, lambda c, k: (0, 0, 0, 0, v_col + c)),
                  _resident(sig, lambda c, k: (0, 0, 0, 0, g_col + c)),
                  pl.BlockSpec((kbs * rows, cb), lambda c, k: (k, order * n_cb + c)),
                  pl.BlockSpec(plan.stage_a.shape, lambda c, k: (0, 0)),
                  pl.BlockSpec((kbs, rows, rows), lambda c, k: (k, 0, 0)),
                  pl.BlockSpec((kbs, rows, rows), lambda c, k: (k, 0, 0)),
                  pl.BlockSpec(plan.stage_a_inv.shape, lambda c, k: (0, 0)),
                  pl.BlockSpec((1, cb), lambda c, k: (0, c))],
        out_specs=_resident(sig, lambda c, k: (0, 0, 0, 0, c)),
        out_shape=jax.ShapeDtypeStruct((bn, plan.t1, T2_BLOCKS, V7X_SUBLANES, W_C), F32),
        scratch_shapes=[plan.scratch(bn)],
        compiler_params=_params("arbitrary", "arbitrary"),
        name="hyena_conv",
    )(v_arr, g_arr, spec, plan.stage_a, plan.stage_b, plan.stage_b_inv, plan.stage_a_inv,
      skip.reshape(1, W_C))


def _hyena(z, n_fft_len, filt_args, skip):
    bn, n, _ = z.shape
    plan = _FftPlan(n_fft_len)
    filt = _filters(n, *filt_args)
    if n_fft_len != n:
        filt = jnp.pad(filt, ((0, n_fft_len - n), (0, 0)))
        z = jnp.pad(z, ((0, 0), (0, n_fft_len - n), (0, 0)))
    spec = _filter_spectrum(plan, filt)
    n_cb = W_C // V7X_LANES
    z = z.reshape(bn, plan.t1, T2_BLOCKS, V7X_SUBLANES, 3 * W_C)
    y1 = _hyena_conv(plan, z, 0, z, n_cb, spec, 0, skip[0])
    y2 = _hyena_conv(plan, y1, 0, z, 2 * n_cb, spec, 1, skip[1])
    return y2.reshape(bn, n_fft_len, W_C)[:, :n]


def _dense_block_diag(w):
    h, c, d = w.shape[-3:]
    eye = jnp.eye(h, dtype=w.dtype)
    dense = jnp.einsum('...hcd,hk->...hckd', w, eye)
    return dense.reshape(w.shape[:-3] + (h * c, h * d))


def _tile_rows(n):
    return min(n, 512)


def kernel(x, c, ctx, c_ctx, w_mod, b_mod, norm_g, ffn_w_in, ffn_w_out, w_in, w_out, lru_conv_w, lru_conv_b,
           lru_gate_w, lru_gate_b, lru_lambda, pool_w, pool_b, pool_scale, hy_conv_w, hy_conv_b, hy_w1, hy_b1,
           hy_freq, hy_w2, hy_b2, hy_w3, hy_skip, final_g):
    bn, n, _ = x.shape
    n_ctx = ctx.shape[1]
    depth = w_mod.shape[0]
    assert bn + 1 <= V7X_SUBLANES and n % GRID_W == 0 and n % POOL_TILE == 0 and n_ctx == POOL_TILE
    tm_x, tm_c = _tile_rows(n), _tile_rows(n_ctx)

    cvec = jnp.zeros((V7X_SUBLANES, D_MODEL), F32).at[:bn].set(c).at[bn].set(c_ctx)
    mods = _modulation(cvec, w_mod, b_mod)
    band_x, inv_x = _pool_tables(GRID_W)
    band_c, inv_c = _pool_tables(n_ctx)
    zeros = jnp.zeros((bn, W_A), F32)
    f_in = ffn_w_in.astype(BF16)
    f_out = ffn_w_out.astype(BF16)

    xc = ctx
    for l in range(depth):
        last = l == depth - 1
        mx = mods[l, :bn].reshape(bn, N_MOD, D_MODEL)
        mc = jnp.broadcast_to(mods[l, bn].reshape(1, N_MOD, D_MODEL), (bn, N_MOD, D_MODEL))
        w_in_l = w_in[l].astype(BF16)
        w_out_l = w_out[l].astype(BF16)
        gate_dense = _dense_block_diag(lru_gate_w[l])
        wg = [jnp.concatenate([gate_dense[d, 0], gate_dense[d, 1]], axis=-1).astype(BF16) for d in range(2)]
        bg = [lru_gate_b[l, d].reshape(1, 2 * W_A) for d in range(2)]
        lam = [lru_lambda[l, d].reshape(1, W_A) for d in range(2)]
        pw = _dense_block_diag(pool_w[l]).astype(BF16)
        filt_args = (hy_w1[l], hy_b1[l], hy_freq[l], hy_w2[l], hy_b2[l], hy_w3[l])
        proj_args = (w_in_l, lru_conv_w[l], lru_conv_b[l], hy_conv_w[l], hy_conv_b[l])

        def mixer(xs, mod, tm, h0f, band, inv, fft_len):
            ua, ga, up, z = _proj(xs, mod, norm_g[l, 1], *proj_args, tm=tm)
            hf = _lru_scan(ua, wg[0], bg[0], lam[0], h0f, tm=tm, reverse=False)
            yc = _hyena(z, fft_len, filt_args, hy_skip[l])
            return lambda h0b: (ua, wg[1], bg[1], lam[1], h0b, hf, ga, up, yc, band, inv, pw, pool_b[l],
                                pool_scale[l], w_out_l), hf[:, -1]

        x = _ffn(x, mx, norm_g[l, 0], f_in, f_out, (l, 0), final_g, j0=0, final=False, tm=tm_x)
        xc = _ffn(xc, mc, norm_g[l, 0], f_in, f_out, (l, 0), final_g, j0=0, final=False, tm=tm_c)
        if last:
            ua_c = _proj(xc, mc, norm_g[l, 1], *proj_args, tm=tm_c)[0]
            h0f = _lru_scan(ua_c, wg[0], bg[0], lam[0], zeros, tm=tm_c, reverse=False)[:, -1]
            h0b = _lru_scan(ua_c, wg[1], bg[1], lam[1], zeros, tm=tm_c, reverse=True)[:, 0]
        else:
            mix_c, h0f = mixer(xc, mc, tm_c, zeros, band_c, inv_c, n_ctx)
            xc, h0b = _ffn(xc, mc, norm_g[l, 2], f_in, f_out, (l, 1), final_g, mix_c(zeros), j0=6, final=False,
                           tm=tm_c)
        mix_x, _ = mixer(x, mx, tm_x, h0f, band_x, inv_x, n)
        x, _ = _ffn(x, mx, norm_g[l, 2], f_in, f_out, (l, 1), final_g, mix_x(h0b), j0=6, final=last, tm=tm_x)
    return x
```

```python
import functools
import math

import numpy as np
import jax
import jax.numpy as jnp
from jax import lax
from jax.experimental import pallas as pl
from jax.experimental.pallas import tpu as pltpu

F32 = jnp.float32
BF16 = jnp.bfloat16

D_MODEL = 1024
GRID_W = 64
N_MOD = 9
W_A, W_B, W_C = 512, 256, 256
D_IN = 2 * W_A + W_B + 3 * W_C
RG_HEADS, RG_HD, RG_CONV, RG_C = 8, 64, 4, 8.0
POOL_WINDOWS = (2, 4, 8, 16)
POOL_GW = W_B // len(POOL_WINDOWS)
HY_ORDER, HY_SHORT, HY_BANDS, HY_FH = 2, 3, 16, 64
HY_EMB = 2 * HY_BANDS + 1
HY_EMB_PAD = 40
HY_TARGET, HY_FAST, HY_SLOW = 1e-2, 0.3, 1.5
D_FF = 2816
EPS = 1e-6

V7X_LANES = 128
V7X_SUBLANES = 8
V7X_MXU_DIM = 256
V7X_VMEM_LIMIT_BYTES = 56 * 1024 * 1024

FFT_MINOR = V7X_LANES
HALO = V7X_SUBLANES
FF_CHUNKS = ((0, 768), (768, 768), (1536, 768), (2304, 512))
POOL_TILE = 256


def _params(*sem):
    return pltpu.CompilerParams(dimension_semantics=sem, vmem_limit_bytes=V7X_VMEM_LIMIT_BYTES)


def _resident(shape, index_map):
    return pl.BlockSpec(shape, index_map, pipeline_mode=pl.Buffered(1))


def _bf16_table(values):
    return jnp.asarray(values, F32).astype(BF16)


def _split_bf16(x):
    hi = x.astype(BF16)
    return hi, (x - hi.astype(F32)).astype(BF16)


def _dot_bf16x3(a, b):
    a_hi, a_lo = _split_bf16(a)
    b_hi, b_lo = _split_bf16(b)
    d = lambda p, q: jnp.dot(p, q, preferred_element_type=F32)
    return d(a_hi, b_hi) + (d(a_lo, b_hi) + d(a_hi, b_lo))


def _adanorm(x, g, shift, scale):
    ms = jnp.mean(x * x, axis=-1, keepdims=True)
    return (x * lax.rsqrt(ms + EPS)) * g * (1.0 + scale) + shift


def _mod_kernel(c_ref, w_ref, b_ref, o_ref):
    s = c_ref[...]
    s = s * jax.nn.sigmoid(s)
    o_ref[0] = jnp.dot(s.astype(BF16), w_ref[0].astype(BF16), preferred_element_type=F32) + b_ref[0]


def _modulation(cvec, w_mod, b_mod):
    depth, _, n = w_mod.shape
    tn = 1024
    return pl.pallas_call(
        _mod_kernel,
        grid=(depth, n // tn),
        in_specs=[pl.BlockSpec((V7X_SUBLANES, D_MODEL), lambda l, j: (0, 0)),
                  pl.BlockSpec((1, D_MODEL, tn), lambda l, j: (l, 0, j)),
                  pl.BlockSpec((1, 1, tn), lambda l, j: (l, 0, j))],
        out_specs=pl.BlockSpec((1, V7X_SUBLANES, tn), lambda l, j: (l, 0, j)),
        out_shape=jax.ShapeDtypeStruct((depth, V7X_SUBLANES, n), F32),
        compiler_params=_params("arbitrary", "arbitrary"),
        name="modulation",
    )(cvec, w_mod, b_mod.reshape(depth, 1, n))


def _pool_tile(x, band_ref, inv_ref, w_ref, b_ref, s_ref):
    xb = x.astype(BF16)
    col = lax.broadcasted_iota(jnp.int32, x.shape, 1) // POOL_GW
    tot = jnp.zeros_like(x)
    for g in range(len(POOL_WINDOWS)):
        tot = jnp.where(col == g, jnp.dot(band_ref[g], xb, preferred_element_type=F32), tot)
    pooled = tot * inv_ref[...] - x
    y = jnp.dot(pooled.astype(BF16), w_ref[...], preferred_element_type=F32)
    return (y + b_ref[...]) * s_ref[...]


def _ffn_kernel(x_ref, m_ref, g_ref, win_ref, wout_ref, fg_ref, *rest, j0, final, mix, tm):
    x = x_ref[0]
    if mix:
        (u0_ref, hf0_ref, ga0_ref, u_ref, hf_ref, ga_ref, wg_ref, bg_ref, lam_ref, h0_ref, up_ref, yc_ref, band_ref,
         inv_ref, pw_ref, pb_ref, ps_ref, wo_ref, o_ref, hend_ref, a_s, b_s, c_s, ya_s) = rest
        i = pl.program_id(1)

        parts = len(FF_CHUNKS)
        part_rows = tm // parts

        @pl.when(i == 0)
        def _():
            _lru_coeffs(u0_ref[0], wg_ref, bg_ref, lam_ref, a_s, b_s)
            c_s[...] = _lru_scan_tile(a_s, b_s, h0_ref[0], tm=tm, reverse=True)
            ya_s[0] = (hf0_ref[0] + b_s[...]) * jax.nn.gelu(ga0_ref[0])

        def next_tile_part(j, carry):
            rows = slice(tm - (j + 1) * part_rows, tm - j * part_rows)
            per = part_rows // V7X_SUBLANES
            _lru_coeffs(u_ref[0, rows, :], wg_ref, bg_ref, lam_ref, a_s, b_s, rows)
            carry = _lru_scan_tile(a_s, b_s, carry, tm=tm, reverse=True, inline=range(j * per, (j + 1) * per))
            ya_s[(i + 1) % 2, rows, :] = (hf_ref[0, rows, :] + b_s[rows, :]) * jax.nn.gelu(ga_ref[0, rows, :])
            return carry

        ya = ya_s[i % 2]
        yb = jnp.concatenate([_pool_tile(up_ref[0, r:r + POOL_TILE, :], band_ref, inv_ref, pw_ref, pb_ref, ps_ref)
                              for r in range(0, tm, POOL_TILE)], axis=0)
        y = jnp.dot(ya.astype(BF16), wo_ref[0:W_A, :], preferred_element_type=F32)
        y = y + jnp.dot(yb.astype(BF16), wo_ref[W_A:W_A + W_B, :], preferred_element_type=F32)
        y = y + jnp.dot(yc_ref[0].astype(BF16), wo_ref[W_A + W_B:, :], preferred_element_type=F32)
        x = x + m_ref[0, 5:6, :] * y
    else:
        (o_ref,) = rest
    shift = m_ref[0, j0:j0 + 1, :]
    scale = m_ref[0, j0 + 1:j0 + 2, :]
    gate = m_ref[0, j0 + 2:j0 + 3, :]
    h = _adanorm(x, g_ref[...], shift, scale).astype(BF16)
    acc = None
    carry = c_s[...] if mix else None
    for j, (s, w) in enumerate(FF_CHUNKS):
        if mix:
            carry = next_tile_part(j, carry)
        gt = jnp.dot(h, win_ref[:, s:s + w], preferred_element_type=F32)
        up = jnp.dot(h, win_ref[:, D_FF + s:D_FF + s + w], preferred_element_type=F32)
        a = (gt * jax.nn.sigmoid(gt) * up).astype(BF16)
        p = jnp.dot(a, wout_ref[s:s + w, :], preferred_element_type=F32)
        acc = p if acc is None else acc + p
    if mix:
        c_s[...] = jnp.where(i + 1 < pl.num_programs(1), carry, c_s[...])
        hend_ref[0] = c_s[...]
    y = x + (0.5 * gate) * acc
    if final:
        ms = jnp.mean(y * y, axis=-1, keepdims=True)
        y = (y * lax.rsqrt(ms + EPS)) * fg_ref[...]
    o_ref[0] = y


def _ffn(x, mod, g, w_in, w_out, wsel, final_g, mix=None, *, j0, final, tm):
    bn, n, _ = x.shape
    last = n // tm - 1
    tile_map = (lambda b, i: (b, last - i, 0)) if mix is not None else (lambda b, i: (b, i, 0))
    tile = lambda w: pl.BlockSpec((1, tm, w), tile_map)
    const = lambda shape: pl.BlockSpec(shape, lambda b, i: (0,) * len(shape))
    in_specs = [tile(D_MODEL),
                pl.BlockSpec((1, N_MOD, D_MODEL), lambda b, i: (b, 0, 0)),
                const((1, D_MODEL)),
                _resident((None, None, D_MODEL, 2 * D_FF), lambda b, i: wsel + (0, 0)),
                _resident((None, None, D_FF, D_MODEL), lambda b, i: wsel + (0, 0)),
                const((1, D_MODEL))]
    args = [x, mod, g.reshape(1, D_MODEL), w_in, w_out, final_g.reshape(1, D_MODEL)]
    x_shape = jax.ShapeDtypeStruct(x.shape, F32)
    if mix is None:
        return pl.pallas_call(
            functools.partial(_ffn_kernel, j0=j0, final=final, mix=False, tm=tm),
            grid=(bn, n // tm), in_specs=in_specs, out_specs=tile(D_MODEL), out_shape=x_shape,
            compiler_params=_params("arbitrary", "arbitrary"), name="ffn",
        )(*args)
    u, wg, bg, lam, h0, hf, ga, up, yc, band, inv, pw, pb, ps, wo = mix
    first = _resident((1, tm, W_A), lambda b, i: (b, last, 0))
    nxt = pl.BlockSpec((1, tm, W_A), lambda b, i: (b, last - jnp.minimum(i + 1, last), 0))
    in_specs += ([first] * 3 + [nxt] * 3 + _lru_param_specs()
                 + [tile(W_B), tile(W_C), const(band.shape), const(inv.shape),
                    const((W_B, W_B)), const((1, W_B)), const((1, W_B)),
                    _resident((D_MODEL, D_MODEL), lambda b, i: (0, 0))])
    args += [u, hf, ga, u, hf, ga, wg, bg, lam, h0.reshape(bn, 1, W_A), up, yc, band, inv, pw, pb.reshape(1, W_B),
             ps.reshape(1, W_B), wo]
    out, hend = pl.pallas_call(
        functools.partial(_ffn_kernel, j0=j0, final=final, mix=True, tm=tm),
        grid=(bn, n // tm),
        in_specs=in_specs,
        out_specs=[tile(D_MODEL), pl.BlockSpec((1, 1, W_A), lambda b, i: (b, 0, 0))],
        out_shape=[x_shape, jax.ShapeDtypeStruct((bn, 1, W_A), F32)],
        scratch_shapes=_lru_scratch(tm) + [pltpu.VMEM((2, tm, W_A), F32)],
        compiler_params=_params("arbitrary", "arbitrary"),
        name="mix_ffn",
    )(*args)
    return out, hend.reshape(bn, W_A)


def _proj_kernel(xp_ref, x_ref, xn_ref, m_ref, g_ref, w_ref, cwa_ref, cba_ref, cwh_ref, cbh_ref,
                 wg_ref, bg_ref, lam_ref, h0_ref, ua_ref, ga_ref, up_ref, z_ref, hf_ref,
                 pa_s, ph_s, a_s, b_s, c_s, uprev_s, *, tm, n_tiles):
    i = pl.program_id(1)
    ti = jnp.minimum(i, n_tiles - 1)

    @pl.when(i == 0)
    def _():
        c_s[...] = h0_ref[0]
        uprev_s[...] = jnp.zeros_like(uprev_s)

    parts = 4
    part_rows = tm // parts

    def scan_part(j, carry):
        rows = slice(j * part_rows, (j + 1) * part_rows)
        per = part_rows // V7X_SUBLANES
        _lru_coeffs(uprev_s[rows, :], wg_ref, bg_ref, lam_ref, a_s, b_s, rows)
        carry = _lru_scan_tile(a_s, b_s, carry, tm=tm, reverse=False, inline=range(j * per, (j + 1) * per))
        hf_ref[0, rows, :] = b_s[rows, :]
        return carry

    norm = lambda v: _adanorm(v, g_ref[...], m_ref[0, 3:4, :], m_ref[0, 4:5, :])
    h_prev = jnp.where(ti > 0, norm(xp_ref[0]), 0.0)
    h_next = jnp.where(ti < n_tiles - 1, norm(xn_ref[0]), 0.0)
    h = jnp.concatenate([h_prev, norm(x_ref[0]), h_next], axis=0).astype(BF16)
    proj = lambda lo, hi: jnp.dot(h, w_ref[:, lo:hi], preferred_element_type=F32)

    def short_conv(lo, hi, w_conv_ref, b_conv_ref, taps, pad_left, p_s):
        p_s[...] = proj(lo, hi)
        acc = b_conv_ref[...] + w_conv_ref[0:1, :] * p_s[pl.ds(HALO - pad_left, tm), :]
        for k in range(1, taps):
            acc = acc + w_conv_ref[k:k + 1, :] * p_s[pl.ds(HALO - pad_left + k, tm), :]
        return acc

    carry = scan_part(0, c_s[...])
    z_ref[0] = short_conv(2 * W_A + W_B, D_IN, cwh_ref, cbh_ref, HY_SHORT, HY_SHORT // 2, ph_s)
    carry = scan_part(1, carry)
    ua = short_conv(0, W_A, cwa_ref, cba_ref, RG_CONV, RG_CONV // 2, pa_s)
    ua_ref[0] = ua
    carry = scan_part(2, carry)
    rest = proj(W_A, 2 * W_A + W_B)
    ga_ref[0] = rest[HALO:HALO + tm, :W_A]
    up_ref[0] = rest[HALO:HALO + tm, W_A:]
    carry = scan_part(3, carry)
    c_s[...] = jnp.where(i > 0, carry, c_s[...])
    uprev_s[...] = ua


def _proj(x, mod, g, w_in, cwa, cba, cwh, cbh, wg, bg, lam, h0, *, tm):
    bn, n, _ = x.shape
    n_tiles = n // tm
    hb = tm // HALO
    last_hb = n // HALO - 1
    outs = (W_A, W_A, W_B, 3 * W_C)
    cur = lambda i: jnp.minimum(i, n_tiles - 1)
    const = lambda shape: pl.BlockSpec(shape, lambda b, i: (0,) * len(shape))
    return pl.pallas_call(
        functools.partial(_proj_kernel, tm=tm, n_tiles=n_tiles),
        grid=(bn, n_tiles + 1),
        in_specs=[pl.BlockSpec((1, HALO, D_MODEL), lambda b, i: (b, jnp.maximum(cur(i) * hb - 1, 0), 0)),
                  pl.BlockSpec((1, tm, D_MODEL), lambda b, i: (b, cur(i), 0)),
                  pl.BlockSpec((1, HALO, D_MODEL), lambda b, i: (b, jnp.minimum((cur(i) + 1) * hb, last_hb), 0)),
                  pl.BlockSpec((1, N_MOD, D_MODEL), lambda b, i: (b, 0, 0)),
                  const((1, D_MODEL)),
                  _resident((D_MODEL, D_IN), lambda b, i: (0, 0)),
                  const((RG_CONV, W_A)), const((1, W_A)), const((HY_SHORT, 3 * W_C)), const((1, 3 * W_C))]
                 + _lru_param_specs(),
        out_specs=[pl.BlockSpec((1, tm, w), lambda b, i: (b, cur(i), 0)) for w in outs]
                  + [pl.BlockSpec((1, tm, W_A), lambda b, i: (b, jnp.maximum(i - 1, 0), 0))],
        out_shape=[jax.ShapeDtypeStruct((bn, n, w), F32) for w in outs + (W_A,)],
        scratch_shapes=[pltpu.VMEM((tm + 2 * HALO, W_A), F32), pltpu.VMEM((tm + 2 * HALO, 3 * W_C), F32)]
                       + _lru_scratch(tm) + [pltpu.VMEM((tm, W_A), F32)],
        compiler_params=_params("arbitrary", "arbitrary"),
        name="proj_scan",
    )(x, x, x, mod, g.reshape(1, D_MODEL), w_in, cwa, cba.reshape(1, W_A), cwh, cbh.reshape(1, 3 * W_C),
      wg, bg, lam, h0.reshape(bn, 1, W_A))


def _lru_coeffs(u, wg_ref, bg_ref, lam_ref, a_s, b_s, rows=slice(None)):
    pre = jnp.dot(u.astype(BF16), wg_ref[...], preferred_element_type=F32) + bg_ref[...]
    q = (-0.25 * RG_C) * jax.nn.softplus(-lam_ref[...])
    t = jnp.tanh(q * jnp.tanh(0.5 * pre[:, :W_A]) + q)
    inv = 1.0 / (1.0 - t)
    a_s[rows, :] = (1.0 + t) * inv
    b_s[rows, :] = (jnp.sqrt(-t) * inv) * ((jnp.tanh(0.5 * pre[:, W_A:]) + 1.0) * u)


def _lru_scan_tile(a_s, b_s, carry, *, tm, reverse, inline=None):
    groups = tm // V7X_SUBLANES
    rowi = lax.broadcasted_iota(jnp.int32, (V7X_SUBLANES, W_A), 0)

    def body(k, c):
        gi = (groups - 1 - k) if reverse else k
        r0 = gi * V7X_SUBLANES if inline is not None else pl.multiple_of(gi * V7X_SUBLANES, V7X_SUBLANES)
        a = a_s[pl.ds(r0, V7X_SUBLANES), :]
        b = b_s[pl.ds(r0, V7X_SUBLANES), :]
        first = rowi == (V7X_SUBLANES - 1 if reverse else 0)
        b = jnp.where(first, a * c + b, b)
        for s in (1, 2, 4):
            if reverse:
                shift, m = V7X_SUBLANES - s, rowi < V7X_SUBLANES - s
            else:
                shift, m = s, rowi >= s
            b = jnp.where(m, a * pltpu.roll(b, shift, 0) + b, b)
            if s < V7X_SUBLANES // 2:
                a = jnp.where(m, a * pltpu.roll(a, shift, 0), a)
        b_s[pl.ds(r0, V7X_SUBLANES), :] = b
        return b[0:1, :] if reverse else b[V7X_SUBLANES - 1:V7X_SUBLANES, :]

    if inline is not None:
        for k in inline:
            carry = body(k, carry)
        return carry
    return lax.fori_loop(0, groups, body, carry, unroll=4)


def _lru_scan_kernel(u_ref, wg_ref, bg_ref, lam_ref, h0_ref, h_ref, a_s, b_s, c_s, *, tm, reverse):
    @pl.when(pl.program_id(1) == 0)
    def _():
        c_s[...] = h0_ref[0]

    _lru_coeffs(u_ref[0], wg_ref, bg_ref, lam_ref, a_s, b_s)
    c_s[...] = _lru_scan_tile(a_s, b_s, c_s[...], tm=tm, reverse=reverse)
    h_ref[0] = b_s[...]


def _lru_param_specs():
    const = lambda shape: pl.BlockSpec(shape, lambda b, i: (0,) * len(shape))
    return [const((W_A, 2 * W_A)), const((1, 2 * W_A)), const((1, W_A)),
            pl.BlockSpec((1, 1, W_A), lambda b, i: (b, 0, 0))]


def _lru_scratch(tm):
    return [pltpu.VMEM((tm, W_A), F32), pltpu.VMEM((tm, W_A), F32), pltpu.VMEM((1, W_A), F32)]


def _lru_scan(u, wg, bg, lam, h0, *, tm, reverse):
    bn, n, _ = u.shape
    last = n // tm - 1
    tile_map = (lambda b, i: (b, last - i, 0)) if reverse else (lambda b, i: (b, i, 0))
    return pl.pallas_call(
        functools.partial(_lru_scan_kernel, tm=tm, reverse=reverse),
        grid=(bn, n // tm),
        in_specs=[pl.BlockSpec((1, tm, W_A), tile_map)] + _lru_param_specs(),
        out_specs=pl.BlockSpec((1, tm, W_A), tile_map),
        out_shape=jax.ShapeDtypeStruct(u.shape, F32),
        scratch_shapes=_lru_scratch(tm),
        compiler_params=_params("arbitrary", "arbitrary"),
        name="lru_scan",
    )(u, wg, bg, lam, h0.reshape(bn, 1, W_A))


def _pool_tables(row_len):
    t = np.arange(POOL_TILE)
    p = t % row_len
    base = t - p
    band = np.zeros((len(POOL_WINDOWS), POOL_TILE, POOL_TILE), np.float32)
    inv = np.zeros((POOL_TILE, W_B), np.float32)
    for g, win in enumerate(POOL_WINDOWS):
        lo = np.clip(p - win // 2, 0, row_len)
        hi = np.clip(p + win - win // 2, 0, row_len)
        s = np.arange(POOL_TILE)[None, :]
        band[g] = ((s >= (base + lo)[:, None]) & (s < (base + hi)[:, None])).astype(np.float32)
        inv[:, g * POOL_GW:(g + 1) * POOL_GW] = (1.0 / (hi - lo).astype(np.float64))[:, None]
    return _bf16_table(band), jnp.asarray(inv, F32)


def _filter_kernel(f_ref, w1_ref, b1_ref, fr_ref, w2_ref, b2_ref, w3_ref, dl_ref, o_ref, *, tm):
    feats = f_ref[...]
    fr = fr_ref[...]
    hid = jnp.sin(fr * (_dot_bf16x3(feats, w1_ref[...]) + b1_ref[...]))
    hid = jnp.sin(fr * (_dot_bf16x3(hid, w2_ref[...]) + b2_ref[...]))
    filt = _dot_bf16x3(hid, w3_ref[...])
    filt = filt * jnp.exp(-feats[:, 0:1] * dl_ref[...])
    row = pl.program_id(0) * tm + lax.broadcasted_iota(jnp.int32, filt.shape, 0)
    bwd_col = (lax.broadcasted_iota(jnp.int32, filt.shape, 1) // W_C) % 2 == 1
    o_ref[...] = jnp.where((row == 0) & bwd_col, 0.0, filt)


def _filter_features(n):
    t = jnp.linspace(0.0, 1.0, n, dtype=F32)[:, None]
    wpos = 2.0 * math.pi * jnp.arange(n, dtype=F32)[:, None] / n
    bands = jnp.linspace(1e-4, HY_BANDS - 1, HY_BANDS, dtype=F32)[None, :]
    feats = jnp.concatenate([t, jnp.cos(bands * wpos), -jnp.sin(bands * wpos)], axis=-1)
    return jnp.pad(feats, ((0, 0), (0, HY_EMB_PAD - HY_EMB)))


def _filter_decay_rates():
    deltas = jnp.abs(jnp.linspace(math.log(HY_TARGET) / HY_FAST, math.log(HY_TARGET) / HY_SLOW, W_C, dtype=F32))
    return jnp.tile(deltas, 2 * HY_ORDER).reshape(1, 2 * HY_ORDER * W_C)


def _filters(n, w1, b1, freq, w2, b2, w3):
    tm = min(n, 512)
    nf = 2 * HY_ORDER * W_C
    const = lambda shape: pl.BlockSpec(shape, lambda i: (0, 0))
    return pl.pallas_call(
        functools.partial(_filter_kernel, tm=tm),
        grid=(n // tm,),
        in_specs=[pl.BlockSpec((tm, HY_EMB_PAD), lambda i: (i, 0)),
                  const((HY_EMB_PAD, HY_FH)), const((1, HY_FH)), const((1, HY_FH)),
                  const((HY_FH, HY_FH)), const((1, HY_FH)), const((HY_FH, nf)), const((1, nf))],
        out_specs=pl.BlockSpec((tm, nf), lambda i: (i, 0)),
        out_shape=jax.ShapeDtypeStruct((n, nf), F32),
        compiler_params=_params("arbitrary"),
        name="hyena_filter",
    )(_filter_features(n), jnp.pad(w1, ((0, HY_EMB_PAD - HY_EMB), (0, 0))), b1.reshape(1, HY_FH),
      freq.reshape(1, HY_FH), w2, b2.reshape(1, HY_FH), w3, _filter_decay_rates())


T2_BLOCKS = FFT_MINOR // V7X_SUBLANES


class _FftPlan:
    def __init__(self, n):
        assert n % (2 * FFT_MINOR) == 0
        self.n = n
        self.n_fft = 2 * n
        self.n1 = self.n_fft // FFT_MINOR
        self.t1 = self.n1 // 2
        self.k1 = self.n1 // 2 + 1
        self.k_block = max(d for d in range(1, 14) if self.k1 % d == 0)
        t1 = np.arange(self.t1)
        k1 = np.arange(self.k1)
        eye = np.eye(V7X_SUBLANES)
        ang = 2.0 * np.pi * ((k1[:, None] * t1[None, :]) % self.n1) / self.n1
        fa = np.concatenate([np.cos(ang), -np.sin(ang)], axis=0)
        self.stage_a = _bf16_table(np.kron(fa, eye))
        wgt = np.where((k1 == 0) | (k1 == self.n1 // 2), 1.0, 2.0) / self.n_fft
        fc = np.concatenate([np.cos(ang) * wgt[:, None], -np.sin(ang) * wgt[:, None]], axis=0).T
        self.stage_a_inv = _bf16_table(np.kron(fc, eye))
        t2 = np.arange(FFT_MINOR)
        k2 = np.arange(FFT_MINOR)
        idx = (t2[None, None, :] * (k1[:, None, None] + self.n1 * k2[None, :, None])) % self.n_fft
        ang_b = 2.0 * np.pi * idx / self.n_fft
        gr, gi = np.cos(ang_b), -np.sin(ang_b)
        fwd = np.concatenate([np.concatenate([gr, -gi], axis=2), np.concatenate([gi, gr], axis=2)], axis=1)
        self.stage_b = _bf16_table(fwd)
        self.stage_b_inv = _bf16_table(np.transpose(fwd, (0, 2, 1)))

    def scratch(self, signals):
        return pltpu.VMEM((signals, 2, self.k1, T2_BLOCKS, V7X_SUBLANES, V7X_LANES), F32)


def _time_tiles(ref, tb, rows):
    return ref[:, pl.ds(tb, 1)].reshape(rows, V7X_LANES)


def _stage_a_forward(plan, x_refs, fa_ref, a_s):
    def body(tb, carry):
        xs = jnp.concatenate([_time_tiles(xr, tb, plan.t1 * V7X_SUBLANES) for xr in x_refs], axis=1)
        av = jnp.dot(fa_ref[...], xs.astype(BF16), preferred_element_type=F32)
        for j in range(len(x_refs)):
            part = av[:, j * V7X_LANES:(j + 1) * V7X_LANES]
            a_s[j, :, :, pl.ds(tb, 1)] = part.reshape(2, plan.k1, 1, V7X_SUBLANES, V7X_LANES)
        return carry
    lax.fori_loop(0, T2_BLOCKS, body, 0)


def _stage_b_slab(a_s, k, count):
    return jnp.concatenate(
        [a_s[j, :, pl.ds(k, 1)].reshape(2 * FFT_MINOR, V7X_LANES) for j in range(count)], axis=1)


def _spectrum_kernel(x0_ref, x1_ref, fa_ref, fb_ref, o_ref, a_s, *, plan):
    kb = pl.program_id(1)

    @pl.when(kb == 0)
    def _():
        _stage_a_forward(plan, (x0_ref, x1_ref), fa_ref, a_s)

    half = FFT_MINOR
    rows = 2 * half
    for i in range(plan.k_block):
        slab = _stage_b_slab(a_s, kb * plan.k_block + i, 2).astype(BF16)
        xs = jnp.dot(fb_ref[i], slab, preferred_element_type=F32)
        o_ref[i * rows:i * rows + half, :] = xs[:half, :V7X_LANES] + xs[:half, V7X_LANES:]
        o_ref[i * rows + half:(i + 1) * rows, :] = xs[half:, :V7X_LANES] - xs[half:, V7X_LANES:]


def _filter_spectrum(plan, filt):
    n, nf = filt.shape
    rows = 2 * FFT_MINOR
    kbs = plan.k_block
    n_cb = W_C // V7X_LANES
    filt = filt.reshape(plan.t1, T2_BLOCKS, V7X_SUBLANES, nf)
    sig = (plan.t1, T2_BLOCKS, V7X_SUBLANES, V7X_LANES)
    return pl.pallas_call(
        functools.partial(_spectrum_kernel, plan=plan),
        grid=(HY_ORDER * n_cb, plan.k1 // kbs),
        in_specs=[_resident(sig, lambda c, k: (0, 0, 0, 2 * (c // n_cb) * n_cb + c % n_cb)),
                  _resident(sig, lambda c, k: (0, 0, 0, (2 * (c // n_cb) + 1) * n_cb + c % n_cb)),
                  pl.BlockSpec(plan.stage_a.shape, lambda c, k: (0, 0)),
                  pl.BlockSpec((kbs, rows, rows), lambda c, k: (k, 0, 0))],
        out_specs=pl.BlockSpec((kbs * rows, V7X_LANES), lambda c, k: (k, c)),
        out_shape=jax.ShapeDtypeStruct((plan.k1 * rows, HY_ORDER * W_C), F32),
        scratch_shapes=[plan.scratch(2)],
        compiler_params=_params("arbitrary", "arbitrary"),
        name="filter_spectrum",
    )(filt, filt, plan.stage_a, plan.stage_b)


def _conv_kernel(v_ref, g_ref, h_ref, fa_ref, fb_ref, fbi_ref, fai_ref, sk_ref, o_ref, a_s, *, plan, bn):
    kb = pl.program_id(1)
    half = FFT_MINOR
    rows = 2 * half

    @pl.when(kb == 0)
    def _():
        _stage_a_forward(plan, [v_ref.at[b] for b in range(bn)], fa_ref, a_s)

    for i in range(plan.k_block):
        k = kb * plan.k_block + i
        xs = jnp.dot(fb_ref[i], _stage_b_slab(a_s, k, bn).astype(BF16), preferred_element_type=F32)
        xr, xi = xs[:half], xs[half:]
        hr = jnp.concatenate([h_ref[i * rows:i * rows + half, :]] * bn, axis=1)
        hi = jnp.concatenate([h_ref[i * rows + half:(i + 1) * rows, :]] * bn, axis=1)
        ys = jnp.concatenate([xr * hr - xi * hi, xr * hi + xi * hr], axis=0)
        ds = jnp.dot(fbi_ref[i], ys.astype(BF16), preferred_element_type=F32)
        for b in range(bn):
            part = ds[:, b * V7X_LANES:(b + 1) * V7X_LANES]
            a_s[b, :, pl.ds(k, 1)] = part.reshape(2, 1, T2_BLOCKS, V7X_SUBLANES, V7X_LANES)

    @pl.when(kb == pl.num_programs(1) - 1)
    def _():
        t_rows = plan.t1 * V7X_SUBLANES

        def body(tb, carry):
            ds = jnp.concatenate(
                [a_s[b, :, :, pl.ds(tb, 1)].reshape(2 * plan.k1 * V7X_SUBLANES, V7X_LANES) for b in range(bn)],
                axis=1)
            y = jnp.dot(fai_ref[...], ds.astype(BF16), preferred_element_type=F32)
            for b in range(bn):
                vv = _time_tiles(v_ref.at[b], tb, t_rows)
                gg = _time_tiles(g_ref.at[b], tb, t_rows)
                res = gg * (y[:, b * V7X_LANES:(b + 1) * V7X_LANES] + sk_ref[...] * vv)
                o_ref[b, :, pl.ds(tb, 1)] = res.reshape(plan.t1, 1, V7X_SUBLANES, V7X_LANES)
            return carry
        lax.fori_loop(0, T2_BLOCKS, body, 0)


def _hyena_conv(plan, v_arr, v_col, g_arr, g_col, spec, order, skip):
    bn = v_arr.shape[0]
    cb = V7X_LANES
    rows = 2 * FFT_MINOR
    n_cb = W_C // cb
    kbs = plan.k_block
    sig = (bn, plan.t1, T2_BLOCKS, V7X_SUBLANES, cb)
    return pl.pallas_call(
        functools.partial(_conv_kernel, plan=plan, bn=bn),
        grid=(n_cb, plan.k1 // kbs),
        in_specs=[_resident(sig, lambda c, k: (0, 0, 0, 0, v_col + c)),
                  _resident(sig, lambda c, k: (0, 0, 0, 0, g_col + c)),
                  pl.BlockSpec((kbs * rows, cb), lambda c, k: (k, order * n_cb + c)),
                  pl.BlockSpec(plan.stage_a.shape, lambda c, k: (0, 0)),
                  pl.BlockSpec((kbs, rows, rows), lambda c, k: (k, 0, 0)),
                  pl.BlockSpec((kbs, rows, rows), lambda c, k: (k, 0, 0)),
                  pl.BlockSpec(plan.stage_a_inv.shape, lambda c, k: (0, 0)),
                  pl.BlockSpec((1, cb), lambda c, k: (0, c))],
        out_specs=_resident(sig, lambda c, k: (0, 0, 0, 0, c)),
        out_shape=jax.ShapeDtypeStruct((bn, plan.t1, T2_BLOCKS, V7X_SUBLANES, W_C), F32),
        scratch_shapes=[plan.scratch(bn)],
        compiler_params=_params("arbitrary", "arbitrary"),
        name="hyena_conv",
    )(v_arr, g_arr, spec, plan.stage_a, plan.stage_b, plan.stage_b_inv, plan.stage_a_inv,
      skip.reshape(1, W_C))


def _hyena(z, n_fft_len, filt_args, skip):
    bn, n, _ = z.shape
    plan = _FftPlan(n_fft_len)
    filt = _filters(n, *filt_args)
    if n_fft_len != n:
        filt = jnp.pad(filt, ((0, n_fft_len - n), (0, 0)))
        z = jnp.pad(z, ((0, 0), (0, n_fft_len - n), (0, 0)))
    spec = _filter_spectrum(plan, filt)
    n_cb = W_C // V7X_LANES
    z = z.reshape(bn, plan.t1, T2_BLOCKS, V7X_SUBLANES, 3 * W_C)
    y1 = _hyena_conv(plan, z, 0, z, n_cb, spec, 0, skip[0])
    y2 = _hyena_conv(plan, y1, 0, z, 2 * n_cb, spec, 1, skip[1])
    return y2.reshape(bn, n_fft_len, W_C)[:, :n]


def _dense_block_diag(w):
    h, c, d = w.shape[-3:]
    eye = jnp.eye(h, dtype=w.dtype)
    dense = jnp.einsum('...hcd,hk->...hckd', w, eye)
    return dense.reshape(w.shape[:-3] + (h * c, h * d))


def _tile_rows(n):
    return min(n, 512)


def kernel(x, c, ctx, c_ctx, w_mod, b_mod, norm_g, ffn_w_in, ffn_w_out, w_in, w_out, lru_conv_w, lru_conv_b,
           lru_gate_w, lru_gate_b, lru_lambda, pool_w, pool_b, pool_scale, hy_conv_w, hy_conv_b, hy_w1, hy_b1,
           hy_freq, hy_w2, hy_b2, hy_w3, hy_skip, final_g):
    bn, n, _ = x.shape
    n_ctx = ctx.shape[1]
    depth = w_mod.shape[0]
    assert bn + 1 <= V7X_SUBLANES and n % GRID_W == 0 and n % POOL_TILE == 0 and n_ctx == POOL_TILE
    tm_x, tm_c = _tile_rows(n), _tile_rows(n_ctx)

    cvec = jnp.zeros((V7X_SUBLANES, D_MODEL), F32).at[:bn].set(c).at[bn].set(c_ctx)
    mods = _modulation(cvec, w_mod, b_mod)
    band_x, inv_x = _pool_tables(GRID_W)
    band_c, inv_c = _pool_tables(n_ctx)
    zeros = jnp.zeros((bn, W_A), F32)
    f_in = ffn_w_in.astype(BF16)
    f_out = ffn_w_out.astype(BF16)

    xc = ctx
    for l in range(depth):
        last = l == depth - 1
        mx = mods[l, :bn].reshape(bn, N_MOD, D_MODEL)
        mc = jnp.broadcast_to(mods[l, bn].reshape(1, N_MOD, D_MODEL), (bn, N_MOD, D_MODEL))
        w_in_l = w_in[l].astype(BF16)
        w_out_l = w_out[l].astype(BF16)
        gate_dense = _dense_block_diag(lru_gate_w[l])
        wg = [jnp.concatenate([gate_dense[d, 0], gate_dense[d, 1]], axis=-1).astype(BF16) for d in range(2)]
        bg = [lru_gate_b[l, d].reshape(1, 2 * W_A) for d in range(2)]
        lam = [lru_lambda[l, d].reshape(1, W_A) for d in range(2)]
        pw = _dense_block_diag(pool_w[l]).astype(BF16)
        filt_args = (hy_w1[l], hy_b1[l], hy_freq[l], hy_w2[l], hy_b2[l], hy_w3[l])
        proj_args = (w_in_l, lru_conv_w[l], lru_conv_b[l], hy_conv_w[l], hy_conv_b[l])

        def mixer(xs, mod, tm, h0f, band, inv, fft_len):
            ua, ga, up, z, hf = _proj(xs, mod, norm_g[l, 1], *proj_args, wg[0], bg[0], lam[0], h0f, tm=tm)
            yc = _hyena(z, fft_len, filt_args, hy_skip[l])
            return lambda h0b: (ua, wg[1], bg[1], lam[1], h0b, hf, ga, up, yc, band, inv, pw, pool_b[l],
                                pool_scale[l], w_out_l), hf[:, -1]

        x = _ffn(x, mx, norm_g[l, 0], f_in, f_out, (l, 0), final_g, j0=0, final=False, tm=tm_x)
        xc = _ffn(xc, mc, norm_g[l, 0], f_in, f_out, (l, 0), final_g, j0=0, final=False, tm=tm_c)
        if last:
            ua_c, _, _, _, hf_c = _proj(xc, mc, norm_g[l, 1], *proj_args, wg[0], bg[0], lam[0], zeros, tm=tm_c)
            h0f = hf_c[:, -1]
            h0b = _lru_scan(ua_c, wg[1], bg[1], lam[1], zeros, tm=tm_c, reverse=True)[:, 0]
        else:
            mix_c, h0f = mixer(xc, mc, tm_c, zeros, band_c, inv_c, n_ctx)
            xc, h0b = _ffn(xc, mc, norm_g[l, 2], f_in, f_out, (l, 1), final_g, mix_c(zeros), j0=6, final=False,
                           tm=tm_c)
        mix_x, _ = mixer(x, mx, tm_x, h0f, band_x, inv_x, n)
        x, _ = _ffn(x, mx, norm_g[l, 2], f_in, f_out, (l, 1), final_g, mix_x(h0b), j0=6, final=last, tm=tm_x)
    return x
```

```python
import functools
import math

import numpy as np
import jax
import jax.numpy as jnp
from jax import lax
from jax.experimental import pallas as pl
from jax.experimental.pallas import tpu as pltpu

F32 = jnp.float32
BF16 = jnp.bfloat16

D_MODEL = 1024
GRID_W = 64
N_MOD = 9
W_A, W_B, W_C = 512, 256, 256
D_IN = 2 * W_A + W_B + 3 * W_C
RG_HEADS, RG_HD, RG_CONV, RG_C = 8, 64, 4, 8.0
POOL_WINDOWS = (2, 4, 8, 16)
POOL_GW = W_B // len(POOL_WINDOWS)
HY_ORDER, HY_SHORT, HY_BANDS, HY_FH = 2, 3, 16, 64
HY_EMB = 2 * HY_BANDS + 1
HY_EMB_PAD = 40
HY_TARGET, HY_FAST, HY_SLOW = 1e-2, 0.3, 1.5
D_FF = 2816
EPS = 1e-6

V7X_LANES = 128
V7X_SUBLANES = 8
V7X_MXU_DIM = 256
V7X_VMEM_LIMIT_BYTES = 56 * 1024 * 1024

FFT_MINOR = V7X_LANES
HALO = V7X_SUBLANES
FF_CHUNKS = ((0, 768), (768, 768), (1536, 768), (2304, 512))
POOL_TILE = 256


def _params(*sem):
    return pltpu.CompilerParams(dimension_semantics=sem, vmem_limit_bytes=V7X_VMEM_LIMIT_BYTES)


def _resident(shape, index_map):
    return pl.BlockSpec(shape, index_map, pipeline_mode=pl.Buffered(1))


def _bf16_table(values):
    return jnp.asarray(values, F32).astype(BF16)


def _split_bf16(x):
    hi = x.astype(BF16)
    return hi, (x - hi.astype(F32)).astype(BF16)


def _dot_bf16x3(a, b):
    a_hi, a_lo = _split_bf16(a)
    b_hi, b_lo = _split_bf16(b)
    d = lambda p, q: jnp.dot(p, q, preferred_element_type=F32)
    return d(a_hi, b_hi) + (d(a_lo, b_hi) + d(a_hi, b_lo))


def _adanorm(x, g, shift, scale):
    ms = jnp.mean(x * x, axis=-1, keepdims=True)
    return (x * lax.rsqrt(ms + EPS)) * g * (1.0 + scale) + shift


def _mod_kernel(c_ref, w_ref, b_ref, o_ref):
    s = c_ref[...]
    s = s * jax.nn.sigmoid(s)
    o_ref[0] = jnp.dot(s.astype(BF16), w_ref[0].astype(BF16), preferred_element_type=F32) + b_ref[0]


def _modulation(cvec, w_mod, b_mod):
    depth, _, n = w_mod.shape
    tn = 1024
    return pl.pallas_call(
        _mod_kernel,
        grid=(depth, n // tn),
        in_specs=[pl.BlockSpec((V7X_SUBLANES, D_MODEL), lambda l, j: (0, 0)),
                  pl.BlockSpec((1, D_MODEL, tn), lambda l, j: (l, 0, j)),
                  pl.BlockSpec((1, 1, tn), lambda l, j: (l, 0, j))],
        out_specs=pl.BlockSpec((1, V7X_SUBLANES, tn), lambda l, j: (l, 0, j)),
        out_shape=jax.ShapeDtypeStruct((depth, V7X_SUBLANES, n), F32),
        compiler_params=_params("arbitrary", "arbitrary"),
        name="modulation",
    )(cvec, w_mod, b_mod.reshape(depth, 1, n))


def _pool_tile(x, band_ref, inv_ref, w_ref, b_ref, s_ref):
    xb = x.astype(BF16)
    col = lax.broadcasted_iota(jnp.int32, x.shape, 1) // POOL_GW
    tot = jnp.zeros_like(x)
    for g in range(len(POOL_WINDOWS)):
        tot = jnp.where(col == g, jnp.dot(band_ref[g], xb, preferred_element_type=F32), tot)
    pooled = tot * inv_ref[...] - x
    y = jnp.dot(pooled.astype(BF16), w_ref[...], preferred_element_type=F32)
    return (y + b_ref[...]) * s_ref[...]


def _ffn_kernel(x_ref, m_ref, g_ref, win_ref, wout_ref, fg_ref, *rest, j0, final, mix, tm):
    x = x_ref[0]
    if mix:
        (u0_ref, hf0_ref, ga0_ref, u_ref, hf_ref, ga_ref, wg_ref, bg_ref, lam_ref, h0_ref, up_ref, yc_ref, band_ref,
         inv_ref, pw_ref, pb_ref, ps_ref, wo_ref, o_ref, hend_ref, a_s, b_s, c_s, ya_s) = rest
        i = pl.program_id(1)

        parts = len(FF_CHUNKS)
        part_rows = tm // parts

        @pl.when(i == 0)
        def _():
            _lru_coeffs(u0_ref[0], wg_ref, bg_ref, lam_ref, a_s, b_s)
            c_s[...] = _lru_scan_tile(a_s, b_s, h0_ref[0], tm=tm, reverse=True)
            ya_s[0] = (hf0_ref[0] + b_s[...]) * jax.nn.gelu(ga0_ref[0])

        def next_tile_part(j, carry):
            rows = slice(tm - (j + 1) * part_rows, tm - j * part_rows)
            per = part_rows // V7X_SUBLANES
            _lru_coeffs(u_ref[0, rows, :], wg_ref, bg_ref, lam_ref, a_s, b_s, rows)
            carry = _lru_scan_tile(a_s, b_s, carry, tm=tm, reverse=True, inline=range(j * per, (j + 1) * per))
            ya_s[(i + 1) % 2, rows, :] = (hf_ref[0, rows, :] + b_s[rows, :]) * jax.nn.gelu(ga_ref[0, rows, :])
            return carry

        ya = ya_s[i % 2]
        yb = jnp.concatenate([_pool_tile(up_ref[0, r:r + POOL_TILE, :], band_ref, inv_ref, pw_ref, pb_ref, ps_ref)
                              for r in range(0, tm, POOL_TILE)], axis=0)
        y = jnp.dot(ya.astype(BF16), wo_ref[0:W_A, :], preferred_element_type=F32)
        y = y + jnp.dot(yb.astype(BF16), wo_ref[W_A:W_A + W_B, :], preferred_element_type=F32)
        y = y + jnp.dot(yc_ref[0].astype(BF16), wo_ref[W_A + W_B:, :], preferred_element_type=F32)
        x = x + m_ref[0, 5:6, :] * y
    else:
        (o_ref,) = rest
    shift = m_ref[0, j0:j0 + 1, :]
    scale = m_ref[0, j0 + 1:j0 + 2, :]
    gate = m_ref[0, j0 + 2:j0 + 3, :]
    h = _adanorm(x, g_ref[...], shift, scale).astype(BF16)
    acc = None
    carry = c_s[...] if mix else None
    for j, (s, w) in enumerate(FF_CHUNKS):
        if mix:
            carry = next_tile_part(j, carry)
        gt = jnp.dot(h, win_ref[:, s:s + w], preferred_element_type=F32)
        up = jnp.dot(h, win_ref[:, D_FF + s:D_FF + s + w], preferred_element_type=F32)
        a = (gt * jax.nn.sigmoid(gt) * up).astype(BF16)
        p = jnp.dot(a, wout_ref[s:s + w, :], preferred_element_type=F32)
        acc = p if acc is None else acc + p
    if mix:
        c_s[...] = jnp.where(i + 1 < pl.num_programs(1), carry, c_s[...])
        hend_ref[0] = c_s[...]
    y = x + (0.5 * gate) * acc
    if final:
        ms = jnp.mean(y * y, axis=-1, keepdims=True)
        y = (y * lax.rsqrt(ms + EPS)) * fg_ref[...]
    o_ref[0] = y


def _ffn(x, mod, g, w_in, w_out, wsel, final_g, mix=None, *, j0, final, tm):
    bn, n, _ = x.shape
    last = n // tm - 1
    tile_map = (lambda b, i: (b, last - i, 0)) if mix is not None else (lambda b, i: (b, i, 0))
    tile = lambda w: pl.BlockSpec((1, tm, w), tile_map)
    const = lambda shape: pl.BlockSpec(shape, lambda b, i: (0,) * len(shape))
    in_specs = [tile(D_MODEL),
                pl.BlockSpec((1, N_MOD, D_MODEL), lambda b, i: (b, 0, 0)),
                const((1, D_MODEL)),
                _resident((None, None, D_MODEL, 2 * D_FF), lambda b, i: wsel + (0, 0)),
                _resident((None, None, D_FF, D_MODEL), lambda b, i: wsel + (0, 0)),
                const((1, D_MODEL))]
    args = [x, mod, g.reshape(1, D_MODEL), w_in, w_out, final_g.reshape(1, D_MODEL)]
    x_shape = jax.ShapeDtypeStruct(x.shape, F32)
    if mix is None:
        return pl.pallas_call(
            functools.partial(_ffn_kernel, j0=j0, final=final, mix=False, tm=tm),
            grid=(bn, n // tm), in_specs=in_specs, out_specs=tile(D_MODEL), out_shape=x_shape,
            compiler_params=_params("arbitrary", "arbitrary"), name="ffn",
        )(*args)
    u, wg, bg, lam, h0, hf, ga, up, yc, band, inv, pw, pb, ps, wo = mix
    first = _resident((1, tm, W_A), lambda b, i: (b, last, 0))
    nxt = pl.BlockSpec((1, tm, W_A), lambda b, i: (b, last - jnp.minimum(i + 1, last), 0))
    in_specs += ([first] * 3 + [nxt] * 3 + _lru_param_specs()
                 + [tile(W_B), tile(W_C), const(band.shape), const(inv.shape),
                    const((W_B, W_B)), const((1, W_B)), const((1, W_B)),
                    _resident((D_MODEL, D_MODEL), lambda b, i: (0, 0))])
    args += [u, hf, ga, u, hf, ga, wg, bg, lam, h0.reshape(bn, 1, W_A), up, yc, band, inv, pw, pb.reshape(1, W_B),
             ps.reshape(1, W_B), wo]
    out, hend = pl.pallas_call(
        functools.partial(_ffn_kernel, j0=j0, final=final, mix=True, tm=tm),
        grid=(bn, n // tm),
        in_specs=in_specs,
        out_specs=[tile(D_MODEL), pl.BlockSpec((1, 1, W_A), lambda b, i: (b, 0, 0))],
        out_shape=[x_shape, jax.ShapeDtypeStruct((bn, 1, W_A), F32)],
        scratch_shapes=_lru_scratch(tm) + [pltpu.VMEM((2, tm, W_A), F32)],
        compiler_params=_params("arbitrary", "arbitrary"),
        name="mix_ffn",
    )(*args)
    return out, hend.reshape(bn, W_A)


def _proj_kernel(xp_ref, x_ref, xn_ref, m_ref, g_ref, w_ref, cwa_ref, cba_ref, cwh_ref, cbh_ref,
                 wg_ref, bg_ref, lam_ref, h0_ref, ua_ref, ga_ref, up_ref, z_ref, hf_ref,
                 pa_s, ph_s, a_s, b_s, c_s, uprev_s, *, tm, n_tiles):
    i = pl.program_id(1)
    ti = jnp.minimum(i, n_tiles - 1)

    @pl.when(i == 0)
    def _():
        c_s[...] = h0_ref[0]
        uprev_s[...] = jnp.zeros_like(uprev_s)

    parts = 4
    part_rows = tm // parts

    def scan_part(j, carry):
        rows = slice(j * part_rows, (j + 1) * part_rows)
        per = part_rows // V7X_SUBLANES
        _lru_coeffs(uprev_s[rows, :], wg_ref, bg_ref, lam_ref, a_s, b_s, rows)
        carry = _lru_scan_tile(a_s, b_s, carry, tm=tm, reverse=False, inline=range(j * per, (j + 1) * per))
        hf_ref[0, rows, :] = b_s[rows, :]
        return carry

    norm = lambda v: _adanorm(v, g_ref[...], m_ref[0, 3:4, :], m_ref[0, 4:5, :])
    h_prev = jnp.where(ti > 0, norm(xp_ref[0]), 0.0)
    h_next = jnp.where(ti < n_tiles - 1, norm(xn_ref[0]), 0.0)
    h = jnp.concatenate([h_prev, norm(x_ref[0]), h_next], axis=0).astype(BF16)
    proj = lambda lo, hi: jnp.dot(h, w_ref[:, lo:hi], preferred_element_type=F32)

    def short_conv(lo, hi, w_conv_ref, b_conv_ref, taps, pad_left, p_s):
        p_s[...] = proj(lo, hi)
        acc = b_conv_ref[...] + w_conv_ref[0:1, :] * p_s[pl.ds(HALO - pad_left, tm), :]
        for k in range(1, taps):
            acc = acc + w_conv_ref[k:k + 1, :] * p_s[pl.ds(HALO - pad_left + k, tm), :]
        return acc

    carry = scan_part(0, c_s[...])
    z_ref[0] = short_conv(2 * W_A + W_B, D_IN, cwh_ref, cbh_ref, HY_SHORT, HY_SHORT // 2, ph_s)
    carry = scan_part(1, carry)
    ua = short_conv(0, W_A, cwa_ref, cba_ref, RG_CONV, RG_CONV // 2, pa_s)
    ua_ref[0] = ua
    carry = scan_part(2, carry)
    rest = proj(W_A, 2 * W_A + W_B)
    ga_ref[0] = rest[HALO:HALO + tm, :W_A]
    up_ref[0] = rest[HALO:HALO + tm, W_A:]
    carry = scan_part(3, carry)
    c_s[...] = jnp.where(i > 0, carry, c_s[...])
    uprev_s[...] = ua


def _proj(x, mod, g, w_in, cwa, cba, cwh, cbh, wg, bg, lam, h0, *, tm):
    bn, n, _ = x.shape
    n_tiles = n // tm
    hb = tm // HALO
    last_hb = n // HALO - 1
    outs = (W_A, W_A, W_B, 3 * W_C)
    cur = lambda i: jnp.minimum(i, n_tiles - 1)
    const = lambda shape: pl.BlockSpec(shape, lambda b, i: (0,) * len(shape))
    return pl.pallas_call(
        functools.partial(_proj_kernel, tm=tm, n_tiles=n_tiles),
        grid=(bn, n_tiles + 1),
        in_specs=[pl.BlockSpec((1, HALO, D_MODEL), lambda b, i: (b, jnp.maximum(cur(i) * hb - 1, 0), 0)),
                  pl.BlockSpec((1, tm, D_MODEL), lambda b, i: (b, cur(i), 0)),
                  pl.BlockSpec((1, HALO, D_MODEL), lambda b, i: (b, jnp.minimum((cur(i) + 1) * hb, last_hb), 0)),
                  pl.BlockSpec((1, N_MOD, D_MODEL), lambda b, i: (b, 0, 0)),
                  const((1, D_MODEL)),
                  _resident((D_MODEL, D_IN), lambda b, i: (0, 0)),
                  const((RG_CONV, W_A)), const((1, W_A)), const((HY_SHORT, 3 * W_C)), const((1, 3 * W_C))]
                 + _lru_param_specs(),
        out_specs=[pl.BlockSpec((1, tm, w), lambda b, i: (b, cur(i), 0)) for w in outs]
                  + [pl.BlockSpec((1, tm, W_A), lambda b, i: (b, jnp.maximum(i - 1, 0), 0))],
        out_shape=[jax.ShapeDtypeStruct((bn, n, w), F32) for w in outs + (W_A,)],
        scratch_shapes=[pltpu.VMEM((tm + 2 * HALO, W_A), F32), pltpu.VMEM((tm + 2 * HALO, 3 * W_C), F32)]
                       + _lru_scratch(tm) + [pltpu.VMEM((tm, W_A), F32)],
        compiler_params=_params("arbitrary", "arbitrary"),
        name="proj_scan",
    )(x, x, x, mod, g.reshape(1, D_MODEL), w_in, cwa, cba.reshape(1, W_A), cwh, cbh.reshape(1, 3 * W_C),
      wg, bg, lam, h0.reshape(bn, 1, W_A))


def _lru_coeffs(u, wg_ref, bg_ref, lam_ref, a_s, b_s, rows=slice(None)):
    pre = jnp.dot(u.astype(BF16), wg_ref[...], preferred_element_type=F32) + bg_ref[...]
    q = (-0.25 * RG_C) * jax.nn.softplus(-lam_ref[...])
    t = jnp.tanh(q * jnp.tanh(0.5 * pre[:, :W_A]) + q)
    inv = 1.0 / (1.0 - t)
    a_s[rows, :] = (1.0 + t) * inv
    b_s[rows, :] = (jnp.sqrt(-t) * inv) * ((jnp.tanh(0.5 * pre[:, W_A:]) + 1.0) * u)


def _lru_scan_tile(a_s, b_s, carry, *, tm, reverse, inline=None):
    groups = tm // V7X_SUBLANES
    rowi = lax.broadcasted_iota(jnp.int32, (V7X_SUBLANES, W_A), 0)

    def body(k, c):
        gi = (groups - 1 - k) if reverse else k
        r0 = gi * V7X_SUBLANES if inline is not None else pl.multiple_of(gi * V7X_SUBLANES, V7X_SUBLANES)
        a = a_s[pl.ds(r0, V7X_SUBLANES), :]
        b = b_s[pl.ds(r0, V7X_SUBLANES), :]
        first = rowi == (V7X_SUBLANES - 1 if reverse else 0)
        b = jnp.where(first, a * c + b, b)
        for s in (1, 2, 4):
            if reverse:
                shift, m = V7X_SUBLANES - s, rowi < V7X_SUBLANES - s
            else:
                shift, m = s, rowi >= s
            b = jnp.where(m, a * pltpu.roll(b, shift, 0) + b, b)
            if s < V7X_SUBLANES // 2:
                a = jnp.where(m, a * pltpu.roll(a, shift, 0), a)
        b_s[pl.ds(r0, V7X_SUBLANES), :] = b
        return b[0:1, :] if reverse else b[V7X_SUBLANES - 1:V7X_SUBLANES, :]

    if inline is not None:
        for k in inline:
            carry = body(k, carry)
        return carry
    return lax.fori_loop(0, groups, body, carry, unroll=4)


def _lru_scan_kernel(u_ref, wg_ref, bg_ref, lam_ref, h0_ref, h_ref, a_s, b_s, c_s, *, tm, reverse):
    @pl.when(pl.program_id(1) == 0)
    def _():
        c_s[...] = h0_ref[0]

    _lru_coeffs(u_ref[0], wg_ref, bg_ref, lam_ref, a_s, b_s)
    c_s[...] = _lru_scan_tile(a_s, b_s, c_s[...], tm=tm, reverse=reverse)
    h_ref[0] = b_s[...]


def _lru_param_specs():
    const = lambda shape: pl.BlockSpec(shape, lambda b, i: (0,) * len(shape))
    return [const((W_A, 2 * W_A)), const((1, 2 * W_A)), const((1, W_A)),
            pl.BlockSpec((1, 1, W_A), lambda b, i: (b, 0, 0))]


def _lru_scratch(tm):
    return [pltpu.VMEM((tm, W_A), F32), pltpu.VMEM((tm, W_A), F32), pltpu.VMEM((1, W_A), F32)]


def _lru_scan(u, wg, bg, lam, h0, *, tm, reverse):
    bn, n, _ = u.shape
    last = n // tm - 1
    tile_map = (lambda b, i: (b, last - i, 0)) if reverse else (lambda b, i: (b, i, 0))
    return pl.pallas_call(
        functools.partial(_lru_scan_kernel, tm=tm, reverse=reverse),
        grid=(bn, n // tm),
        in_specs=[pl.BlockSpec((1, tm, W_A), tile_map)] + _lru_param_specs(),
        out_specs=pl.BlockSpec((1, tm, W_A), tile_map),
        out_shape=jax.ShapeDtypeStruct(u.shape, F32),
        scratch_shapes=_lru_scratch(tm),
        compiler_params=_params("arbitrary", "arbitrary"),
        name="lru_scan",
    )(u, wg, bg, lam, h0.reshape(bn, 1, W_A))


def _pool_tables(row_len):
    t = np.arange(POOL_TILE)
    p = t % row_len
    base = t - p
    band = np.zeros((len(POOL_WINDOWS), POOL_TILE, POOL_TILE), np.float32)
    inv = np.zeros((POOL_TILE, W_B), np.float32)
    for g, win in enumerate(POOL_WINDOWS):
        lo = np.clip(p - win // 2, 0, row_len)
        hi = np.clip(p + win - win // 2, 0, row_len)
        s = np.arange(POOL_TILE)[None, :]
        band[g] = ((s >= (base + lo)[:, None]) & (s < (base + hi)[:, None])).astype(np.float32)
        inv[:, g * POOL_GW:(g + 1) * POOL_GW] = (1.0 / (hi - lo).astype(np.float64))[:, None]
    return _bf16_table(band), jnp.asarray(inv, F32)


def _filter_kernel(f_ref, w1_ref, b1_ref, fr_ref, w2_ref, b2_ref, w3_ref, dl_ref, o_ref):
    feats = f_ref[...]
    fr = fr_ref[...]
    hid = jnp.sin(fr * (_dot_bf16x3(feats, w1_ref[...]) + b1_ref[...]))
    hid = jnp.sin(fr * (_dot_bf16x3(hid, w2_ref[...]) + b2_ref[...]))
    filt = _dot_bf16x3(hid, w3_ref[...])
    filt = filt * jnp.exp(-feats[:, 0:1] * dl_ref[...])
    o_ref[...] = filt

    @pl.when(pl.program_id(0) == 0)
    def _():
        head = filt[0:V7X_SUBLANES, :]
        row = lax.broadcasted_iota(jnp.int32, head.shape, 0)
        bwd_col = (lax.broadcasted_iota(jnp.int32, head.shape, 1) // W_C) % 2 == 1
        o_ref[0:V7X_SUBLANES, :] = jnp.where((row == 0) & bwd_col, 0.0, head)


def _filter_features(n):
    t = jnp.linspace(0.0, 1.0, n, dtype=F32)[:, None]
    wpos = 2.0 * math.pi * jnp.arange(n, dtype=F32)[:, None] / n
    bands = jnp.linspace(1e-4, HY_BANDS - 1, HY_BANDS, dtype=F32)[None, :]
    feats = jnp.concatenate([t, jnp.cos(bands * wpos), -jnp.sin(bands * wpos)], axis=-1)
    return jnp.pad(feats, ((0, 0), (0, HY_EMB_PAD - HY_EMB)))


def _filter_decay_rates():
    deltas = jnp.abs(jnp.linspace(math.log(HY_TARGET) / HY_FAST, math.log(HY_TARGET) / HY_SLOW, W_C, dtype=F32))
    return jnp.tile(deltas, 2 * HY_ORDER).reshape(1, 2 * HY_ORDER * W_C)


def _filters(n, w1, b1, freq, w2, b2, w3):
    tm = min(n, 512)
    nf = 2 * HY_ORDER * W_C
    const = lambda shape: pl.BlockSpec(shape, lambda i: (0, 0))
    return pl.pallas_call(
        _filter_kernel,
        grid=(n // tm,),
        in_specs=[pl.BlockSpec((tm, HY_EMB_PAD), lambda i: (i, 0)),
                  const((HY_EMB_PAD, HY_FH)), const((1, HY_FH)), const((1, HY_FH)),
                  const((HY_FH, HY_FH)), const((1, HY_FH)), const((HY_FH, nf)), const((1, nf))],
        out_specs=pl.BlockSpec((tm, nf), lambda i: (i, 0)),
        out_shape=jax.ShapeDtypeStruct((n, nf), F32),
        compiler_params=_params("arbitrary"),
        name="hyena_filter",
    )(_filter_features(n), jnp.pad(w1, ((0, HY_EMB_PAD - HY_EMB), (0, 0))), b1.reshape(1, HY_FH),
      freq.reshape(1, HY_FH), w2, b2.reshape(1, HY_FH), w3, _filter_decay_rates())


T2_BLOCKS = FFT_MINOR // V7X_SUBLANES


class _FftPlan:
    def __init__(self, n):
        assert n % (2 * FFT_MINOR) == 0
        self.n = n
        self.n_fft = 2 * n
        self.n1 = self.n_fft // FFT_MINOR
        self.t1 = self.n1 // 2
        self.k1 = self.n1 // 2 + 1
        self.k_block = max(d for d in range(1, 14) if self.k1 % d == 0)
        t1 = np.arange(self.t1)
        k1 = np.arange(self.k1)
        eye = np.eye(V7X_SUBLANES)
        ang = 2.0 * np.pi * ((k1[:, None] * t1[None, :]) % self.n1) / self.n1
        fa = np.concatenate([np.cos(ang), -np.sin(ang)], axis=0)
        self.stage_a = _bf16_table(np.kron(fa, eye))
        wgt = np.where((k1 == 0) | (k1 == self.n1 // 2), 1.0, 2.0) / self.n_fft
        fc = np.concatenate([np.cos(ang) * wgt[:, None], -np.sin(ang) * wgt[:, None]], axis=0).T
        self.stage_a_inv = _bf16_table(np.kron(fc, eye))
        t2 = np.arange(FFT_MINOR)
        k2 = np.arange(FFT_MINOR)
        idx = (t2[None, None, :] * (k1[:, None, None] + self.n1 * k2[None, :, None])) % self.n_fft
        ang_b = 2.0 * np.pi * idx / self.n_fft
        gr, gi = np.cos(ang_b), -np.sin(ang_b)
        fwd = np.concatenate([np.concatenate([gr, -gi], axis=2), np.concatenate([gi, gr], axis=2)], axis=1)
        self.stage_b = _bf16_table(fwd)
        self.stage_b_inv = _bf16_table(np.transpose(fwd, (0, 2, 1)))

    def scratch(self, signals):
        return pltpu.VMEM((signals, 2, self.k1, T2_BLOCKS, V7X_SUBLANES, V7X_LANES), F32)


def _time_tiles(ref, tb, rows):
    return ref[:, pl.ds(tb, 1)].reshape(rows, V7X_LANES)


def _stage_a_forward(plan, x_refs, fa_ref, a_s):
    def body(tb, carry):
        xs = jnp.concatenate([_time_tiles(xr, tb, plan.t1 * V7X_SUBLANES) for xr in x_refs], axis=1)
        av = jnp.dot(fa_ref[...], xs.astype(BF16), preferred_element_type=F32)
        for j in range(len(x_refs)):
            part = av[:, j * V7X_LANES:(j + 1) * V7X_LANES]
            a_s[j, :, :, pl.ds(tb, 1)] = part.reshape(2, plan.k1, 1, V7X_SUBLANES, V7X_LANES)
        return carry
    lax.fori_loop(0, T2_BLOCKS, body, 0, unroll=4)


def _stage_b_slab(a_s, k, count):
    return jnp.concatenate(
        [a_s[j, :, pl.ds(k, 1)].reshape(2 * FFT_MINOR, V7X_LANES) for j in range(count)], axis=1)


def _spectrum_kernel(x0_ref, x1_ref, fa_ref, fb_ref, o_ref, a_s, *, plan):
    kb = pl.program_id(1)

    @pl.when(kb == 0)
    def _():
        _stage_a_forward(plan, (x0_ref, x1_ref), fa_ref, a_s)

    half = FFT_MINOR
    rows = 2 * half
    for i in range(plan.k_block):
        slab = _stage_b_slab(a_s, kb * plan.k_block + i, 2).astype(BF16)
        xs = jnp.dot(fb_ref[i], slab, preferred_element_type=F32)
        o_ref[i * rows:i * rows + half, :] = xs[:half, :V7X_LANES] + xs[:half, V7X_LANES:]
        o_ref[i * rows + half:(i + 1) * rows, :] = xs[half:, :V7X_LANES] - xs[half:, V7X_LANES:]


def _filter_spectrum(plan, filt):
    n, nf = filt.shape
    rows = 2 * FFT_MINOR
    kbs = plan.k_block
    n_cb = W_C // V7X_LANES
    filt = filt.reshape(plan.t1, T2_BLOCKS, V7X_SUBLANES, nf)
    sig = (plan.t1, T2_BLOCKS, V7X_SUBLANES, V7X_LANES)
    return pl.pallas_call(
        functools.partial(_spectrum_kernel, plan=plan),
        grid=(HY_ORDER * n_cb, plan.k1 // kbs),
        in_specs=[_resident(sig, lambda c, k: (0, 0, 0, 2 * (c // n_cb) * n_cb + c % n_cb)),
                  _resident(sig, lambda c, k: (0, 0, 0, (2 * (c // n_cb) + 1) * n_cb + c % n_cb)),
                  pl.BlockSpec(plan.stage_a.shape, lambda c, k: (0, 0)),
                  pl.BlockSpec((kbs, rows, rows), lambda c, k: (k, 0, 0))],
        out_specs=pl.BlockSpec((kbs * rows, V7X_LANES), lambda c, k: (k, c)),
        out_shape=jax.ShapeDtypeStruct((plan.k1 * rows, HY_ORDER * W_C), F32),
        scratch_shapes=[plan.scratch(2)],
        compiler_params=_params("arbitrary", "arbitrary"),
        name="filter_spectrum",
    )(filt, filt, plan.stage_a, plan.stage_b)


def _conv_kernel(v_ref, g_ref, h_ref, fa_ref, fb_ref, fbi_ref, fai_ref, sk_ref, o_ref, a_s, *, plan, bn):
    kb = pl.program_id(1)
    half = FFT_MINOR
    rows = 2 * half

    @pl.when(kb == 0)
    def _():
        _stage_a_forward(plan, [v_ref.at[b] for b in range(bn)], fa_ref, a_s)

    for i in range(plan.k_block):
        k = kb * plan.k_block + i
        xs = jnp.dot(fb_ref[i], _stage_b_slab(a_s, k, bn).astype(BF16), preferred_element_type=F32)
        xr, xi = xs[:half], xs[half:]
        hr = jnp.concatenate([h_ref[i * rows:i * rows + half, :]] * bn, axis=1)
        hi = jnp.concatenate([h_ref[i * rows + half:(i + 1) * rows, :]] * bn, axis=1)
        ys = jnp.concatenate([xr * hr - xi * hi, xr * hi + xi * hr], axis=0)
        ds = jnp.dot(fbi_ref[i], ys.astype(BF16), preferred_element_type=F32)
        for b in range(bn):
            part = ds[:, b * V7X_LANES:(b + 1) * V7X_LANES]
            a_s[b, :, pl.ds(k, 1)] = part.reshape(2, 1, T2_BLOCKS, V7X_SUBLANES, V7X_LANES)

    @pl.when(kb == pl.num_programs(1) - 1)
    def _():
        t_rows = plan.t1 * V7X_SUBLANES

        def body(tb, carry):
            ds = jnp.concatenate(
                [a_s[b, :, :, pl.ds(tb, 1)].reshape(2 * plan.k1 * V7X_SUBLANES, V7X_LANES) for b in range(bn)],
                axis=1)
            y = jnp.dot(fai_ref[...], ds.astype(BF16), preferred_element_type=F32)
            for b in range(bn):
                vv = _time_tiles(v_ref.at[b], tb, t_rows)
                gg = _time_tiles(g_ref.at[b], tb, t_rows)
                res = gg * (y[:, b * V7X_LANES:(b + 1) * V7X_LANES] + sk_ref[...] * vv)
                o_ref[b, :, pl.ds(tb, 1)] = res.reshape(plan.t1, 1, V7X_SUBLANES, V7X_LANES)
            return carry
        lax.fori_loop(0, T2_BLOCKS, body, 0, unroll=4)


def _hyena_conv(plan, v_arr, v_col, g_arr, g_col, spec, order, skip):
    bn = v_arr.shape[0]
    cb = V7X_LANES
    rows = 2 * FFT_MINOR
    n_cb = W_C // cb
    kbs = plan.k_block
    sig = (bn, plan.t1, T2_BLOCKS, V7X_SUBLANES, cb)
    return pl.pallas_call(
        functools.partial(_conv_kernel, plan=plan, bn=bn),
        grid=(n_cb, plan.k1 // kbs),
        in_specs=[_resident(sig, lambda c, k: (0, 0, 0, 0, v_col + c)),
                  _resident(sig, lambda c, k: (0, 0, 0, 0, g_col + c)),
                  pl.BlockSpec((kbs * rows, cb), lambda c, k: (k, order * n_cb + c)),
                  pl.BlockSpec(plan.stage_a.shape, lambda c, k: (0, 0)),
                  pl.BlockSpec((kbs, rows, rows), lambda c, k: (k, 0, 0)),
                  pl.BlockSpec((kbs, rows, rows), lambda c, k: (k, 0, 0)),
                  pl.BlockSpec(plan.stage_a_inv.shape, lambda c, k: (0, 0)),
                  pl.BlockSpec((1, cb), lambda c, k: (0, c))],
        out_specs=_resident(sig, lambda c, k: (0, 0, 0, 0, c)),
        out_shape=jax.ShapeDtypeStruct((bn, plan.t1, T2_BLOCKS, V7X_SUBLANES, W_C), F32),
        scratch_shapes=[plan.scratch(bn)],
        compiler_params=_params("arbitrary", "arbitrary"),
        name="hyena_conv",
    )(v_arr, g_arr, spec, plan.stage_a, plan.stage_b, plan.stage_b_inv, plan.stage_a_inv,
      skip.reshape(1, W_C))


def _hyena(z, n_fft_len, filt_args, skip):
    bn, n, _ = z.shape
    plan = _FftPlan(n_fft_len)
    filt = _filters(n, *filt_args)
    if n_fft_len != n:
        filt = jnp.pad(filt, ((0, n_fft_len - n), (0, 0)))
        z = jnp.pad(z, ((0, 0), (0, n_fft_len - n), (0, 0)))
    spec = _filter_spectrum(plan, filt)
    n_cb = W_C // V7X_LANES
    z = z.reshape(bn, plan.t1, T2_BLOCKS, V7X_SUBLANES, 3 * W_C)
    y1 = _hyena_conv(plan, z, 0, z, n_cb, spec, 0, skip[0])
    y2 = _hyena_conv(plan, y1, 0, z, 2 * n_cb, spec, 1, skip[1])
    return y2.reshape(bn, n_fft_len, W_C)[:, :n]


def _dense_block_diag(w):
    h, c, d = w.shape[-3:]
    eye = jnp.eye(h, dtype=w.dtype)
    dense = jnp.einsum('...hcd,hk->...hckd', w, eye)
    return dense.reshape(w.shape[:-3] + (h * c, h * d))


def _tile_rows(n):
    return min(n, 512)


def kernel(x, c, ctx, c_ctx, w_mod, b_mod, norm_g, ffn_w_in, ffn_w_out, w_in, w_out, lru_conv_w, lru_conv_b,
           lru_gate_w, lru_gate_b, lru_lambda, pool_w, pool_b, pool_scale, hy_conv_w, hy_conv_b, hy_w1, hy_b1,
           hy_freq, hy_w2, hy_b2, hy_w3, hy_skip, final_g):
    bn, n, _ = x.shape
    n_ctx = ctx.shape[1]
    depth = w_mod.shape[0]
    assert bn + 1 <= V7X_SUBLANES and n % GRID_W == 0 and n % POOL_TILE == 0 and n_ctx == POOL_TILE
    tm_x, tm_c = _tile_rows(n), _tile_rows(n_ctx)

    cvec = jnp.zeros((V7X_SUBLANES, D_MODEL), F32).at[:bn].set(c).at[bn].set(c_ctx)
    mods = _modulation(cvec, w_mod, b_mod)
    band_x, inv_x = _pool_tables(GRID_W)
    band_c, inv_c = _pool_tables(n_ctx)
    zeros = jnp.zeros((bn, W_A), F32)
    f_in = ffn_w_in.astype(BF16)
    f_out = ffn_w_out.astype(BF16)

    xc = ctx
    for l in range(depth):
        last = l == depth - 1
        mx = mods[l, :bn].reshape(bn, N_MOD, D_MODEL)
        mc = jnp.broadcast_to(mods[l, bn].reshape(1, N_MOD, D_MODEL), (bn, N_MOD, D_MODEL))
        w_in_l = w_in[l].astype(BF16)
        w_out_l = w_out[l].astype(BF16)
        gate_dense = _dense_block_diag(lru_gate_w[l])
        wg = [jnp.concatenate([gate_dense[d, 0], gate_dense[d, 1]], axis=-1).astype(BF16) for d in range(2)]
        bg = [lru_gate_b[l, d].reshape(1, 2 * W_A) for d in range(2)]
        lam = [lru_lambda[l, d].reshape(1, W_A) for d in range(2)]
        pw = _dense_block_diag(pool_w[l]).astype(BF16)
        filt_args = (hy_w1[l], hy_b1[l], hy_freq[l], hy_w2[l], hy_b2[l], hy_w3[l])
        proj_args = (w_in_l, lru_conv_w[l], lru_conv_b[l], hy_conv_w[l], hy_conv_b[l])

        def mixer(xs, mod, tm, h0f, band, inv, fft_len):
            ua, ga, up, z, hf = _proj(xs, mod, norm_g[l, 1], *proj_args, wg[0], bg[0], lam[0], h0f, tm=tm)
            yc = _hyena(z, fft_len, filt_args, hy_skip[l])
            return lambda h0b: (ua, wg[1], bg[1], lam[1], h0b, hf, ga, up, yc, band, inv, pw, pool_b[l],
                                pool_scale[l], w_out_l), hf[:, -1]

        x = _ffn(x, mx, norm_g[l, 0], f_in, f_out, (l, 0), final_g, j0=0, final=False, tm=tm_x)
        xc = _ffn(xc, mc, norm_g[l, 0], f_in, f_out, (l, 0), final_g, j0=0, final=False, tm=tm_c)
        if last:
            ua_c, _, _, _, hf_c = _proj(xc, mc, norm_g[l, 1], *proj_args, wg[0], bg[0], lam[0], zeros, tm=tm_c)
            h0f = hf_c[:, -1]
            h0b = _lru_scan(ua_c, wg[1], bg[1], lam[1], zeros, tm=tm_c, reverse=True)[:, 0]
        else:
            mix_c, h0f = mixer(xc, mc, tm_c, zeros, band_c, inv_c, n_ctx)
            xc, h0b = _ffn(xc, mc, norm_g[l, 2], f_in, f_out, (l, 1), final_g, mix_c(zeros), j0=6, final=False,
                           tm=tm_c)
        mix_x, _ = mixer(x, mx, tm_x, h0f, band_x, inv_x, n)
        x, _ = _ffn(x, mx, norm_g[l, 2], f_in, f_out, (l, 1), final_g, mix_x(h0b), j0=6, final=last, tm=tm_x)
    return x
```

```python
import functools
import math

import numpy as np
import jax
import jax.numpy as jnp
from jax import lax
from jax.experimental import pallas as pl
from jax.experimental.pallas import tpu as pltpu

F32 = jnp.float32
BF16 = jnp.bfloat16

D_MODEL = 1024
GRID_W = 64
N_MOD = 9
W_A, W_B, W_C = 512, 256, 256
D_IN = 2 * W_A + W_B + 3 * W_C
RG_HEADS, RG_HD, RG_CONV, RG_C = 8, 64, 4, 8.0
POOL_WINDOWS = (2, 4, 8, 16)
POOL_GW = W_B // len(POOL_WINDOWS)
HY_ORDER, HY_SHORT, HY_BANDS, HY_FH = 2, 3, 16, 64
HY_EMB = 2 * HY_BANDS + 1
HY_EMB_PAD = 40
HY_TARGET, HY_FAST, HY_SLOW = 1e-2, 0.3, 1.5
D_FF = 2816
EPS = 1e-6

V7X_LANES = 128
V7X_SUBLANES = 8
V7X_MXU_DIM = 256
V7X_VMEM_LIMIT_BYTES = 56 * 1024 * 1024

FFT_MINOR = V7X_LANES
HALO = V7X_SUBLANES
FF_CHUNKS = ((0, 768), (768, 768), (1536, 768), (2304, 512))
POOL_TILE = 256


def _params(*sem):
    return pltpu.CompilerParams(dimension_semantics=sem, vmem_limit_bytes=V7X_VMEM_LIMIT_BYTES)


def _resident(shape, index_map):
    return pl.BlockSpec(shape, index_map, pipeline_mode=pl.Buffered(1))


def _bf16_table(values):
    return jnp.asarray(values, F32).astype(BF16)


def _split_bf16(x):
    hi = x.astype(BF16)
    return hi, (x - hi.astype(F32)).astype(BF16)


def _dot_bf16x3(a, b):
    a_hi, a_lo = _split_bf16(a)
    b_hi, b_lo = _split_bf16(b)
    d = lambda p, q: jnp.dot(p, q, preferred_element_type=F32)
    return d(a_hi, b_hi) + (d(a_lo, b_hi) + d(a_hi, b_lo))


def _adanorm(x, g, shift, scale):
    ms = jnp.mean(x * x, axis=-1, keepdims=True)
    return (x * lax.rsqrt(ms + EPS)) * g * (1.0 + scale) + shift


def _mod_kernel(c_ref, w_ref, b_ref, o_ref):
    s = c_ref[...]
    s = s * jax.nn.sigmoid(s)
    o_ref[0] = jnp.dot(s.astype(BF16), w_ref[0].astype(BF16), preferred_element_type=F32) + b_ref[0]


def _modulation(cvec, w_mod, b_mod):
    depth, _, n = w_mod.shape
    tn = 1024
    return pl.pallas_call(
        _mod_kernel,
        grid=(depth, n // tn),
        in_specs=[pl.BlockSpec((V7X_SUBLANES, D_MODEL), lambda l, j: (0, 0)),
                  pl.BlockSpec((1, D_MODEL, tn), lambda l, j: (l, 0, j)),
                  pl.BlockSpec((1, 1, tn), lambda l, j: (l, 0, j))],
        out_specs=pl.BlockSpec((1, V7X_SUBLANES, tn), lambda l, j: (l, 0, j)),
        out_shape=jax.ShapeDtypeStruct((depth, V7X_SUBLANES, n), F32),
        compiler_params=_params("arbitrary", "arbitrary"),
        name="modulation",
    )(cvec, w_mod, b_mod.reshape(depth, 1, n))


def _pool_tile(x, band_ref, inv_ref, w_ref, b_ref, s_ref):
    xb = x.astype(BF16)
    col = lax.broadcasted_iota(jnp.int32, x.shape, 1) // POOL_GW
    tot = jnp.zeros_like(x)
    for g in range(len(POOL_WINDOWS)):
        tot = jnp.where(col == g, jnp.dot(band_ref[g], xb, preferred_element_type=F32), tot)
    pooled = tot * inv_ref[...] - x
    y = jnp.dot(pooled.astype(BF16), w_ref[...], preferred_element_type=F32)
    return (y + b_ref[...]) * s_ref[...]


def _ffn_kernel(x_ref, m_ref, g_ref, win_ref, wout_ref, fg_ref, *rest, j0, final, mix, tm):
    x = x_ref[0]
    if mix:
        (u0_ref, hf0_ref, ga0_ref, u_ref, hf_ref, ga_ref, wg_ref, bg_ref, lam_ref, h0_ref, up_ref, yc_ref, band_ref,
         inv_ref, pw_ref, pb_ref, ps_ref, wo_ref, o_ref, hend_ref, a_s, b_s, c_s, ya_s) = rest
        i = pl.program_id(1)

        parts = len(FF_CHUNKS)
        part_rows = tm // parts

        @pl.when(i == 0)
        def _():
            _lru_coeffs(u0_ref[0], wg_ref, bg_ref, lam_ref, a_s, b_s)
            c_s[...] = _lru_scan_tile(a_s, b_s, h0_ref[0], tm=tm, reverse=True)
            ya_s[0] = (hf0_ref[0] + b_s[...]) * jax.nn.gelu(ga0_ref[0])

        def next_tile_part(j, carry):
            rows = slice(tm - (j + 1) * part_rows, tm - j * part_rows)
            per = part_rows // V7X_SUBLANES
            _lru_coeffs(u_ref[0, rows, :], wg_ref, bg_ref, lam_ref, a_s, b_s, rows)
            carry = _lru_scan_tile(a_s, b_s, carry, tm=tm, reverse=True, inline=range(j * per, (j + 1) * per))
            ya_s[(i + 1) % 2, rows, :] = (hf_ref[0, rows, :] + b_s[rows, :]) * jax.nn.gelu(ga_ref[0, rows, :])
            return carry

        ya = ya_s[i % 2]
        yb = jnp.concatenate([_pool_tile(up_ref[0, r:r + POOL_TILE, :], band_ref, inv_ref, pw_ref, pb_ref, ps_ref)
                              for r in range(0, tm, POOL_TILE)], axis=0)
        y = jnp.dot(ya.astype(BF16), wo_ref[0:W_A, :], preferred_element_type=F32)
        y = y + jnp.dot(yb.astype(BF16), wo_ref[W_A:W_A + W_B, :], preferred_element_type=F32)
        y = y + jnp.dot(yc_ref[0].astype(BF16), wo_ref[W_A + W_B:, :], preferred_element_type=F32)
        x = x + m_ref[0, 5:6, :] * y
    else:
        (o_ref,) = rest
    shift = m_ref[0, j0:j0 + 1, :]
    scale = m_ref[0, j0 + 1:j0 + 2, :]
    gate = m_ref[0, j0 + 2:j0 + 3, :]
    h = _adanorm(x, g_ref[...], shift, scale).astype(BF16)
    acc = None
    carry = c_s[...] if mix else None
    for j, (s, w) in enumerate(FF_CHUNKS):
        if mix:
            carry = next_tile_part(j, carry)
        gt = jnp.dot(h, win_ref[:, s:s + w], preferred_element_type=F32)
        up = jnp.dot(h, win_ref[:, D_FF + s:D_FF + s + w], preferred_element_type=F32)
        a = (gt * jax.nn.sigmoid(gt) * up).astype(BF16)
        p = jnp.dot(a, wout_ref[s:s + w, :], preferred_element_type=F32)
        acc = p if acc is None else acc + p
    if mix:
        c_s[...] = jnp.where(i + 1 < pl.num_programs(1), carry, c_s[...])
        hend_ref[0] = c_s[...]
    y = x + (0.5 * gate) * acc
    if final:
        ms = jnp.mean(y * y, axis=-1, keepdims=True)
        y = (y * lax.rsqrt(ms + EPS)) * fg_ref[...]
    o_ref[0] = y


def _ffn(x, mod, g, w_in, w_out, wsel, final_g, mix=None, *, j0, final, tm):
    bn, n, _ = x.shape
    last = n // tm - 1
    tile_map = (lambda b, i: (b, last - i, 0)) if mix is not None else (lambda b, i: (b, i, 0))
    tile = lambda w: pl.BlockSpec((1, tm, w), tile_map)
    const = lambda shape: pl.BlockSpec(shape, lambda b, i: (0,) * len(shape))
    in_specs = [tile(D_MODEL),
                pl.BlockSpec((1, N_MOD, D_MODEL), lambda b, i: (b, 0, 0)),
                const((1, D_MODEL)),
                _resident((None, None, D_MODEL, 2 * D_FF), lambda b, i: wsel + (0, 0)),
                _resident((None, None, D_FF, D_MODEL), lambda b, i: wsel + (0, 0)),
                const((1, D_MODEL))]
    args = [x, mod, g.reshape(1, D_MODEL), w_in, w_out, final_g.reshape(1, D_MODEL)]
    x_shape = jax.ShapeDtypeStruct(x.shape, F32)
    if mix is None:
        return pl.pallas_call(
            functools.partial(_ffn_kernel, j0=j0, final=final, mix=False, tm=tm),
            grid=(bn, n // tm), in_specs=in_specs, out_specs=tile(D_MODEL), out_shape=x_shape,
            compiler_params=_params("arbitrary", "arbitrary"), name="ffn",
        )(*args)
    u, wg, bg, lam, h0, hf, ga, up, yc, band, inv, pw, pb, ps, wo = mix
    first = _resident((1, tm, W_A), lambda b, i: (b, last, 0))
    nxt = pl.BlockSpec((1, tm, W_A), lambda b, i: (b, last - jnp.minimum(i + 1, last), 0))
    in_specs += ([first] * 3 + [nxt] * 3 + _lru_param_specs()
                 + [tile(W_B), tile(W_C), const(band.shape), const(inv.shape),
                    const((W_B, W_B)), const((1, W_B)), const((1, W_B)),
                    _resident((D_MODEL, D_MODEL), lambda b, i: (0, 0))])
    args += [u, hf, ga, u, hf, ga, wg, bg, lam, h0.reshape(bn, 1, W_A), up, yc, band, inv, pw, pb.reshape(1, W_B),
             ps.reshape(1, W_B), wo]
    out, hend = pl.pallas_call(
        functools.partial(_ffn_kernel, j0=j0, final=final, mix=True, tm=tm),
        grid=(bn, n // tm),
        in_specs=in_specs,
        out_specs=[tile(D_MODEL), pl.BlockSpec((1, 1, W_A), lambda b, i: (b, 0, 0))],
        out_shape=[x_shape, jax.ShapeDtypeStruct((bn, 1, W_A), F32)],
        scratch_shapes=_lru_scratch(tm) + [pltpu.VMEM((2, tm, W_A), F32)],
        compiler_params=_params("arbitrary", "arbitrary"),
        name="mix_ffn",
    )(*args)
    return out, hend.reshape(bn, W_A)


def _proj_kernel(xp_ref, x_ref, xn_ref, m_ref, g_ref, w_ref, cwa_ref, cba_ref, cwh_ref, cbh_ref,
                 wg_ref, bg_ref, lam_ref, h0_ref, ua_ref, ga_ref, up_ref, z_ref, hf_ref,
                 pa_s, ph_s, a_s, b_s, c_s, uprev_s, *, tm, n_tiles):
    i = pl.program_id(1)
    ti = jnp.minimum(i, n_tiles - 1)

    @pl.when(i == 0)
    def _():
        c_s[...] = h0_ref[0]
        uprev_s[...] = jnp.zeros_like(uprev_s)

    parts = 4
    part_rows = tm // parts

    def scan_part(j, carry):
        rows = slice(j * part_rows, (j + 1) * part_rows)
        per = part_rows // V7X_SUBLANES
        _lru_coeffs(uprev_s[rows, :], wg_ref, bg_ref, lam_ref, a_s, b_s, rows)
        carry = _lru_scan_tile(a_s, b_s, carry, tm=tm, reverse=False, inline=range(j * per, (j + 1) * per))
        hf_ref[0, rows, :] = b_s[rows, :]
        return carry

    norm = lambda v: _adanorm(v, g_ref[...], m_ref[0, 3:4, :], m_ref[0, 4:5, :])
    h_prev = jnp.where(ti > 0, norm(xp_ref[0]), 0.0)
    h_next = jnp.where(ti < n_tiles - 1, norm(xn_ref[0]), 0.0)
    h = jnp.concatenate([h_prev, norm(x_ref[0]), h_next], axis=0).astype(BF16)
    proj = lambda lo, hi: jnp.dot(h, w_ref[:, lo:hi], preferred_element_type=F32)

    def short_conv(lo, hi, w_conv_ref, b_conv_ref, taps, pad_left, p_s):
        p_s[...] = proj(lo, hi)
        acc = b_conv_ref[...] + w_conv_ref[0:1, :] * p_s[pl.ds(HALO - pad_left, tm), :]
        for k in range(1, taps):
            acc = acc + w_conv_ref[k:k + 1, :] * p_s[pl.ds(HALO - pad_left + k, tm), :]
        return acc

    carry = scan_part(0, c_s[...])
    z_ref[0] = short_conv(2 * W_A + W_B, D_IN, cwh_ref, cbh_ref, HY_SHORT, HY_SHORT // 2, ph_s)
    carry = scan_part(1, carry)
    ua = short_conv(0, W_A, cwa_ref, cba_ref, RG_CONV, RG_CONV // 2, pa_s)
    ua_ref[0] = ua
    carry = scan_part(2, carry)
    rest = proj(W_A, 2 * W_A + W_B)
    ga_ref[0] = rest[HALO:HALO + tm, :W_A]
    up_ref[0] = rest[HALO:HALO + tm, W_A:]
    carry = scan_part(3, carry)
    c_s[...] = jnp.where(i > 0, carry, c_s[...])
    uprev_s[...] = ua


def _proj(x, mod, g, w_in, cwa, cba, cwh, cbh, wg, bg, lam, h0, *, tm):
    bn, n, _ = x.shape
    n_tiles = n // tm
    hb = tm // HALO
    last_hb = n // HALO - 1
    outs = (W_A, W_A, W_B, 3 * W_C)
    cur = lambda i: jnp.minimum(i, n_tiles - 1)
    const = lambda shape: pl.BlockSpec(shape, lambda b, i: (0,) * len(shape))
    return pl.pallas_call(
        functools.partial(_proj_kernel, tm=tm, n_tiles=n_tiles),
        grid=(bn, n_tiles + 1),
        in_specs=[pl.BlockSpec((1, HALO, D_MODEL), lambda b, i: (b, jnp.maximum(cur(i) * hb - 1, 0), 0)),
                  pl.BlockSpec((1, tm, D_MODEL), lambda b, i: (b, cur(i), 0)),
                  pl.BlockSpec((1, HALO, D_MODEL), lambda b, i: (b, jnp.minimum((cur(i) + 1) * hb, last_hb), 0)),
                  pl.BlockSpec((1, N_MOD, D_MODEL), lambda b, i: (b, 0, 0)),
                  const((1, D_MODEL)),
                  _resident((D_MODEL, D_IN), lambda b, i: (0, 0)),
                  const((RG_CONV, W_A)), const((1, W_A)), const((HY_SHORT, 3 * W_C)), const((1, 3 * W_C))]
                 + _lru_param_specs(),
        out_specs=[pl.BlockSpec((1, tm, w), lambda b, i: (b, cur(i), 0)) for w in outs]
                  + [pl.BlockSpec((1, tm, W_A), lambda b, i: (b, jnp.maximum(i - 1, 0), 0))],
        out_shape=[jax.ShapeDtypeStruct((bn, n, w), F32) for w in outs + (W_A,)],
        scratch_shapes=[pltpu.VMEM((tm + 2 * HALO, W_A), F32), pltpu.VMEM((tm + 2 * HALO, 3 * W_C), F32)]
                       + _lru_scratch(tm) + [pltpu.VMEM((tm, W_A), F32)],
        compiler_params=_params("arbitrary", "arbitrary"),
        name="proj_scan",
    )(x, x, x, mod, g.reshape(1, D_MODEL), w_in, cwa, cba.reshape(1, W_A), cwh, cbh.reshape(1, 3 * W_C),
      wg, bg, lam, h0.reshape(bn, 1, W_A))


def _lru_coeffs(u, wg_ref, bg_ref, lam_ref, a_s, b_s, rows=slice(None)):
    ub = u.astype(BF16)
    blk = V7X_MXU_DIM

    def gate_block(c):
        r0 = (c * blk) % W_A
        return jnp.dot(ub[:, r0:r0 + blk], wg_ref[r0:r0 + blk, c * blk:(c + 1) * blk], preferred_element_type=F32)

    pre = jnp.concatenate([gate_block(c) for c in range(2 * W_A // blk)], axis=1) + bg_ref[...]
    q = (-0.25 * RG_C) * jax.nn.softplus(-lam_ref[...])
    t = jnp.tanh(q * jnp.tanh(0.5 * pre[:, :W_A]) + q)
    inv = 1.0 / (1.0 - t)
    a_s[rows, :] = (1.0 + t) * inv
    b_s[rows, :] = (jnp.sqrt(-t) * inv) * ((jnp.tanh(0.5 * pre[:, W_A:]) + 1.0) * u)


def _lru_scan_tile(a_s, b_s, carry, *, tm, reverse, inline=None):
    groups = tm // V7X_SUBLANES
    rowi = lax.broadcasted_iota(jnp.int32, (V7X_SUBLANES, W_A), 0)

    def body(k, c):
        gi = (groups - 1 - k) if reverse else k
        r0 = gi * V7X_SUBLANES if inline is not None else pl.multiple_of(gi * V7X_SUBLANES, V7X_SUBLANES)
        a = a_s[pl.ds(r0, V7X_SUBLANES), :]
        b = b_s[pl.ds(r0, V7X_SUBLANES), :]
        first = rowi == (V7X_SUBLANES - 1 if reverse else 0)
        b = jnp.where(first, a * c + b, b)
        for s in (1, 2, 4):
            if reverse:
                shift, m = V7X_SUBLANES - s, rowi < V7X_SUBLANES - s
            else:
                shift, m = s, rowi >= s
            b = jnp.where(m, a * pltpu.roll(b, shift, 0) + b, b)
            if s < V7X_SUBLANES // 2:
                a = jnp.where(m, a * pltpu.roll(a, shift, 0), a)
        b_s[pl.ds(r0, V7X_SUBLANES), :] = b
        return b[0:1, :] if reverse else b[V7X_SUBLANES - 1:V7X_SUBLANES, :]

    if inline is not None:
        for k in inline:
            carry = body(k, carry)
        return carry
    return lax.fori_loop(0, groups, body, carry, unroll=4)


def _lru_scan_kernel(u_ref, wg_ref, bg_ref, lam_ref, h0_ref, h_ref, a_s, b_s, c_s, *, tm, reverse):
    @pl.when(pl.program_id(1) == 0)
    def _():
        c_s[...] = h0_ref[0]

    _lru_coeffs(u_ref[0], wg_ref, bg_ref, lam_ref, a_s, b_s)
    c_s[...] = _lru_scan_tile(a_s, b_s, c_s[...], tm=tm, reverse=reverse)
    h_ref[0] = b_s[...]


def _lru_param_specs():
    const = lambda shape: pl.BlockSpec(shape, lambda b, i: (0,) * len(shape))
    return [const((W_A, 2 * W_A)), const((1, 2 * W_A)), const((1, W_A)),
            pl.BlockSpec((1, 1, W_A), lambda b, i: (b, 0, 0))]


def _lru_scratch(tm):
    return [pltpu.VMEM((tm, W_A), F32), pltpu.VMEM((tm, W_A), F32), pltpu.VMEM((1, W_A), F32)]


def _lru_scan(u, wg, bg, lam, h0, *, tm, reverse):
    bn, n, _ = u.shape
    last = n // tm - 1
    tile_map = (lambda b, i: (b, last - i, 0)) if reverse else (lambda b, i: (b, i, 0))
    return pl.pallas_call(
        functools.partial(_lru_scan_kernel, tm=tm, reverse=reverse),
        grid=(bn, n // tm),
        in_specs=[pl.BlockSpec((1, tm, W_A), tile_map)] + _lru_param_specs(),
        out_specs=pl.BlockSpec((1, tm, W_A), tile_map),
        out_shape=jax.ShapeDtypeStruct(u.shape, F32),
        scratch_shapes=_lru_scratch(tm),
        compiler_params=_params("arbitrary", "arbitrary"),
        name="lru_scan",
    )(u, wg, bg, lam, h0.reshape(bn, 1, W_A))


def _pool_tables(row_len):
    t = np.arange(POOL_TILE)
    p = t % row_len
    base = t - p
    band = np.zeros((len(POOL_WINDOWS), POOL_TILE, POOL_TILE), np.float32)
    inv = np.zeros((POOL_TILE, W_B), np.float32)
    for g, win in enumerate(POOL_WINDOWS):
        lo = np.clip(p - win // 2, 0, row_len)
        hi = np.clip(p + win - win // 2, 0, row_len)
        s = np.arange(POOL_TILE)[None, :]
        band[g] = ((s >= (base + lo)[:, None]) & (s < (base + hi)[:, None])).astype(np.float32)
        inv[:, g * POOL_GW:(g + 1) * POOL_GW] = (1.0 / (hi - lo).astype(np.float64))[:, None]
    return _bf16_table(band), jnp.asarray(inv, F32)


def _filter_kernel(f_ref, w1_ref, b1_ref, fr_ref, w2_ref, b2_ref, w3_ref, dl_ref, o_ref):
    feats = f_ref[...]
    fr = fr_ref[...]
    hid = jnp.sin(fr * (_dot_bf16x3(feats, w1_ref[...]) + b1_ref[...]))
    hid = jnp.sin(fr * (_dot_bf16x3(hid, w2_ref[...]) + b2_ref[...]))
    filt = _dot_bf16x3(hid, w3_ref[...])
    filt = filt * jnp.exp(-feats[:, 0:1] * dl_ref[...])
    o_ref[...] = filt

    @pl.when(pl.program_id(0) == 0)
    def _():
        head = filt[0:V7X_SUBLANES, :]
        row = lax.broadcasted_iota(jnp.int32, head.shape, 0)
        bwd_col = (lax.broadcasted_iota(jnp.int32, head.shape, 1) // W_C) % 2 == 1
        o_ref[0:V7X_SUBLANES, :] = jnp.where((row == 0) & bwd_col, 0.0, head)


def _filter_features(n):
    t = jnp.linspace(0.0, 1.0, n, dtype=F32)[:, None]
    wpos = 2.0 * math.pi * jnp.arange(n, dtype=F32)[:, None] / n
    bands = jnp.linspace(1e-4, HY_BANDS - 1, HY_BANDS, dtype=F32)[None, :]
    feats = jnp.concatenate([t, jnp.cos(bands * wpos), -jnp.sin(bands * wpos)], axis=-1)
    return jnp.pad(feats, ((0, 0), (0, HY_EMB_PAD - HY_EMB)))


def _filter_decay_rates():
    deltas = jnp.abs(jnp.linspace(math.log(HY_TARGET) / HY_FAST, math.log(HY_TARGET) / HY_SLOW, W_C, dtype=F32))
    return jnp.tile(deltas, 2 * HY_ORDER).reshape(1, 2 * HY_ORDER * W_C)


def _filters(n, w1, b1, freq, w2, b2, w3):
    tm = min(n, 512)
    nf = 2 * HY_ORDER * W_C
    const = lambda shape: pl.BlockSpec(shape, lambda i: (0, 0))
    return pl.pallas_call(
        _filter_kernel,
        grid=(n // tm,),
        in_specs=[pl.BlockSpec((tm, HY_EMB_PAD), lambda i: (i, 0)),
                  const((HY_EMB_PAD, HY_FH)), const((1, HY_FH)), const((1, HY_FH)),
                  const((HY_FH, HY_FH)), const((1, HY_FH)), const((HY_FH, nf)), const((1, nf))],
        out_specs=pl.BlockSpec((tm, nf), lambda i: (i, 0)),
        out_shape=jax.ShapeDtypeStruct((n, nf), F32),
        compiler_params=_params("arbitrary"),
        name="hyena_filter",
    )(_filter_features(n), jnp.pad(w1, ((0, HY_EMB_PAD - HY_EMB), (0, 0))), b1.reshape(1, HY_FH),
      freq.reshape(1, HY_FH), w2, b2.reshape(1, HY_FH), w3, _filter_decay_rates())


T2_BLOCKS = FFT_MINOR // V7X_SUBLANES


class _FftPlan:
    def __init__(self, n):
        assert n % (2 * FFT_MINOR) == 0
        self.n = n
        self.n_fft = 2 * n
        self.n1 = self.n_fft // FFT_MINOR
        self.t1 = self.n1 // 2
        self.k1 = self.n1 // 2 + 1
        self.k_block = max(d for d in range(1, 14) if self.k1 % d == 0)
        t1 = np.arange(self.t1)
        k1 = np.arange(self.k1)
        eye = np.eye(V7X_SUBLANES)
        ang = 2.0 * np.pi * ((k1[:, None] * t1[None, :]) % self.n1) / self.n1
        fa = np.concatenate([np.cos(ang), -np.sin(ang)], axis=0)
        self.stage_a = _bf16_table(np.kron(fa, eye))
        wgt = np.where((k1 == 0) | (k1 == self.n1 // 2), 1.0, 2.0) / self.n_fft
        fc = np.concatenate([np.cos(ang) * wgt[:, None], -np.sin(ang) * wgt[:, None]], axis=0).T
        self.stage_a_inv = _bf16_table(np.kron(fc, eye))
        t2 = np.arange(FFT_MINOR)
        k2 = np.arange(FFT_MINOR)
        idx = (t2[None, None, :] * (k1[:, None, None] + self.n1 * k2[None, :, None])) % self.n_fft
        ang_b = 2.0 * np.pi * idx / self.n_fft
        g = np.stack([np.cos(ang_b), -np.sin(ang_b)], axis=1)
        self.stage_b = _bf16_table(g)
        self.stage_b_inv = _bf16_table(np.transpose(g, (0, 1, 3, 2)))

    def scratch(self, signals):
        return pltpu.VMEM((signals, 2, self.k1, T2_BLOCKS, V7X_SUBLANES, V7X_LANES), F32)


def _time_tiles(ref, tb, rows):
    return ref[:, pl.ds(tb, 1)].reshape(rows, V7X_LANES)


def _stage_a_forward(plan, x_refs, fa_ref, a_s):
    def body(tb, carry):
        xs = jnp.concatenate([_time_tiles(xr, tb, plan.t1 * V7X_SUBLANES) for xr in x_refs], axis=1)
        av = jnp.dot(fa_ref[...], xs.astype(BF16), preferred_element_type=F32)
        for j in range(len(x_refs)):
            part = av[:, j * V7X_LANES:(j + 1) * V7X_LANES]
            a_s[j, :, :, pl.ds(tb, 1)] = part.reshape(2, plan.k1, 1, V7X_SUBLANES, V7X_LANES)
        return carry
    lax.fori_loop(0, T2_BLOCKS, body, 0, unroll=4)


def _stage_b_matrix(g_ref, i, inverse):
    gr, gi = g_ref[i, 0], g_ref[i, 1]
    if inverse:
        gi = -gi
    return jnp.concatenate([jnp.concatenate([gr, -gi], axis=1), jnp.concatenate([gi, gr], axis=1)], axis=0)


def _stage_b_slab(a_s, k, count):
    return jnp.concatenate(
        [a_s[j, :, pl.ds(k, 1)].reshape(2 * FFT_MINOR, V7X_LANES) for j in range(count)], axis=1)


def _spectrum_kernel(x0_ref, x1_ref, fa_ref, fb_ref, o_ref, a_s, *, plan):
    kb = pl.program_id(1)

    @pl.when(kb == 0)
    def _():
        _stage_a_forward(plan, (x0_ref, x1_ref), fa_ref, a_s)

    half = FFT_MINOR
    rows = 2 * half
    for i in range(plan.k_block):
        slab = _stage_b_slab(a_s, kb * plan.k_block + i, 2).astype(BF16)
        xs = jnp.dot(_stage_b_matrix(fb_ref, i, False), slab, preferred_element_type=F32)
        o_ref[i * rows:i * rows + half, :] = xs[:half, :V7X_LANES] + xs[:half, V7X_LANES:]
        o_ref[i * rows + half:(i + 1) * rows, :] = xs[half:, :V7X_LANES] - xs[half:, V7X_LANES:]


def _filter_spectrum(plan, filt):
    n, nf = filt.shape
    rows = 2 * FFT_MINOR
    kbs = plan.k_block
    n_cb = W_C // V7X_LANES
    filt = filt.reshape(plan.t1, T2_BLOCKS, V7X_SUBLANES, nf)
    sig = (plan.t1, T2_BLOCKS, V7X_SUBLANES, V7X_LANES)
    return pl.pallas_call(
        functools.partial(_spectrum_kernel, plan=plan),
        grid=(HY_ORDER * n_cb, plan.k1 // kbs),
        in_specs=[_resident(sig, lambda c, k: (0, 0, 0, 2 * (c // n_cb) * n_cb + c % n_cb)),
                  _resident(sig, lambda c, k: (0, 0, 0, (2 * (c // n_cb) + 1) * n_cb + c % n_cb)),
                  pl.BlockSpec(plan.stage_a.shape, lambda c, k: (0, 0)),
                  pl.BlockSpec((kbs, 2, FFT_MINOR, FFT_MINOR), lambda c, k: (k, 0, 0, 0))],
        out_specs=pl.BlockSpec((kbs * rows, V7X_LANES), lambda c, k: (k, c)),
        out_shape=jax.ShapeDtypeStruct((plan.k1 * rows, HY_ORDER * W_C), F32),
        scratch_shapes=[plan.scratch(2)],
        compiler_params=_params("arbitrary", "arbitrary"),
        name="filter_spectrum",
    )(filt, filt, plan.stage_a, plan.stage_b)


def _conv_kernel(v_ref, g_ref, h_ref, fa_ref, fb_ref, fbi_ref, fai_ref, sk_ref, o_ref, a_s, *, plan, bn):
    kb = pl.program_id(1)
    half = FFT_MINOR
    rows = 2 * half

    @pl.when(kb == 0)
    def _():
        _stage_a_forward(plan, [v_ref.at[b] for b in range(bn)], fa_ref, a_s)

    for i in range(plan.k_block):
        k = kb * plan.k_block + i
        xs = jnp.dot(_stage_b_matrix(fb_ref, i, False), _stage_b_slab(a_s, k, bn).astype(BF16),
                     preferred_element_type=F32)
        xr, xi = xs[:half], xs[half:]
        hr = jnp.concatenate([h_ref[i * rows:i * rows + half, :]] * bn, axis=1)
        hi = jnp.concatenate([h_ref[i * rows + half:(i + 1) * rows, :]] * bn, axis=1)
        ys = jnp.concatenate([xr * hr - xi * hi, xr * hi + xi * hr], axis=0)
        ds = jnp.dot(_stage_b_matrix(fbi_ref, i, True), ys.astype(BF16), preferred_element_type=F32)
        for b in range(bn):
            part = ds[:, b * V7X_LANES:(b + 1) * V7X_LANES]
            a_s[b, :, pl.ds(k, 1)] = part.reshape(2, 1, T2_BLOCKS, V7X_SUBLANES, V7X_LANES)

    @pl.when(kb == pl.num_programs(1) - 1)
    def _():
        t_rows = plan.t1 * V7X_SUBLANES

        def body(tb, carry):
            ds = jnp.concatenate(
                [a_s[b, :, :, pl.ds(tb, 1)].reshape(2 * plan.k1 * V7X_SUBLANES, V7X_LANES) for b in range(bn)],
                axis=1)
            y = jnp.dot(fai_ref[...], ds.astype(BF16), preferred_element_type=F32)
            for b in range(bn):
                vv = _time_tiles(v_ref.at[b], tb, t_rows)
                gg = _time_tiles(g_ref.at[b], tb, t_rows)
                res = gg * (y[:, b * V7X_LANES:(b + 1) * V7X_LANES] + sk_ref[...] * vv)
                o_ref[b, :, pl.ds(tb, 1)] = res.reshape(plan.t1, 1, V7X_SUBLANES, V7X_LANES)
            return carry
        lax.fori_loop(0, T2_BLOCKS, body, 0, unroll=4)


def _hyena_conv(plan, v_arr, v_col, g_arr, g_col, spec, order, skip):
    bn = v_arr.shape[0]
    cb = V7X_LANES
    rows = 2 * FFT_MINOR
    n_cb = W_C // cb
    kbs = plan.k_block
    sig = (bn, plan.t1, T2_BLOCKS, V7X_SUBLANES, cb)
    return pl.pallas_call(
        functools.partial(_conv_kernel, plan=plan, bn=bn),
        grid=(n_cb, plan.k1 // kbs),
        in_specs=[_resident(sig, lambda c, k: (0, 0, 0, 0, v_col + c)),
                  _resident(sig, lambda c, k: (0, 0, 0, 0, g_col + c)),
                  pl.BlockSpec((kbs * rows, cb), lambda c, k: (k, order * n_cb + c)),
                  pl.BlockSpec(plan.stage_a.shape, lambda c, k: (0, 0)),
                  pl.BlockSpec((kbs, 2, FFT_MINOR, FFT_MINOR), lambda c, k: (k, 0, 0, 0)),
                  pl.BlockSpec((kbs, 2, FFT_MINOR, FFT_MINOR), lambda c, k: (k, 0, 0, 0)),
                  pl.BlockSpec(plan.stage_a_inv.shape, lambda c, k: (0, 0)),
                  pl.BlockSpec((1, cb), lambda c, k: (0, c))],
        out_specs=_resident(sig, lambda c, k: (0, 0, 0, 0, c)),
        out_shape=jax.ShapeDtypeStruct((bn, plan.t1, T2_BLOCKS, V7X_SUBLANES, W_C), F32),
        scratch_shapes=[plan.scratch(bn)],
        compiler_params=_params("arbitrary", "arbitrary"),
        name="hyena_conv",
    )(v_arr, g_arr, spec, plan.stage_a, plan.stage_b, plan.stage_b_inv, plan.stage_a_inv,
      skip.reshape(1, W_C))


def _hyena(z, n_fft_len, filt_args, skip):
    bn, n, _ = z.shape
    plan = _FftPlan(n_fft_len)
    filt = _filters(n, *filt_args)
    if n_fft_len != n:
        filt = jnp.pad(filt, ((0, n_fft_len - n), (0, 0)))
        z = jnp.pad(z, ((0, 0), (0, n_fft_len - n), (0, 0)))
    spec = _filter_spectrum(plan, filt)
    n_cb = W_C // V7X_LANES
    z = z.reshape(bn, plan.t1, T2_BLOCKS, V7X_SUBLANES, 3 * W_C)
    y1 = _hyena_conv(plan, z, 0, z, n_cb, spec, 0, skip[0])
    y2 = _hyena_conv(plan, y1, 0, z, 2 * n_cb, spec, 1, skip[1])
    return y2.reshape(bn, n_fft_len, W_C)[:, :n]


def _dense_block_diag(w):
    h, c, d = w.shape[-3:]
    eye = jnp.eye(h, dtype=w.dtype)
    dense = jnp.einsum('...hcd,hk->...hckd', w, eye)
    return dense.reshape(w.shape[:-3] + (h * c, h * d))


def _tile_rows(n):
    return min(n, 512)


def kernel(x, c, ctx, c_ctx, w_mod, b_mod, norm_g, ffn_w_in, ffn_w_out, w_in, w_out, lru_conv_w, lru_conv_b,
           lru_gate_w, lru_gate_b, lru_lambda, pool_w, pool_b, pool_scale, hy_conv_w, hy_conv_b, hy_w1, hy_b1,
           hy_freq, hy_w2, hy_b2, hy_w3, hy_skip, final_g):
    bn, n, _ = x.shape
    n_ctx = ctx.shape[1]
    depth = w_mod.shape[0]
    assert bn + 1 <= V7X_SUBLANES and n % GRID_W == 0 and n % POOL_TILE == 0 and n_ctx == POOL_TILE
    tm_x, tm_c = _tile_rows(n), _tile_rows(n_ctx)

    cvec = jnp.zeros((V7X_SUBLANES, D_MODEL), F32).at[:bn].set(c).at[bn].set(c_ctx)
    mods = _modulation(cvec, w_mod, b_mod)
    band_x, inv_x = _pool_tables(GRID_W)
    band_c, inv_c = _pool_tables(n_ctx)
    zeros = jnp.zeros((bn, W_A), F32)
    f_in = ffn_w_in.astype(BF16)
    f_out = ffn_w_out.astype(BF16)

    xc = ctx
    for l in range(depth):
        last = l == depth - 1
        mx = mods[l, :bn].reshape(bn, N_MOD, D_MODEL)
        mc = jnp.broadcast_to(mods[l, bn].reshape(1, N_MOD, D_MODEL), (bn, N_MOD, D_MODEL))
        w_in_l = w_in[l].astype(BF16)
        w_out_l = w_out[l].astype(BF16)
        gate_dense = _dense_block_diag(lru_gate_w[l])
        wg = [jnp.concatenate([gate_dense[d, 0], gate_dense[d, 1]], axis=-1).astype(BF16) for d in range(2)]
        bg = [lru_gate_b[l, d].reshape(1, 2 * W_A) for d in range(2)]
        lam = [lru_lambda[l, d].reshape(1, W_A) for d in range(2)]
        pw = _dense_block_diag(pool_w[l]).astype(BF16)
        filt_args = (hy_w1[l], hy_b1[l], hy_freq[l], hy_w2[l], hy_b2[l], hy_w3[l])
        proj_args = (w_in_l, lru_conv_w[l], lru_conv_b[l], hy_conv_w[l], hy_conv_b[l])

        def mixer(xs, mod, tm, h0f, band, inv, fft_len):
            ua, ga, up, z, hf = _proj(xs, mod, norm_g[l, 1], *proj_args, wg[0], bg[0], lam[0], h0f, tm=tm)
            yc = _hyena(z, fft_len, filt_args, hy_skip[l])
            return lambda h0b: (ua, wg[1], bg[1], lam[1], h0b, hf, ga, up, yc, band, inv, pw, pool_b[l],
                                pool_scale[l], w_out_l), hf[:, -1]

        x = _ffn(x, mx, norm_g[l, 0], f_in, f_out, (l, 0), final_g, j0=0, final=False, tm=tm_x)
        xc = _ffn(xc, mc, norm_g[l, 0], f_in, f_out, (l, 0), final_g, j0=0, final=False, tm=tm_c)
        if last:
            ua_c, _, _, _, hf_c = _proj(xc, mc, norm_g[l, 1], *proj_args, wg[0], bg[0], lam[0], zeros, tm=tm_c)
            h0f = hf_c[:, -1]
            h0b = _lru_scan(ua_c, wg[1], bg[1], lam[1], zeros, tm=tm_c, reverse=True)[:, 0]
        else:
            mix_c, h0f = mixer(xc, mc, tm_c, zeros, band_c, inv_c, n_ctx)
            xc, h0b = _ffn(xc, mc, norm_g[l, 2], f_in, f_out, (l, 1), final_g, mix_c(zeros), j0=6, final=False,
                           tm=tm_c)
        mix_x, _ = mixer(x, mx, tm_x, h0f, band_x, inv_x, n)
        x, _ = _ffn(x, mx, norm_g[l, 2], f_in, f_out, (l, 1), final_g, mix_x(h0b), j0=6, final=last, tm=tm_x)
    return x
```

```python
import functools
import math

import numpy as np
import jax
import jax.numpy as jnp
from jax import lax
from jax.experimental import pallas as pl
from jax.experimental.pallas import tpu as pltpu

F32 = jnp.float32
BF16 = jnp.bfloat16

D_MODEL = 1024
GRID_W = 64
N_MOD = 9
W_A, W_B, W_C = 512, 256, 256
D_IN = 2 * W_A + W_B + 3 * W_C
RG_HEADS, RG_HD, RG_CONV, RG_C = 8, 64, 4, 8.0
POOL_WINDOWS = (2, 4, 8, 16)
POOL_GW = W_B // len(POOL_WINDOWS)
HY_ORDER, HY_SHORT, HY_BANDS, HY_FH = 2, 3, 16, 64
HY_EMB = 2 * HY_BANDS + 1
HY_EMB_PAD = 40
HY_TARGET, HY_FAST, HY_SLOW = 1e-2, 0.3, 1.5
D_FF = 2816
EPS = 1e-6

V7X_LANES = 128
V7X_SUBLANES = 8
V7X_MXU_DIM = 256
V7X_VMEM_LIMIT_BYTES = 56 * 1024 * 1024

FFT_MINOR = V7X_LANES
HALO = V7X_SUBLANES
FF_CHUNKS = ((0, 768), (768, 768), (1536, 768), (2304, 512))
POOL_TILE = 256


def _params(*sem):
    return pltpu.CompilerParams(dimension_semantics=sem, vmem_limit_bytes=V7X_VMEM_LIMIT_BYTES)


def _resident(shape, index_map):
    return pl.BlockSpec(shape, index_map, pipeline_mode=pl.Buffered(1))


def _bf16_table(values):
    return jnp.asarray(values, F32).astype(BF16)


def _split_bf16(x):
    hi = x.astype(BF16)
    return hi, (x - hi.astype(F32)).astype(BF16)


def _dot_bf16x3(a, b):
    a_hi, a_lo = _split_bf16(a)
    b_hi, b_lo = _split_bf16(b)
    d = lambda p, q: jnp.dot(p, q, preferred_element_type=F32)
    return d(a_hi, b_hi) + (d(a_lo, b_hi) + d(a_hi, b_lo))


def _adanorm(x, g, shift, scale):
    ms = jnp.mean(x * x, axis=-1, keepdims=True)
    return (x * lax.rsqrt(ms + EPS)) * g * (1.0 + scale) + shift


def _mod_kernel(c_ref, w_ref, b_ref, o_ref):
    s = c_ref[...]
    s = s * jax.nn.sigmoid(s)
    o_ref[0] = jnp.dot(s.astype(BF16), w_ref[0].astype(BF16), preferred_element_type=F32) + b_ref[0]


def _modulation(cvec, w_mod, b_mod):
    depth, _, n = w_mod.shape
    tn = n // 4
    return pl.pallas_call(
        _mod_kernel,
        grid=(depth, n // tn),
        in_specs=[pl.BlockSpec((V7X_SUBLANES, D_MODEL), lambda l, j: (0, 0)),
                  pl.BlockSpec((1, D_MODEL, tn), lambda l, j: (l, 0, j)),
                  pl.BlockSpec((1, 1, tn), lambda l, j: (l, 0, j))],
        out_specs=pl.BlockSpec((1, V7X_SUBLANES, tn), lambda l, j: (l, 0, j)),
        out_shape=jax.ShapeDtypeStruct((depth, V7X_SUBLANES, n), F32),
        compiler_params=_params("arbitrary", "arbitrary"),
        name="modulation",
    )(cvec, w_mod, b_mod.reshape(depth, 1, n))


def _pool_tile(x, band_ref, inv_ref, w_ref, b_ref, s_ref):
    xb = x.astype(BF16)
    col = lax.broadcasted_iota(jnp.int32, x.shape, 1) // POOL_GW
    tot = jnp.zeros_like(x)
    for g in range(len(POOL_WINDOWS)):
        tot = jnp.where(col == g, jnp.dot(band_ref[g], xb, preferred_element_type=F32), tot)
    pooled = tot * inv_ref[...] - x
    y = jnp.dot(pooled.astype(BF16), w_ref[...], preferred_element_type=F32)
    return (y + b_ref[...]) * s_ref[...]


def _ffn_kernel(x_ref, m_ref, g_ref, win_ref, wout_ref, fg_ref, *rest, j0, final, mix, tm, n_tiles):
    x = x_ref[0]
    if mix:
        (u0_ref, hf0_ref, ga0_ref, u_ref, hf_ref, ga_ref, wg_ref, bg_ref, lam_ref, h0_ref, up_ref, yc_ref, band_ref,
         inv_ref, pw_ref, pb_ref, ps_ref, wo_ref, o_ref, hend_ref, a_s, b_s, c_s, ya_s) = rest
        i = pl.program_id(1)

        parts = len(FF_CHUNKS)
        part_rows = tm // parts

        @pl.when(i == 0)
        def _():
            _lru_coeffs(u0_ref[0], wg_ref, bg_ref, lam_ref, a_s, b_s)
            c_s[...] = _lru_scan_tile(a_s, b_s, h0_ref[0], tm=tm, reverse=True)
            ya_s[0] = (hf0_ref[0] + b_s[...]) * jax.nn.gelu(ga0_ref[0])

        def next_tile_part(j, carry):
            rows = slice(tm - (j + 1) * part_rows, tm - j * part_rows)
            per = part_rows // V7X_SUBLANES
            _lru_coeffs(u_ref[0, rows, :], wg_ref, bg_ref, lam_ref, a_s, b_s, rows)
            carry = _lru_scan_tile(a_s, b_s, carry, tm=tm, reverse=True, inline=range(j * per, (j + 1) * per))
            ya_s[(i + 1) % 2, rows, :] = (hf_ref[0, rows, :] + b_s[rows, :]) * jax.nn.gelu(ga_ref[0, rows, :])
            return carry

        ya = ya_s[i % 2]
        yb = jnp.concatenate([_pool_tile(up_ref[0, r:r + POOL_TILE, :], band_ref, inv_ref, pw_ref, pb_ref, ps_ref)
                              for r in range(0, tm, POOL_TILE)], axis=0)
        y = jnp.dot(ya.astype(BF16), wo_ref[0:W_A, :], preferred_element_type=F32)
        y = y + jnp.dot(yb.astype(BF16), wo_ref[W_A:W_A + W_B, :], preferred_element_type=F32)
        y = y + jnp.dot(yc_ref[0].astype(BF16), wo_ref[W_A + W_B:, :], preferred_element_type=F32)
        x = x + m_ref[0, 5:6, :] * y
    else:
        (o_ref,) = rest
    shift = m_ref[0, j0:j0 + 1, :]
    scale = m_ref[0, j0 + 1:j0 + 2, :]
    gate = m_ref[0, j0 + 2:j0 + 3, :]
    h = _adanorm(x, g_ref[...], shift, scale).astype(BF16)
    acc = None
    ahead = mix and n_tiles > 1
    carry = c_s[...] if ahead else None
    for j, (s, w) in enumerate(FF_CHUNKS):
        if ahead:
            carry = next_tile_part(j, carry)
        gt = jnp.dot(h, win_ref[:, s:s + w], preferred_element_type=F32)
        up = jnp.dot(h, win_ref[:, D_FF + s:D_FF + s + w], preferred_element_type=F32)
        a = (gt * jax.nn.sigmoid(gt) * up).astype(BF16)
        p = jnp.dot(a, wout_ref[s:s + w, :], preferred_element_type=F32)
        acc = p if acc is None else acc + p
    if ahead:
        c_s[...] = jnp.where(i + 1 < n_tiles, carry, c_s[...])
    if mix:
        hend_ref[0] = c_s[...]
    y = x + (0.5 * gate) * acc
    if final:
        ms = jnp.mean(y * y, axis=-1, keepdims=True)
        y = (y * lax.rsqrt(ms + EPS)) * fg_ref[...]
    o_ref[0] = y


def _ffn(x, mod, g, w_in, w_out, wsel, final_g, mix=None, *, j0, final, tm):
    bn, n, _ = x.shape
    last = n // tm - 1
    tile_map = (lambda b, i: (b, last - i, 0)) if mix is not None else (lambda b, i: (b, i, 0))
    tile = lambda w: pl.BlockSpec((1, tm, w), tile_map)
    const = lambda shape: pl.BlockSpec(shape, lambda b, i: (0,) * len(shape))
    in_specs = [tile(D_MODEL),
                pl.BlockSpec((1, N_MOD, D_MODEL), lambda b, i: (b, 0, 0)),
                const((1, D_MODEL)),
                _resident((None, None, D_MODEL, 2 * D_FF), lambda b, i: wsel + (0, 0)),
                _resident((None, None, D_FF, D_MODEL), lambda b, i: wsel + (0, 0)),
                const((1, D_MODEL))]
    args = [x, mod, g.reshape(1, D_MODEL), w_in, w_out, final_g.reshape(1, D_MODEL)]
    x_shape = jax.ShapeDtypeStruct(x.shape, F32)
    if mix is None:
        return pl.pallas_call(
            functools.partial(_ffn_kernel, j0=j0, final=final, mix=False, tm=tm, n_tiles=n // tm),
            grid=(bn, n // tm), in_specs=in_specs, out_specs=tile(D_MODEL), out_shape=x_shape,
            compiler_params=_params("arbitrary", "arbitrary"), name="ffn",
        )(*args)
    u, wg, bg, lam, h0, hf, ga, up, yc, band, inv, pw, pb, ps, wo = mix
    first = _resident((1, tm, W_A), lambda b, i: (b, last, 0))
    nxt = pl.BlockSpec((1, tm, W_A), lambda b, i: (b, last - jnp.minimum(i + 1, last), 0))
    in_specs += ([first] * 3 + [nxt] * 3 + _lru_param_specs()
                 + [tile(W_B), tile(W_C), const(band.shape), const(inv.shape),
                    const((W_B, W_B)), const((1, W_B)), const((1, W_B)),
                    _resident((D_MODEL, D_MODEL), lambda b, i: (0, 0))])
    args += [u, hf, ga, u, hf, ga, wg, bg, lam, h0.reshape(bn, 1, W_A), up, yc, band, inv, pw, pb.reshape(1, W_B),
             ps.reshape(1, W_B), wo]
    out, hend = pl.pallas_call(
        functools.partial(_ffn_kernel, j0=j0, final=final, mix=True, tm=tm, n_tiles=n // tm),
        grid=(bn, n // tm),
        in_specs=in_specs,
        out_specs=[tile(D_MODEL), pl.BlockSpec((1, 1, W_A), lambda b, i: (b, 0, 0))],
        out_shape=[x_shape, jax.ShapeDtypeStruct((bn, 1, W_A), F32)],
        scratch_shapes=_lru_scratch(tm) + [pltpu.VMEM((2, tm, W_A), F32)],
        compiler_params=_params("arbitrary", "arbitrary"),
        name="mix_ffn",
    )(*args)
    return out, hend.reshape(bn, W_A)


def _proj_kernel(xp_ref, x_ref, xn_ref, m_ref, g_ref, w_ref, cwa_ref, cba_ref, cwh_ref, cbh_ref,
                 wg_ref, bg_ref, lam_ref, h0_ref, ua_ref, ga_ref, up_ref, z_ref, hf_ref,
                 pa_s, ph_s, a_s, b_s, c_s, uprev_s, *, tm, n_tiles):
    i = pl.program_id(1)
    ti = jnp.minimum(i, n_tiles - 1)
    pipelined = n_tiles > 1

    @pl.when(i == 0)
    def _():
        c_s[...] = h0_ref[0]
        if pipelined:
            uprev_s[...] = jnp.zeros_like(uprev_s)

    parts = 4
    part_rows = tm // parts

    def scan_part(j, carry):
        rows = slice(j * part_rows, (j + 1) * part_rows)
        per = part_rows // V7X_SUBLANES
        _lru_coeffs(uprev_s[rows, :], wg_ref, bg_ref, lam_ref, a_s, b_s, rows)
        carry = _lru_scan_tile(a_s, b_s, carry, tm=tm, reverse=False, inline=range(j * per, (j + 1) * per))
        hf_ref[0, rows, :] = b_s[rows, :]
        return carry

    norm = lambda v: _adanorm(v, g_ref[...], m_ref[0, 3:4, :], m_ref[0, 4:5, :])
    h_prev = jnp.where(ti > 0, norm(xp_ref[0]), 0.0)
    h_next = jnp.where(ti < n_tiles - 1, norm(xn_ref[0]), 0.0)
    h = jnp.concatenate([h_prev, norm(x_ref[0]), h_next], axis=0).astype(BF16)
    proj = lambda lo, hi: jnp.dot(h, w_ref[:, lo:hi], preferred_element_type=F32)

    def short_conv(lo, hi, w_conv_ref, b_conv_ref, taps, pad_left, p_s):
        p_s[...] = proj(lo, hi)
        acc = b_conv_ref[...] + w_conv_ref[0:1, :] * p_s[pl.ds(HALO - pad_left, tm), :]
        for k in range(1, taps):
            acc = acc + w_conv_ref[k:k + 1, :] * p_s[pl.ds(HALO - pad_left + k, tm), :]
        return acc

    carry = scan_part(0, c_s[...]) if pipelined else None
    z_ref[0] = short_conv(2 * W_A + W_B, D_IN, cwh_ref, cbh_ref, HY_SHORT, HY_SHORT // 2, ph_s)
    carry = scan_part(1, carry) if pipelined else None
    ua = short_conv(0, W_A, cwa_ref, cba_ref, RG_CONV, RG_CONV // 2, pa_s)
    ua_ref[0] = ua
    carry = scan_part(2, carry) if pipelined else None
    rest = proj(W_A, 2 * W_A + W_B)
    ga_ref[0] = rest[HALO:HALO + tm, :W_A]
    up_ref[0] = rest[HALO:HALO + tm, W_A:]
    if pipelined:
        carry = scan_part(3, carry)
        c_s[...] = jnp.where(i > 0, carry, c_s[...])
        uprev_s[...] = ua
    else:
        uprev_s[...] = ua
        carry = c_s[...]
        for j in range(parts):
            carry = scan_part(j, carry)


def _proj(x, mod, g, w_in, cwa, cba, cwh, cbh, wg, bg, lam, h0, *, tm):
    bn, n, _ = x.shape
    n_tiles = n // tm
    hb = tm // HALO
    last_hb = n // HALO - 1
    outs = (W_A, W_A, W_B, 3 * W_C)
    cur = lambda i: jnp.minimum(i, n_tiles - 1)
    const = lambda shape: pl.BlockSpec(shape, lambda b, i: (0,) * len(shape))
    return pl.pallas_call(
        functools.partial(_proj_kernel, tm=tm, n_tiles=n_tiles),
        grid=(bn, n_tiles + 1 if n_tiles > 1 else 1),
        in_specs=[pl.BlockSpec((1, HALO, D_MODEL), lambda b, i: (b, jnp.maximum(cur(i) * hb - 1, 0), 0)),
                  pl.BlockSpec((1, tm, D_MODEL), lambda b, i: (b, cur(i), 0)),
                  pl.BlockSpec((1, HALO, D_MODEL), lambda b, i: (b, jnp.minimum((cur(i) + 1) * hb, last_hb), 0)),
                  pl.BlockSpec((1, N_MOD, D_MODEL), lambda b, i: (b, 0, 0)),
                  const((1, D_MODEL)),
                  _resident((D_MODEL, D_IN), lambda b, i: (0, 0)),
                  const((RG_CONV, W_A)), const((1, W_A)), const((HY_SHORT, 3 * W_C)), const((1, 3 * W_C))]
                 + _lru_param_specs(),
        out_specs=[pl.BlockSpec((1, tm, w), lambda b, i: (b, cur(i), 0)) for w in outs]
                  + [pl.BlockSpec((1, tm, W_A), lambda b, i: (b, jnp.maximum(i - 1, 0), 0))],
        out_shape=[jax.ShapeDtypeStruct((bn, n, w), F32) for w in outs + (W_A,)],
        scratch_shapes=[pltpu.VMEM((tm + 2 * HALO, W_A), F32), pltpu.VMEM((tm + 2 * HALO, 3 * W_C), F32)]
                       + _lru_scratch(tm) + [pltpu.VMEM((tm, W_A), F32)],
        compiler_params=_params("arbitrary", "arbitrary"),
        name="proj_scan",
    )(x, x, x, mod, g.reshape(1, D_MODEL), w_in, cwa, cba.reshape(1, W_A), cwh, cbh.reshape(1, 3 * W_C),
      wg, bg, lam, h0.reshape(bn, 1, W_A))


def _lru_coeffs(u, wg_ref, bg_ref, lam_ref, a_s, b_s, rows=slice(None)):
    ub = u.astype(BF16)
    blk = V7X_MXU_DIM

    def gate_block(c):
        r0 = (c * blk) % W_A
        return jnp.dot(ub[:, r0:r0 + blk], wg_ref[r0:r0 + blk, c * blk:(c + 1) * blk], preferred_element_type=F32)

    half = jnp.concatenate([gate_block(c) for c in range(2 * W_A // blk)], axis=1) + bg_ref[...]
    q = (-0.25 * RG_C) * jax.nn.softplus(-lam_ref[...])
    t = jnp.tanh(q * jnp.tanh(half[:, :W_A]) + q)
    inv = 1.0 / (1.0 - t)
    a_s[rows, :] = (1.0 + t) * inv
    b_s[rows, :] = (jnp.sqrt(-t) * inv) * ((jnp.tanh(half[:, W_A:]) + 1.0) * u)


def _lru_scan_tile(a_s, b_s, carry, *, tm, reverse, inline=None):
    groups = tm // V7X_SUBLANES
    rowi = lax.broadcasted_iota(jnp.int32, (V7X_SUBLANES, W_A), 0)

    def body(k, c):
        gi = (groups - 1 - k) if reverse else k
        r0 = gi * V7X_SUBLANES if inline is not None else pl.multiple_of(gi * V7X_SUBLANES, V7X_SUBLANES)
        a = a_s[pl.ds(r0, V7X_SUBLANES), :]
        b = b_s[pl.ds(r0, V7X_SUBLANES), :]
        first = rowi == (V7X_SUBLANES - 1 if reverse else 0)
        b = jnp.where(first, a * c + b, b)
        for s in (1, 2, 4):
            if reverse:
                shift, m = V7X_SUBLANES - s, rowi < V7X_SUBLANES - s
            else:
                shift, m = s, rowi >= s
            b = jnp.where(m, a * pltpu.roll(b, shift, 0) + b, b)
            if s < V7X_SUBLANES // 2:
                a = jnp.where(m, a * pltpu.roll(a, shift, 0), a)
        b_s[pl.ds(r0, V7X_SUBLANES), :] = b
        return b[0:1, :] if reverse else b[V7X_SUBLANES - 1:V7X_SUBLANES, :]

    if inline is not None:
        for k in inline:
            carry = body(k, carry)
        return carry
    return lax.fori_loop(0, groups, body, carry, unroll=4)


def _lru_scan_kernel(u_ref, wg_ref, bg_ref, lam_ref, h0_ref, h_ref, a_s, b_s, c_s, *, tm, reverse):
    @pl.when(pl.program_id(1) == 0)
    def _():
        c_s[...] = h0_ref[0]

    _lru_coeffs(u_ref[0], wg_ref, bg_ref, lam_ref, a_s, b_s)
    c_s[...] = _lru_scan_tile(a_s, b_s, c_s[...], tm=tm, reverse=reverse)
    h_ref[0] = b_s[...]


def _lru_param_specs():
    const = lambda shape: pl.BlockSpec(shape, lambda b, i: (0,) * len(shape))
    return [const((W_A, 2 * W_A)), const((1, 2 * W_A)), const((1, W_A)),
            pl.BlockSpec((1, 1, W_A), lambda b, i: (b, 0, 0))]


def _lru_scratch(tm):
    return [pltpu.VMEM((tm, W_A), F32), pltpu.VMEM((tm, W_A), F32), pltpu.VMEM((1, W_A), F32)]


def _lru_scan(u, wg, bg, lam, h0, *, tm, reverse):
    bn, n, _ = u.shape
    last = n // tm - 1
    tile_map = (lambda b, i: (b, last - i, 0)) if reverse else (lambda b, i: (b, i, 0))
    return pl.pallas_call(
        functools.partial(_lru_scan_kernel, tm=tm, reverse=reverse),
        grid=(bn, n // tm),
        in_specs=[pl.BlockSpec((1, tm, W_A), tile_map)] + _lru_param_specs(),
        out_specs=pl.BlockSpec((1, tm, W_A), tile_map),
        out_shape=jax.ShapeDtypeStruct(u.shape, F32),
        scratch_shapes=_lru_scratch(tm),
        compiler_params=_params("arbitrary", "arbitrary"),
        name="lru_scan",
    )(u, wg, bg, lam, h0.reshape(bn, 1, W_A))


def _pool_tables(row_len):
    t = np.arange(POOL_TILE)
    p = t % row_len
    base = t - p
    band = np.zeros((len(POOL_WINDOWS), POOL_TILE, POOL_TILE), np.float32)
    inv = np.zeros((POOL_TILE, W_B), np.float32)
    for g, win in enumerate(POOL_WINDOWS):
        lo = np.clip(p - win // 2, 0, row_len)
        hi = np.clip(p + win - win // 2, 0, row_len)
        s = np.arange(POOL_TILE)[None, :]
        band[g] = ((s >= (base + lo)[:, None]) & (s < (base + hi)[:, None])).astype(np.float32)
        inv[:, g * POOL_GW:(g + 1) * POOL_GW] = (1.0 / (hi - lo).astype(np.float64))[:, None]
    return _bf16_table(band), jnp.asarray(inv, F32)


def _filter_kernel(f_ref, w1_ref, b1_ref, fr_ref, w2_ref, b2_ref, w3_ref, dl_ref, o_ref):
    feats = f_ref[...]
    fr = fr_ref[...]
    hid = jnp.sin(fr * (_dot_bf16x3(feats, w1_ref[...]) + b1_ref[...]))
    hid = jnp.sin(fr * (_dot_bf16x3(hid, w2_ref[...]) + b2_ref[...]))
    filt = _dot_bf16x3(hid, w3_ref[...])
    filt = filt * jnp.exp(-feats[:, 0:1] * dl_ref[...])
    o_ref[...] = filt

    @pl.when(pl.program_id(0) == 0)
    def _():
        head = filt[0:V7X_SUBLANES, :]
        row = lax.broadcasted_iota(jnp.int32, head.shape, 0)
        bwd_col = (lax.broadcasted_iota(jnp.int32, head.shape, 1) // W_C) % 2 == 1
        o_ref[0:V7X_SUBLANES, :] = jnp.where((row == 0) & bwd_col, 0.0, head)


def _filter_features(n):
    t = jnp.linspace(0.0, 1.0, n, dtype=F32)[:, None]
    wpos = 2.0 * math.pi * jnp.arange(n, dtype=F32)[:, None] / n
    bands = jnp.linspace(1e-4, HY_BANDS - 1, HY_BANDS, dtype=F32)[None, :]
    feats = jnp.concatenate([t, jnp.cos(bands * wpos), -jnp.sin(bands * wpos)], axis=-1)
    return jnp.pad(feats, ((0, 0), (0, HY_EMB_PAD - HY_EMB)))


def _filter_decay_rates():
    deltas = jnp.abs(jnp.linspace(math.log(HY_TARGET) / HY_FAST, math.log(HY_TARGET) / HY_SLOW, W_C, dtype=F32))
    return jnp.tile(deltas, 2 * HY_ORDER).reshape(1, 2 * HY_ORDER * W_C)


def _filters(n, w1, b1, freq, w2, b2, w3):
    tm = min(n, 512)
    nf = 2 * HY_ORDER * W_C
    const = lambda shape: pl.BlockSpec(shape, lambda i: (0, 0))
    return pl.pallas_call(
        _filter_kernel,
        grid=(n // tm,),
        in_specs=[pl.BlockSpec((tm, HY_EMB_PAD), lambda i: (i, 0)),
                  const((HY_EMB_PAD, HY_FH)), const((1, HY_FH)), const((1, HY_FH)),
                  const((HY_FH, HY_FH)), const((1, HY_FH)), const((HY_FH, nf)), const((1, nf))],
        out_specs=pl.BlockSpec((tm, nf), lambda i: (i, 0)),
        out_shape=jax.ShapeDtypeStruct((n, nf), F32),
        compiler_params=_params("arbitrary"),
        name="hyena_filter",
    )(_filter_features(n), jnp.pad(w1, ((0, HY_EMB_PAD - HY_EMB), (0, 0))), b1.reshape(1, HY_FH),
      freq.reshape(1, HY_FH), w2, b2.reshape(1, HY_FH), w3, _filter_decay_rates())


T2_BLOCKS = FFT_MINOR // V7X_SUBLANES


class _FftPlan:
    def __init__(self, n):
        assert n % (2 * FFT_MINOR) == 0
        self.n = n
        self.n_fft = 2 * n
        self.n1 = self.n_fft // FFT_MINOR
        self.t1 = self.n1 // 2
        self.k1 = self.n1 // 2 + 1
        self.k_block = max(d for d in range(1, 14) if self.k1 % d == 0)
        t1 = np.arange(self.t1)
        k1 = np.arange(self.k1)
        eye = np.eye(V7X_SUBLANES)
        ang = 2.0 * np.pi * ((k1[:, None] * t1[None, :]) % self.n1) / self.n1
        fa = np.concatenate([np.cos(ang), -np.sin(ang)], axis=0)
        self.stage_a = _bf16_table(np.kron(fa, eye))
        wgt = np.where((k1 == 0) | (k1 == self.n1 // 2), 1.0, 2.0) / self.n_fft
        fc = np.concatenate([np.cos(ang) * wgt[:, None], -np.sin(ang) * wgt[:, None]], axis=0).T
        self.stage_a_inv = _bf16_table(np.kron(fc, eye))
        t2 = np.arange(FFT_MINOR)
        k2 = np.arange(FFT_MINOR)
        idx = (t2[None, None, :] * (k1[:, None, None] + self.n1 * k2[None, :, None])) % self.n_fft
        ang_b = 2.0 * np.pi * idx / self.n_fft
        g = np.stack([np.cos(ang_b), -np.sin(ang_b)], axis=1)
        self.stage_b = _bf16_table(g)
        self.stage_b_inv = _bf16_table(np.transpose(g, (0, 1, 3, 2)))

    def scratch(self, signals):
        return pltpu.VMEM((signals, 2, self.k1, T2_BLOCKS, V7X_SUBLANES, V7X_LANES), F32)


def _time_tiles(ref, tb, rows):
    return ref[:, pl.ds(tb, 1)].reshape(rows, V7X_LANES)


def _stage_a_forward(plan, x_refs, fa_ref, a_s):
    def body(tb, carry):
        xs = jnp.concatenate([_time_tiles(xr, tb, plan.t1 * V7X_SUBLANES) for xr in x_refs], axis=1)
        av = jnp.dot(fa_ref[...], xs.astype(BF16), preferred_element_type=F32)
        for j in range(len(x_refs)):
            part = av[:, j * V7X_LANES:(j + 1) * V7X_LANES]
            a_s[j, :, :, pl.ds(tb, 1)] = part.reshape(2, plan.k1, 1, V7X_SUBLANES, V7X_LANES)
        return carry
    lax.fori_loop(0, T2_BLOCKS, body, 0, unroll=4)


def _stage_b_matrix(g_ref, i, inverse):
    gr, gi = g_ref[i, 0], g_ref[i, 1]
    if inverse:
        gi = -gi
    return jnp.concatenate([jnp.concatenate([gr, -gi], axis=1), jnp.concatenate([gi, gr], axis=1)], axis=0)


def _stage_b_slab(a_s, k, count):
    return jnp.concatenate(
        [a_s[j, :, pl.ds(k, 1)].reshape(2 * FFT_MINOR, V7X_LANES) for j in range(count)], axis=1)


def _spectrum_kernel(x0_ref, x1_ref, fa_ref, fb_ref, o_ref, a_s, *, plan):
    kb = pl.program_id(1)

    @pl.when(kb == 0)
    def _():
        _stage_a_forward(plan, (x0_ref, x1_ref), fa_ref, a_s)

    half = FFT_MINOR
    rows = 2 * half
    for i in range(plan.k_block):
        slab = _stage_b_slab(a_s, kb * plan.k_block + i, 2).astype(BF16)
        xs = jnp.dot(_stage_b_matrix(fb_ref, i, False), slab, preferred_element_type=F32)
        o_ref[i * rows:i * rows + half, :] = xs[:half, :V7X_LANES] + xs[:half, V7X_LANES:]
        o_ref[i * rows + half:(i + 1) * rows, :] = xs[half:, :V7X_LANES] - xs[half:, V7X_LANES:]


def _filter_spectrum(plan, filt):
    n, nf = filt.shape
    rows = 2 * FFT_MINOR
    kbs = plan.k_block
    n_cb = W_C // V7X_LANES
    filt = filt.reshape(plan.t1, T2_BLOCKS, V7X_SUBLANES, nf)
    sig = (plan.t1, T2_BLOCKS, V7X_SUBLANES, V7X_LANES)
    return pl.pallas_call(
        functools.partial(_spectrum_kernel, plan=plan),
        grid=(HY_ORDER * n_cb, plan.k1 // kbs),
        in_specs=[_resident(sig, lambda c, k: (0, 0, 0, 2 * (c // n_cb) * n_cb + c % n_cb)),
                  _resident(sig, lambda c, k: (0, 0, 0, (2 * (c // n_cb) + 1) * n_cb + c % n_cb)),
                  pl.BlockSpec(plan.stage_a.shape, lambda c, k: (0, 0)),
                  pl.BlockSpec((kbs, 2, FFT_MINOR, FFT_MINOR), lambda c, k: (k, 0, 0, 0))],
        out_specs=pl.BlockSpec((kbs * rows, V7X_LANES), lambda c, k: (k, c)),
        out_shape=jax.ShapeDtypeStruct((plan.k1 * rows, HY_ORDER * W_C), F32),
        scratch_shapes=[plan.scratch(2)],
        compiler_params=_params("arbitrary", "arbitrary"),
        name="filter_spectrum",
    )(filt, filt, plan.stage_a, plan.stage_b)


def _conv_kernel(v_ref, g_ref, h_ref, fa_ref, fb_ref, fbi_ref, fai_ref, sk_ref, o_ref, a_s, *, plan, bn):
    kb = pl.program_id(1)
    half = FFT_MINOR
    rows = 2 * half

    @pl.when(kb == 0)
    def _():
        _stage_a_forward(plan, [v_ref.at[b] for b in range(bn)], fa_ref, a_s)

    for i in range(plan.k_block):
        k = kb * plan.k_block + i
        xs = jnp.dot(_stage_b_matrix(fb_ref, i, False), _stage_b_slab(a_s, k, bn).astype(BF16),
                     preferred_element_type=F32)
        xr, xi = xs[:half], xs[half:]
        hr = jnp.concatenate([h_ref[i * rows:i * rows + half, :]] * bn, axis=1)
        hi = jnp.concatenate([h_ref[i * rows + half:(i + 1) * rows, :]] * bn, axis=1)
        ys = jnp.concatenate([xr * hr - xi * hi, xr * hi + xi * hr], axis=0)
        ds = jnp.dot(_stage_b_matrix(fbi_ref, i, True), ys.astype(BF16), preferred_element_type=F32)
        for b in range(bn):
            part = ds[:, b * V7X_LANES:(b + 1) * V7X_LANES]
            a_s[b, :, pl.ds(k, 1)] = part.reshape(2, 1, T2_BLOCKS, V7X_SUBLANES, V7X_LANES)

    @pl.when(kb == pl.num_programs(1) - 1)
    def _():
        t_rows = plan.t1 * V7X_SUBLANES

        def body(tb, carry):
            ds = jnp.concatenate(
                [a_s[b, :, :, pl.ds(tb, 1)].reshape(2 * plan.k1 * V7X_SUBLANES, V7X_LANES) for b in range(bn)],
                axis=1)
            y = jnp.dot(fai_ref[...], ds.astype(BF16), preferred_element_type=F32)
            for b in range(bn):
                vv = _time_tiles(v_ref.at[b], tb, t_rows)
                gg = _time_tiles(g_ref.at[b], tb, t_rows)
                res = gg * (y[:, b * V7X_LANES:(b + 1) * V7X_LANES] + sk_ref[...] * vv)
                o_ref[b, :, pl.ds(tb, 1)] = res.reshape(plan.t1, 1, V7X_SUBLANES, V7X_LANES)
            return carry
        lax.fori_loop(0, T2_BLOCKS, body, 0, unroll=4)


def _hyena_conv(plan, v_arr, v_col, g_arr, g_col, spec, order, skip):
    bn = v_arr.shape[0]
    cb = V7X_LANES
    rows = 2 * FFT_MINOR
    n_cb = W_C // cb
    kbs = plan.k_block
    sig = (bn, plan.t1, T2_BLOCKS, V7X_SUBLANES, cb)
    return pl.pallas_call(
        functools.partial(_conv_kernel, plan=plan, bn=bn),
        grid=(n_cb, plan.k1 // kbs),
        in_specs=[_resident(sig, lambda c, k: (0, 0, 0, 0, v_col + c)),
                  _resident(sig, lambda c, k: (0, 0, 0, 0, g_col + c)),
                  pl.BlockSpec((kbs * rows, cb), lambda c, k: (k, order * n_cb + c)),
                  pl.BlockSpec(plan.stage_a.shape, lambda c, k: (0, 0)),
                  pl.BlockSpec((kbs, 2, FFT_MINOR, FFT_MINOR), lambda c, k: (k, 0, 0, 0)),
                  pl.BlockSpec((kbs, 2, FFT_MINOR, FFT_MINOR), lambda c, k: (k, 0, 0, 0)),
                  pl.BlockSpec(plan.stage_a_inv.shape, lambda c, k: (0, 0)),
                  pl.BlockSpec((1, cb), lambda c, k: (0, c))],
        out_specs=_resident(sig, lambda c, k: (0, 0, 0, 0, c)),
        out_shape=jax.ShapeDtypeStruct((bn, plan.t1, T2_BLOCKS, V7X_SUBLANES, W_C), F32),
        scratch_shapes=[plan.scratch(bn)],
        compiler_params=_params("arbitrary", "arbitrary"),
        name="hyena_conv",
    )(v_arr, g_arr, spec, plan.stage_a, plan.stage_b, plan.stage_b_inv, plan.stage_a_inv,
      skip.reshape(1, W_C))


def _hyena(z, n_fft_len, filt_args, skip):
    bn, n, _ = z.shape
    plan = _FftPlan(n_fft_len)
    filt = _filters(n, *filt_args)
    if n_fft_len != n:
        filt = jnp.pad(filt, ((0, n_fft_len - n), (0, 0)))
        z = jnp.pad(z, ((0, 0), (0, n_fft_len - n), (0, 0)))
    spec = _filter_spectrum(plan, filt)
    n_cb = W_C // V7X_LANES
    z = z.reshape(bn, plan.t1, T2_BLOCKS, V7X_SUBLANES, 3 * W_C)
    y1 = _hyena_conv(plan, z, 0, z, n_cb, spec, 0, skip[0])
    y2 = _hyena_conv(plan, y1, 0, z, 2 * n_cb, spec, 1, skip[1])
    return y2.reshape(bn, n_fft_len, W_C)[:, :n]


def _dense_block_diag(w):
    h, c, d = w.shape[-3:]
    eye = jnp.eye(h, dtype=w.dtype)
    dense = jnp.einsum('...hcd,hk->...hckd', w, eye)
    return dense.reshape(w.shape[:-3] + (h * c, h * d))


def _tile_rows(n):
    return min(n, 512)


def kernel(x, c, ctx, c_ctx, w_mod, b_mod, norm_g, ffn_w_in, ffn_w_out, w_in, w_out, lru_conv_w, lru_conv_b,
           lru_gate_w, lru_gate_b, lru_lambda, pool_w, pool_b, pool_scale, hy_conv_w, hy_conv_b, hy_w1, hy_b1,
           hy_freq, hy_w2, hy_b2, hy_w3, hy_skip, final_g):
    bn, n, _ = x.shape
    n_ctx = ctx.shape[1]
    depth = w_mod.shape[0]
    assert bn + 1 <= V7X_SUBLANES and n % GRID_W == 0 and n % POOL_TILE == 0 and n_ctx == POOL_TILE
    tm_x, tm_c = _tile_rows(n), _tile_rows(n_ctx)

    cvec = jnp.zeros((V7X_SUBLANES, D_MODEL), F32).at[:bn].set(c).at[bn].set(c_ctx)
    mods = _modulation(cvec, w_mod, b_mod)
    band_x, inv_x = _pool_tables(GRID_W)
    band_c, inv_c = _pool_tables(n_ctx)
    zeros = jnp.zeros((bn, W_A), F32)
    f_in = ffn_w_in.astype(BF16)
    f_out = ffn_w_out.astype(BF16)

    xc = ctx
    for l in range(depth):
        last = l == depth - 1
        mx = mods[l, :bn].reshape(bn, N_MOD, D_MODEL)
        mc = jnp.broadcast_to(mods[l, bn].reshape(1, N_MOD, D_MODEL), (bn, N_MOD, D_MODEL))
        w_in_l = w_in[l].astype(BF16)
        w_out_l = w_out[l].astype(BF16)
        gate_dense = _dense_block_diag(lru_gate_w[l])
        wg = [(0.5 * jnp.concatenate([gate_dense[d, 0], gate_dense[d, 1]], axis=-1)).astype(BF16) for d in range(2)]
        bg = [0.5 * lru_gate_b[l, d].reshape(1, 2 * W_A) for d in range(2)]
        lam = [lru_lambda[l, d].reshape(1, W_A) for d in range(2)]
        pw = _dense_block_diag(pool_w[l]).astype(BF16)
        filt_args = (hy_w1[l], hy_b1[l], hy_freq[l], hy_w2[l], hy_b2[l], hy_w3[l])
        proj_args = (w_in_l, lru_conv_w[l], lru_conv_b[l], hy_conv_w[l], hy_conv_b[l])

        def mixer(xs, mod, tm, h0f, band, inv, fft_len):
            ua, ga, up, z, hf = _proj(xs, mod, norm_g[l, 1], *proj_args, wg[0], bg[0], lam[0], h0f, tm=tm)
            yc = _hyena(z, fft_len, filt_args, hy_skip[l])
            return lambda h0b: (ua, wg[1], bg[1], lam[1], h0b, hf, ga, up, yc, band, inv, pw, pool_b[l],
                                pool_scale[l], w_out_l), hf[:, -1]

        x = _ffn(x, mx, norm_g[l, 0], f_in, f_out, (l, 0), final_g, j0=0, final=False, tm=tm_x)
        xc = _ffn(xc, mc, norm_g[l, 0], f_in, f_out, (l, 0), final_g, j0=0, final=False, tm=tm_c)
        if last:
            ua_c, _, _, _, hf_c = _proj(xc, mc, norm_g[l, 1], *proj_args, wg[0], bg[0], lam[0], zeros, tm=tm_c)
            h0f = hf_c[:, -1]
            h0b = _lru_scan(ua_c, wg[1], bg[1], lam[1], zeros, tm=tm_c, reverse=True)[:, 0]
        else:
            mix_c, h0f = mixer(xc, mc, tm_c, zeros, band_c, inv_c, n_ctx)
            xc, h0b = _ffn(xc, mc, norm_g[l, 2], f_in, f_out, (l, 1), final_g, mix_c(zeros), j0=6, final=False,
                           tm=tm_c)
        mix_x, _ = mixer(x, mx, tm_x, h0f, band_x, inv_x, n)
        x, _ = _ffn(x, mx, norm_g[l, 2], f_in, f_out, (l, 1), final_g, mix_x(h0b), j0=6, final=last, tm=tm_x)
    return x
```

```python
import functools
import math

import numpy as np
import jax
import jax.numpy as jnp
from jax import lax
from jax.experimental import pallas as pl
from jax.experimental.pallas import tpu as pltpu

F32 = jnp.float32
BF16 = jnp.bfloat16

D_MODEL = 1024
GRID_W = 64
N_MOD = 9
W_A, W_B, W_C = 512, 256, 256
D_IN = 2 * W_A + W_B + 3 * W_C
RG_HEADS, RG_HD, RG_CONV, RG_C = 8, 64, 4, 8.0
POOL_WINDOWS = (2, 4, 8, 16)
POOL_GW = W_B // len(POOL_WINDOWS)
HY_ORDER, HY_SHORT, HY_BANDS, HY_FH = 2, 3, 16, 64
HY_EMB = 2 * HY_BANDS + 1
HY_EMB_PAD = 40
HY_TARGET, HY_FAST, HY_SLOW = 1e-2, 0.3, 1.5
D_FF = 2816
EPS = 1e-6

V7X_LANES = 128
V7X_SUBLANES = 8
V7X_MXU_DIM = 256
V7X_VMEM_LIMIT_BYTES = 56 * 1024 * 1024

FFT_MINOR = V7X_LANES
HALO = V7X_SUBLANES
FF_CHUNKS = ((0, 768), (768, 768), (1536, 768), (2304, 512))
POOL_TILE = 256
SCAN_PARTS = 4
STAGE_B_MAX_SLABS = 13


def _params(*sem):
    return pltpu.CompilerParams(dimension_semantics=sem, vmem_limit_bytes=V7X_VMEM_LIMIT_BYTES)


def _resident(shape, index_map):
    return pl.BlockSpec(shape, index_map, pipeline_mode=pl.Buffered(1))


def _bf16_table(values):
    return jnp.asarray(values, F32).astype(BF16)


def _split_bf16(x):
    hi = x.astype(BF16)
    return hi, (x - hi.astype(F32)).astype(BF16)


def _dot_bf16x3(a, b):
    a_hi, a_lo = _split_bf16(a)
    b_hi, b_lo = _split_bf16(b)
    d = lambda p, q: jnp.dot(p, q, preferred_element_type=F32)
    return d(a_hi, b_hi) + (d(a_lo, b_hi) + d(a_hi, b_lo))


def _adanorm(x, g, shift, scale):
    ms = jnp.mean(x * x, axis=-1, keepdims=True)
    return (x * lax.rsqrt(ms + EPS)) * g * (1.0 + scale) + shift


def _mod_kernel(c_ref, w_ref, b_ref, o_ref):
    s = c_ref[...]
    s = s * jax.nn.sigmoid(s)
    o_ref[0] = jnp.dot(s.astype(BF16), w_ref[0].astype(BF16), preferred_element_type=F32) + b_ref[0]


def _modulation(cvec, w_mod, b_mod):
    depth, _, n = w_mod.shape
    tn = n // 4
    return pl.pallas_call(
        _mod_kernel,
        grid=(depth, n // tn),
        in_specs=[pl.BlockSpec((V7X_SUBLANES, D_MODEL), lambda l, j: (0, 0)),
                  pl.BlockSpec((1, D_MODEL, tn), lambda l, j: (l, 0, j)),
                  pl.BlockSpec((1, 1, tn), lambda l, j: (l, 0, j))],
        out_specs=pl.BlockSpec((1, V7X_SUBLANES, tn), lambda l, j: (l, 0, j)),
        out_shape=jax.ShapeDtypeStruct((depth, V7X_SUBLANES, n), F32),
        compiler_params=_params("arbitrary", "arbitrary"),
        name="modulation",
    )(cvec, w_mod, b_mod.reshape(depth, 1, n))


def _pool_tile(x, band_ref, inv_ref, w_ref, b_ref, s_ref):
    xb = x.astype(BF16)
    col = lax.broadcasted_iota(jnp.int32, x.shape, 1) // POOL_GW
    tot = jnp.zeros_like(x)
    for g in range(len(POOL_WINDOWS)):
        tot = jnp.where(col == g, jnp.dot(band_ref[g], xb, preferred_element_type=F32), tot)
    pooled = tot * inv_ref[...] - x
    y = jnp.dot(pooled.astype(BF16), w_ref[...], preferred_element_type=F32)
    return (y + b_ref[...]) * s_ref[...]


def _ffn_kernel(x_ref, m_ref, g_ref, win_ref, wout_ref, fg_ref, *rest, j0, final, mix, tm, n_tiles):
    x = x_ref[0]
    if mix:
        (u0_ref, hf0_ref, ga0_ref, u_ref, hf_ref, ga_ref, wg_ref, bg_ref, lam_ref, h0_ref, up_ref, yc_ref, band_ref,
         inv_ref, pw_ref, pb_ref, ps_ref, wo_ref, o_ref, hend_ref, a_s, b_s, c_s, ya_s) = rest
        i = pl.program_id(1)

        parts = len(FF_CHUNKS)
        part_rows = tm // parts

        @pl.when(i == 0)
        def _():
            _lru_coeffs(u0_ref[0], wg_ref, bg_ref, lam_ref, a_s, b_s)
            c_s[...] = _lru_scan_tile(a_s, b_s, h0_ref[0], tm=tm, reverse=True)
            ya_s[0] = (hf0_ref[0] + b_s[...]) * jax.nn.gelu(ga0_ref[0])

        def next_tile_part(j, carry):
            rows = slice(tm - (j + 1) * part_rows, tm - j * part_rows)
            per = part_rows // V7X_SUBLANES
            _lru_coeffs(u_ref[0, rows, :], wg_ref, bg_ref, lam_ref, a_s, b_s, rows)
            carry = _lru_scan_tile(a_s, b_s, carry, tm=tm, reverse=True, inline=range(j * per, (j + 1) * per))
            ya_s[(i + 1) % 2, rows, :] = (hf_ref[0, rows, :] + b_s[rows, :]) * jax.nn.gelu(ga_ref[0, rows, :])
            return carry

        ya = ya_s[i % 2]
        yb = jnp.concatenate([_pool_tile(up_ref[0, r:r + POOL_TILE, :], band_ref, inv_ref, pw_ref, pb_ref, ps_ref)
                              for r in range(0, tm, POOL_TILE)], axis=0)
        y = jnp.dot(ya.astype(BF16), wo_ref[0:W_A, :], preferred_element_type=F32)
        y = y + jnp.dot(yb.astype(BF16), wo_ref[W_A:W_A + W_B, :], preferred_element_type=F32)
        y = y + jnp.dot(yc_ref[0].astype(BF16), wo_ref[W_A + W_B:, :], preferred_element_type=F32)
        x = x + m_ref[0, 5:6, :] * y
    else:
        (o_ref,) = rest
    shift = m_ref[0, j0:j0 + 1, :]
    scale = m_ref[0, j0 + 1:j0 + 2, :]
    gate = m_ref[0, j0 + 2:j0 + 3, :]
    h = _adanorm(x, g_ref[...], shift, scale).astype(BF16)
    acc = None
    ahead = mix and n_tiles > 1
    carry = c_s[...] if ahead else None
    for j, (s, w) in enumerate(FF_CHUNKS):
        if ahead:
            carry = next_tile_part(j, carry)
        gt = jnp.dot(h, win_ref[:, s:s + w], preferred_element_type=F32)
        up = jnp.dot(h, win_ref[:, D_FF + s:D_FF + s + w], preferred_element_type=F32)
        a = (gt * jax.nn.sigmoid(gt) * up).astype(BF16)
        p = jnp.dot(a, wout_ref[s:s + w, :], preferred_element_type=F32)
        acc = p if acc is None else acc + p
    if ahead:
        c_s[...] = jnp.where(i + 1 < n_tiles, carry, c_s[...])
    if mix:
        hend_ref[0] = c_s[...]
    y = x + (0.5 * gate) * acc
    if final:
        ms = jnp.mean(y * y, axis=-1, keepdims=True)
        y = (y * lax.rsqrt(ms + EPS)) * fg_ref[...]
    o_ref[0] = y


def _ffn(x, mod, g, w_in, w_out, wsel, final_g, mix=None, *, j0, final, tm):
    bn, n, _ = x.shape
    last = n // tm - 1
    tile_map = (lambda b, i: (b, last - i, 0)) if mix is not None else (lambda b, i: (b, i, 0))
    tile = lambda w: pl.BlockSpec((1, tm, w), tile_map)
    const = lambda shape: pl.BlockSpec(shape, lambda b, i: (0,) * len(shape))
    in_specs = [tile(D_MODEL),
                pl.BlockSpec((1, N_MOD, D_MODEL), lambda b, i: (b, 0, 0)),
                const((1, D_MODEL)),
                _resident((None, None, D_MODEL, 2 * D_FF), lambda b, i: wsel + (0, 0)),
                _resident((None, None, D_FF, D_MODEL), lambda b, i: wsel + (0, 0)),
                const((1, D_MODEL))]
    args = [x, mod, g.reshape(1, D_MODEL), w_in, w_out, final_g.reshape(1, D_MODEL)]
    x_shape = jax.ShapeDtypeStruct(x.shape, F32)
    if mix is None:
        return pl.pallas_call(
            functools.partial(_ffn_kernel, j0=j0, final=final, mix=False, tm=tm, n_tiles=n // tm),
            grid=(bn, n // tm), in_specs=in_specs, out_specs=tile(D_MODEL), out_shape=x_shape,
            compiler_params=_params("arbitrary", "arbitrary"), name="ffn",
        )(*args)
    u, wg, bg, lam, h0, hf, ga, up, yc, band, inv, pw, pb, ps, wo = mix
    first = _resident((1, tm, W_A), lambda b, i: (b, last, 0))
    nxt = pl.BlockSpec((1, tm, W_A), lambda b, i: (b, last - jnp.minimum(i + 1, last), 0))
    in_specs += ([first] * 3 + [nxt] * 3 + _lru_param_specs()
                 + [tile(W_B), tile(W_C), const(band.shape), const(inv.shape),
                    const((W_B, W_B)), const((1, W_B)), const((1, W_B)),
                    _resident((D_MODEL, D_MODEL), lambda b, i: (0, 0))])
    args += [u, hf, ga, u, hf, ga, wg, bg, lam, h0.reshape(bn, 1, W_A), up, yc, band, inv, pw, pb.reshape(1, W_B),
             ps.reshape(1, W_B), wo]
    out, hend = pl.pallas_call(
        functools.partial(_ffn_kernel, j0=j0, final=final, mix=True, tm=tm, n_tiles=n // tm),
        grid=(bn, n // tm),
        in_specs=in_specs,
        out_specs=[tile(D_MODEL), pl.BlockSpec((1, 1, W_A), lambda b, i: (b, 0, 0))],
        out_shape=[x_shape, jax.ShapeDtypeStruct((bn, 1, W_A), F32)],
        scratch_shapes=_lru_scratch(tm) + [pltpu.VMEM((2, tm, W_A), F32)],
        compiler_params=_params("arbitrary", "arbitrary"),
        name="mix_ffn",
    )(*args)
    return out, hend.reshape(bn, W_A)


def _proj_kernel(xp_ref, x_ref, xn_ref, m_ref, g_ref, w_ref, cwa_ref, cba_ref, cwh_ref, cbh_ref,
                 wg_ref, bg_ref, lam_ref, h0_ref, ua_ref, ga_ref, up_ref, z_ref, hf_ref,
                 pa_s, ph_s, a_s, b_s, c_s, uprev_s, *, tm, n_tiles):
    i = pl.program_id(1)
    ti = jnp.minimum(i, n_tiles - 1)
    pipelined = n_tiles > 1

    @pl.when(i == 0)
    def _():
        c_s[...] = h0_ref[0]
        if pipelined:
            uprev_s[...] = jnp.zeros_like(uprev_s)

    parts = SCAN_PARTS
    part_rows = tm // parts

    def scan_part(j, carry):
        rows = slice(j * part_rows, (j + 1) * part_rows)
        per = part_rows // V7X_SUBLANES
        _lru_coeffs(uprev_s[rows, :], wg_ref, bg_ref, lam_ref, a_s, b_s, rows)
        carry = _lru_scan_tile(a_s, b_s, carry, tm=tm, reverse=False, inline=range(j * per, (j + 1) * per))
        hf_ref[0, rows, :] = b_s[rows, :]
        return carry

    norm = lambda v: _adanorm(v, g_ref[...], m_ref[0, 3:4, :], m_ref[0, 4:5, :])
    h_prev = jnp.where(ti > 0, norm(xp_ref[0]), 0.0)
    h_next = jnp.where(ti < n_tiles - 1, norm(xn_ref[0]), 0.0)
    h = jnp.concatenate([h_prev, norm(x_ref[0]), h_next], axis=0).astype(BF16)
    proj = lambda lo, hi: jnp.dot(h, w_ref[:, lo:hi], preferred_element_type=F32)

    def short_conv(lo, hi, w_conv_ref, b_conv_ref, taps, pad_left, p_s):
        p_s[...] = proj(lo, hi)
        acc = b_conv_ref[...] + w_conv_ref[0:1, :] * p_s[pl.ds(HALO - pad_left, tm), :]
        for k in range(1, taps):
            acc = acc + w_conv_ref[k:k + 1, :] * p_s[pl.ds(HALO - pad_left + k, tm), :]
        return acc

    carry = scan_part(0, c_s[...]) if pipelined else None
    z_ref[0] = short_conv(2 * W_A + W_B, D_IN, cwh_ref, cbh_ref, HY_SHORT, HY_SHORT // 2, ph_s)
    carry = scan_part(1, carry) if pipelined else None
    ua = short_conv(0, W_A, cwa_ref, cba_ref, RG_CONV, RG_CONV // 2, pa_s)
    ua_ref[0] = ua
    carry = scan_part(2, carry) if pipelined else None
    rest = proj(W_A, 2 * W_A + W_B)
    ga_ref[0] = rest[HALO:HALO + tm, :W_A]
    up_ref[0] = rest[HALO:HALO + tm, W_A:]
    if pipelined:
        carry = scan_part(3, carry)
        c_s[...] = jnp.where(i > 0, carry, c_s[...])
        uprev_s[...] = ua
    else:
        uprev_s[...] = ua
        carry = c_s[...]
        for j in range(parts):
            carry = scan_part(j, carry)


def _proj(x, mod, g, w_in, cwa, cba, cwh, cbh, wg, bg, lam, h0, *, tm):
    bn, n, _ = x.shape
    n_tiles = n // tm
    hb = tm // HALO
    last_hb = n // HALO - 1
    outs = (W_A, W_A, W_B, 3 * W_C)
    cur = lambda i: jnp.minimum(i, n_tiles - 1)
    const = lambda shape: pl.BlockSpec(shape, lambda b, i: (0,) * len(shape))
    return pl.pallas_call(
        functools.partial(_proj_kernel, tm=tm, n_tiles=n_tiles),
        grid=(bn, n_tiles + 1 if n_tiles > 1 else 1),
        in_specs=[pl.BlockSpec((1, HALO, D_MODEL), lambda b, i: (b, jnp.maximum(cur(i) * hb - 1, 0), 0)),
                  pl.BlockSpec((1, tm, D_MODEL), lambda b, i: (b, cur(i), 0)),
                  pl.BlockSpec((1, HALO, D_MODEL), lambda b, i: (b, jnp.minimum((cur(i) + 1) * hb, last_hb), 0)),
                  pl.BlockSpec((1, N_MOD, D_MODEL), lambda b, i: (b, 0, 0)),
                  const((1, D_MODEL)),
                  _resident((D_MODEL, D_IN), lambda b, i: (0, 0)),
                  const((RG_CONV, W_A)), const((1, W_A)), const((HY_SHORT, 3 * W_C)), const((1, 3 * W_C))]
                 + _lru_param_specs(),
        out_specs=[pl.BlockSpec((1, tm, w), lambda b, i: (b, cur(i), 0)) for w in outs]
                  + [pl.BlockSpec((1, tm, W_A), lambda b, i: (b, jnp.maximum(i - 1, 0), 0))],
        out_shape=[jax.ShapeDtypeStruct((bn, n, w), F32) for w in outs + (W_A,)],
        scratch_shapes=[pltpu.VMEM((tm + 2 * HALO, W_A), F32), pltpu.VMEM((tm + 2 * HALO, 3 * W_C), F32)]
                       + _lru_scratch(tm) + [pltpu.VMEM((tm, W_A), F32)],
        compiler_params=_params("arbitrary", "arbitrary"),
        name="proj_scan",
    )(x, x, x, mod, g.reshape(1, D_MODEL), w_in, cwa, cba.reshape(1, W_A), cwh, cbh.reshape(1, 3 * W_C),
      wg, bg, lam, h0.reshape(bn, 1, W_A))


def _lru_coeffs(u, wg_ref, bg_ref, lam_ref, a_s, b_s, rows=slice(None)):
    ub = u.astype(BF16)
    blk = V7X_MXU_DIM

    def gate_block(c):
        r0 = (c * blk) % W_A
        return jnp.dot(ub[:, r0:r0 + blk], wg_ref[r0:r0 + blk, c * blk:(c + 1) * blk], preferred_element_type=F32)

    half = jnp.concatenate([gate_block(c) for c in range(2 * W_A // blk)], axis=1) + bg_ref[...]
    q = (-0.25 * RG_C) * jax.nn.softplus(-lam_ref[...])
    t = jnp.tanh(q * jnp.tanh(half[:, :W_A]) + q)
    inv = 1.0 / (1.0 - t)
    a_s[rows, :] = (1.0 + t) * inv
    b_s[rows, :] = (jnp.sqrt(-t) * inv) * ((jnp.tanh(half[:, W_A:]) + 1.0) * u)


def _lru_scan_tile(a_s, b_s, carry, *, tm, reverse, inline=None):
    groups = tm // V7X_SUBLANES
    rowi = lax.broadcasted_iota(jnp.int32, (V7X_SUBLANES, W_A), 0)

    def body(k, c):
        gi = (groups - 1 - k) if reverse else k
        r0 = gi * V7X_SUBLANES if inline is not None else pl.multiple_of(gi * V7X_SUBLANES, V7X_SUBLANES)
        a = a_s[pl.ds(r0, V7X_SUBLANES), :]
        b = b_s[pl.ds(r0, V7X_SUBLANES), :]
        first = rowi == (V7X_SUBLANES - 1 if reverse else 0)
        b = jnp.where(first, a * c + b, b)
        for s in (1, 2, 4):
            if reverse:
                shift, m = V7X_SUBLANES - s, rowi < V7X_SUBLANES - s
            else:
                shift, m = s, rowi >= s
            b = jnp.where(m, a * pltpu.roll(b, shift, 0) + b, b)
            if s < V7X_SUBLANES // 2:
                a = jnp.where(m, a * pltpu.roll(a, shift, 0), a)
        b_s[pl.ds(r0, V7X_SUBLANES), :] = b
        return b[0:1, :] if reverse else b[V7X_SUBLANES - 1:V7X_SUBLANES, :]

    if inline is not None:
        for k in inline:
            carry = body(k, carry)
        return carry
    return lax.fori_loop(0, groups, body, carry, unroll=4)


def _lru_scan_kernel(u_ref, wg_ref, bg_ref, lam_ref, h0_ref, h_ref, a_s, b_s, c_s, *, tm, reverse):
    @pl.when(pl.program_id(1) == 0)
    def _():
        c_s[...] = h0_ref[0]

    _lru_coeffs(u_ref[0], wg_ref, bg_ref, lam_ref, a_s, b_s)
    c_s[...] = _lru_scan_tile(a_s, b_s, c_s[...], tm=tm, reverse=reverse)
    h_ref[0] = b_s[...]


def _lru_param_specs():
    const = lambda shape: pl.BlockSpec(shape, lambda b, i: (0,) * len(shape))
    return [const((W_A, 2 * W_A)), const((1, 2 * W_A)), const((1, W_A)),
            pl.BlockSpec((1, 1, W_A), lambda b, i: (b, 0, 0))]


def _lru_scratch(tm):
    return [pltpu.VMEM((tm, W_A), F32), pltpu.VMEM((tm, W_A), F32), pltpu.VMEM((1, W_A), F32)]


def _lru_scan(u, wg, bg, lam, h0, *, tm, reverse):
    bn, n, _ = u.shape
    last = n // tm - 1
    tile_map = (lambda b, i: (b, last - i, 0)) if reverse else (lambda b, i: (b, i, 0))
    return pl.pallas_call(
        functools.partial(_lru_scan_kernel, tm=tm, reverse=reverse),
        grid=(bn, n // tm),
        in_specs=[pl.BlockSpec((1, tm, W_A), tile_map)] + _lru_param_specs(),
        out_specs=pl.BlockSpec((1, tm, W_A), tile_map),
        out_shape=jax.ShapeDtypeStruct(u.shape, F32),
        scratch_shapes=_lru_scratch(tm),
        compiler_params=_params("arbitrary", "arbitrary"),
        name="lru_scan",
    )(u, wg, bg, lam, h0.reshape(bn, 1, W_A))


def _pool_tables(row_len):
    t = np.arange(POOL_TILE)
    p = t % row_len
    base = t - p
    band = np.zeros((len(POOL_WINDOWS), POOL_TILE, POOL_TILE), np.float32)
    inv = np.zeros((POOL_TILE, W_B), np.float32)
    for g, win in enumerate(POOL_WINDOWS):
        lo = np.clip(p - win // 2, 0, row_len)
        hi = np.clip(p + win - win // 2, 0, row_len)
        s = np.arange(POOL_TILE)[None, :]
        band[g] = ((s >= (base + lo)[:, None]) & (s < (base + hi)[:, None])).astype(np.float32)
        inv[:, g * POOL_GW:(g + 1) * POOL_GW] = (1.0 / (hi - lo).astype(np.float64))[:, None]
    return _bf16_table(band), jnp.asarray(inv, F32)


def _filter_kernel(f_ref, w1_ref, b1_ref, fr_ref, w2_ref, b2_ref, w3_ref, dl_ref, o_ref):
    feats = f_ref[...]
    fr = fr_ref[...]
    hid = jnp.sin(fr * (_dot_bf16x3(feats, w1_ref[...]) + b1_ref[...]))
    hid = jnp.sin(fr * (_dot_bf16x3(hid, w2_ref[...]) + b2_ref[...]))
    filt = _dot_bf16x3(hid, w3_ref[...])
    filt = filt * jnp.exp(-feats[:, 0:1] * dl_ref[...])
    o_ref[...] = filt

    @pl.when(pl.program_id(0) == 0)
    def _():
        head = filt[0:V7X_SUBLANES, :]
        row = lax.broadcasted_iota(jnp.int32, head.shape, 0)
        bwd_col = (lax.broadcasted_iota(jnp.int32, head.shape, 1) // W_C) % 2 == 1
        o_ref[0:V7X_SUBLANES, :] = jnp.where((row == 0) & bwd_col, 0.0, head)


def _filter_features(n):
    t = jnp.linspace(0.0, 1.0, n, dtype=F32)[:, None]
    wpos = 2.0 * math.pi * jnp.arange(n, dtype=F32)[:, None] / n
    bands = jnp.linspace(1e-4, HY_BANDS - 1, HY_BANDS, dtype=F32)[None, :]
    feats = jnp.concatenate([t, jnp.cos(bands * wpos), -jnp.sin(bands * wpos)], axis=-1)
    return jnp.pad(feats, ((0, 0), (0, HY_EMB_PAD - HY_EMB)))


def _filter_decay_rates():
    deltas = jnp.abs(jnp.linspace(math.log(HY_TARGET) / HY_FAST, math.log(HY_TARGET) / HY_SLOW, W_C, dtype=F32))
    return jnp.tile(deltas, 2 * HY_ORDER).reshape(1, 2 * HY_ORDER * W_C)


def _filters(n, w1, b1, freq, w2, b2, w3):
    tm = min(n, 512)
    nf = 2 * HY_ORDER * W_C
    const = lambda shape: pl.BlockSpec(shape, lambda i: (0, 0))
    return pl.pallas_call(
        _filter_kernel,
        grid=(n // tm,),
        in_specs=[pl.BlockSpec((tm, HY_EMB_PAD), lambda i: (i, 0)),
                  const((HY_EMB_PAD, HY_FH)), const((1, HY_FH)), const((1, HY_FH)),
                  const((HY_FH, HY_FH)), const((1, HY_FH)), const((HY_FH, nf)), const((1, nf))],
        out_specs=pl.BlockSpec((tm, nf), lambda i: (i, 0)),
        out_shape=jax.ShapeDtypeStruct((n, nf), F32),
        compiler_params=_params("arbitrary"),
        name="hyena_filter",
    )(_filter_features(n), jnp.pad(w1, ((0, HY_EMB_PAD - HY_EMB), (0, 0))), b1.reshape(1, HY_FH),
      freq.reshape(1, HY_FH), w2, b2.reshape(1, HY_FH), w3, _filter_decay_rates())


T2_BLOCKS = FFT_MINOR // V7X_SUBLANES


class _FftPlan:
    def __init__(self, n):
        assert n % (2 * FFT_MINOR) == 0
        self.n = n
        self.n_fft = 2 * n
        self.n1 = self.n_fft // FFT_MINOR
        self.t1 = self.n1 // 2
        self.k1 = self.n1 // 2 + 1
        self.k_block = max(d for d in range(1, STAGE_B_MAX_SLABS + 1) if self.k1 % d == 0)
        t1 = np.arange(self.t1)
        k1 = np.arange(self.k1)
        eye = np.eye(V7X_SUBLANES)
        ang = 2.0 * np.pi * ((k1[:, None] * t1[None, :]) % self.n1) / self.n1
        fa = np.concatenate([np.cos(ang), -np.sin(ang)], axis=0)
        self.stage_a = _bf16_table(np.kron(fa, eye))
        wgt = np.where((k1 == 0) | (k1 == self.n1 // 2), 1.0, 2.0) / self.n_fft
        fc = np.concatenate([np.cos(ang) * wgt[:, None], -np.sin(ang) * wgt[:, None]], axis=0).T
        self.stage_a_inv = _bf16_table(np.kron(fc, eye))
        t2 = np.arange(FFT_MINOR)
        k2 = np.arange(FFT_MINOR)
        idx = (t2[None, None, :] * (k1[:, None, None] + self.n1 * k2[None, :, None])) % self.n_fft
        ang_b = 2.0 * np.pi * idx / self.n_fft
        g = np.stack([np.cos(ang_b), -np.sin(ang_b)], axis=1)
        self.stage_b = _bf16_table(g)
        self.stage_b_inv = _bf16_table(np.transpose(g, (0, 1, 3, 2)))

    def scratch(self, signals):
        return pltpu.VMEM((signals, 2, self.k1, T2_BLOCKS, V7X_SUBLANES, V7X_LANES), F32)


def _time_tiles(ref, tb, rows):
    return ref[:, pl.ds(tb, 1)].reshape(rows, V7X_LANES)


def _stage_a_forward(plan, x_refs, fa_ref, a_s):
    def body(tb, carry):
        xs = jnp.concatenate([_time_tiles(xr, tb, plan.t1 * V7X_SUBLANES) for xr in x_refs], axis=1)
        av = jnp.dot(fa_ref[...], xs.astype(BF16), preferred_element_type=F32)
        for j in range(len(x_refs)):
            part = av[:, j * V7X_LANES:(j + 1) * V7X_LANES]
            a_s[j, :, :, pl.ds(tb, 1)] = part.reshape(2, plan.k1, 1, V7X_SUBLANES, V7X_LANES)
        return carry
    lax.fori_loop(0, T2_BLOCKS, body, 0, unroll=4)


def _stage_b_matrix(g_ref, i, inverse):
    gr, gi = g_ref[i, 0], g_ref[i, 1]
    if inverse:
        gi = -gi
    return jnp.concatenate([jnp.concatenate([gr, -gi], axis=1), jnp.concatenate([gi, gr], axis=1)], axis=0)


def _stage_b_slab(a_s, k, count):
    return jnp.concatenate(
        [a_s[j, :, pl.ds(k, 1)].reshape(2 * FFT_MINOR, V7X_LANES) for j in range(count)], axis=1)


def _spectrum_kernel(x0_ref, x1_ref, fa_ref, fb_ref, o_ref, a_s, *, plan):
    kb = pl.program_id(1)

    @pl.when(kb == 0)
    def _():
        _stage_a_forward(plan, (x0_ref, x1_ref), fa_ref, a_s)

    half = FFT_MINOR
    rows = 2 * half
    for i in range(plan.k_block):
        slab = _stage_b_slab(a_s, kb * plan.k_block + i, 2).astype(BF16)
        xs = jnp.dot(_stage_b_matrix(fb_ref, i, False), slab, preferred_element_type=F32)
        o_ref[i * rows:i * rows + half, :] = xs[:half, :V7X_LANES] + xs[:half, V7X_LANES:]
        o_ref[i * rows + half:(i + 1) * rows, :] = xs[half:, :V7X_LANES] - xs[half:, V7X_LANES:]


def _filter_spectrum(plan, filt):
    n, nf = filt.shape
    rows = 2 * FFT_MINOR
    kbs = plan.k_block
    n_cb = W_C // V7X_LANES
    filt = filt.reshape(plan.t1, T2_BLOCKS, V7X_SUBLANES, nf)
    sig = (plan.t1, T2_BLOCKS, V7X_SUBLANES, V7X_LANES)
    return pl.pallas_call(
        functools.partial(_spectrum_kernel, plan=plan),
        grid=(HY_ORDER * n_cb, plan.k1 // kbs),
        in_specs=[_resident(sig, lambda c, k: (0, 0, 0, 2 * (c // n_cb) * n_cb + c % n_cb)),
                  _resident(sig, lambda c, k: (0, 0, 0, (2 * (c // n_cb) + 1) * n_cb + c % n_cb)),
                  pl.BlockSpec(plan.stage_a.shape, lambda c, k: (0, 0)),
                  pl.BlockSpec((kbs, 2, FFT_MINOR, FFT_MINOR), lambda c, k: (k, 0, 0, 0))],
        out_specs=pl.BlockSpec((kbs * rows, V7X_LANES), lambda c, k: (k, c)),
        out_shape=jax.ShapeDtypeStruct((plan.k1 * rows, HY_ORDER * W_C), F32),
        scratch_shapes=[plan.scratch(2)],
        compiler_params=_params("arbitrary", "arbitrary"),
        name="filter_spectrum",
    )(filt, filt, plan.stage_a, plan.stage_b)


def _conv_kernel(v_ref, g_ref, h_ref, fa_ref, fb_ref, fbi_ref, fai_ref, sk_ref, o_ref, a_s, *, plan, bn):
    kb = pl.program_id(1)
    half = FFT_MINOR
    rows = 2 * half

    @pl.when(kb == 0)
    def _():
        _stage_a_forward(plan, [v_ref.at[b] for b in range(bn)], fa_ref, a_s)

    for i in range(plan.k_block):
        k = kb * plan.k_block + i
        xs = jnp.dot(_stage_b_matrix(fb_ref, i, False), _stage_b_slab(a_s, k, bn).astype(BF16),
                     preferred_element_type=F32)
        xr, xi = xs[:half], xs[half:]
        hr = jnp.concatenate([h_ref[i * rows:i * rows + half, :]] * bn, axis=1)
        hi = jnp.concatenate([h_ref[i * rows + half:(i + 1) * rows, :]] * bn, axis=1)
        ys = jnp.concatenate([xr * hr - xi * hi, xr * hi + xi * hr], axis=0)
        ds = jnp.dot(_stage_b_matrix(fbi_ref, i, True), ys.astype(BF16), preferred_element_type=F32)
        for b in range(bn):
            part = ds[:, b * V7X_LANES:(b + 1) * V7X_LANES]
            a_s[b, :, pl.ds(k, 1)] = part.reshape(2, 1, T2_BLOCKS, V7X_SUBLANES, V7X_LANES)

    @pl.when(kb == pl.num_programs(1) - 1)
    def _():
        t_rows = plan.t1 * V7X_SUBLANES

        def body(tb, carry):
            ds = jnp.concatenate(
                [a_s[b, :, :, pl.ds(tb, 1)].reshape(2 * plan.k1 * V7X_SUBLANES, V7X_LANES) for b in range(bn)],
                axis=1)
            y = jnp.dot(fai_ref[...], ds.astype(BF16), preferred_element_type=F32)
            for b in range(bn):
                vv = _time_tiles(v_ref.at[b], tb, t_rows)
                gg = _time_tiles(g_ref.at[b], tb, t_rows)
                res = gg * (y[:, b * V7X_LANES:(b + 1) * V7X_LANES] + sk_ref[...] * vv)
                o_ref[b, :, pl.ds(tb, 1)] = res.reshape(plan.t1, 1, V7X_SUBLANES, V7X_LANES)
            return carry
        lax.fori_loop(0, T2_BLOCKS, body, 0, unroll=4)


def _hyena_conv(plan, v_arr, v_col, g_arr, g_col, spec, order, skip):
    bn = v_arr.shape[0]
    cb = V7X_LANES
    rows = 2 * FFT_MINOR
    n_cb = W_C // cb
    kbs = plan.k_block
    sig = (bn, plan.t1, T2_BLOCKS, V7X_SUBLANES, cb)
    return pl.pallas_call(
        functools.partial(_conv_kernel, plan=plan, bn=bn),
        grid=(n_cb, plan.k1 // kbs),
        in_specs=[_resident(sig, lambda c, k: (0, 0, 0, 0, v_col + c)),
                  _resident(sig, lambda c, k: (0, 0, 0, 0, g_col + c)),
                  pl.BlockSpec((kbs * rows, cb), lambda c, k: (k, order * n_cb + c)),
                  pl.BlockSpec(plan.stage_a.shape, lambda c, k: (0, 0)),
                  pl.BlockSpec((kbs, 2, FFT_MINOR, FFT_MINOR), lambda c, k: (k, 0, 0, 0)),
                  pl.BlockSpec((kbs, 2, FFT_MINOR, FFT_MINOR), lambda c, k: (k, 0, 0, 0)),
                  pl.BlockSpec(plan.stage_a_inv.shape, lambda c, k: (0, 0)),
                  pl.BlockSpec((1, cb), lambda c, k: (0, c))],
        out_specs=_resident(sig, lambda c, k: (0, 0, 0, 0, c)),
        out_shape=jax.ShapeDtypeStruct((bn, plan.t1, T2_BLOCKS, V7X_SUBLANES, W_C), F32),
        scratch_shapes=[plan.scratch(bn)],
        compiler_params=_params("arbitrary", "arbitrary"),
        name="hyena_conv",
    )(v_arr, g_arr, spec, plan.stage_a, plan.stage_b, plan.stage_b_inv, plan.stage_a_inv,
      skip.reshape(1, W_C))


def _hyena(z, n_fft_len, filt_args, skip):
    bn, n, _ = z.shape
    plan = _FftPlan(n_fft_len)
    filt = _filters(n, *filt_args)
    if n_fft_len != n:
        filt = jnp.pad(filt, ((0, n_fft_len - n), (0, 0)))
        z = jnp.pad(z, ((0, 0), (0, n_fft_len - n), (0, 0)))
    spec = _filter_spectrum(plan, filt)
    n_cb = W_C // V7X_LANES
    z = z.reshape(bn, plan.t1, T2_BLOCKS, V7X_SUBLANES, 3 * W_C)
    y1 = _hyena_conv(plan, z, 0, z, n_cb, spec, 0, skip[0])
    y2 = _hyena_conv(plan, y1, 0, z, 2 * n_cb, spec, 1, skip[1])
    return y2.reshape(bn, n_fft_len, W_C)[:, :n]


def _dense_block_diag(w):
    h, c, d = w.shape[-3:]
    tiled = jnp.tile(w.reshape(w.shape[:-3] + (h * c, d)), (1,) * (w.ndim - 2) + (h,))
    on_diagonal = (jnp.arange(h * c)[:, None] // c) == (jnp.arange(h * d)[None, :] // d)
    return jnp.where(on_diagonal, tiled, jnp.zeros_like(tiled))


def _tile_rows(n):
    return min(n, 512)


def kernel(x, c, ctx, c_ctx, w_mod, b_mod, norm_g, ffn_w_in, ffn_w_out, w_in, w_out, lru_conv_w, lru_conv_b,
           lru_gate_w, lru_gate_b, lru_lambda, pool_w, pool_b, pool_scale, hy_conv_w, hy_conv_b, hy_w1, hy_b1,
           hy_freq, hy_w2, hy_b2, hy_w3, hy_skip, final_g):
    bn, n, _ = x.shape
    n_ctx = ctx.shape[1]
    depth = w_mod.shape[0]
    assert bn + 1 <= V7X_SUBLANES and n % GRID_W == 0 and n % POOL_TILE == 0 and n_ctx == POOL_TILE
    tm_x, tm_c = _tile_rows(n), _tile_rows(n_ctx)

    cvec = jnp.zeros((V7X_SUBLANES, D_MODEL), F32).at[:bn].set(c).at[bn].set(c_ctx)
    mods = _modulation(cvec, w_mod, b_mod)
    band_x, inv_x = _pool_tables(GRID_W)
    band_c, inv_c = _pool_tables(n_ctx)
    zeros = jnp.zeros((bn, W_A), F32)
    f_in = ffn_w_in.astype(BF16)
    f_out = ffn_w_out.astype(BF16)

    xc = ctx
    for l in range(depth):
        last = l == depth - 1
        mx = mods[l, :bn].reshape(bn, N_MOD, D_MODEL)
        mc = jnp.broadcast_to(mods[l, bn].reshape(1, N_MOD, D_MODEL), (bn, N_MOD, D_MODEL))
        w_in_l = w_in[l].astype(BF16)
        w_out_l = w_out[l].astype(BF16)
        gate_dense = _dense_block_diag(lru_gate_w[l])
        wg = [(0.5 * jnp.concatenate([gate_dense[d, 0], gate_dense[d, 1]], axis=-1)).astype(BF16) for d in range(2)]
        bg = [0.5 * lru_gate_b[l, d].reshape(1, 2 * W_A) for d in range(2)]
        lam = [lru_lambda[l, d].reshape(1, W_A) for d in range(2)]
        pw = _dense_block_diag(pool_w[l]).astype(BF16)
        filt_args = (hy_w1[l], hy_b1[l], hy_freq[l], hy_w2[l], hy_b2[l], hy_w3[l])
        proj_args = (w_in_l, lru_conv_w[l], lru_conv_b[l], hy_conv_w[l], hy_conv_b[l])

        def mixer(xs, mod, tm, h0f, band, inv, fft_len):
            ua, ga, up, z, hf = _proj(xs, mod, norm_g[l, 1], *proj_args, wg[0], bg[0], lam[0], h0f, tm=tm)
            yc = _hyena(z, fft_len, filt_args, hy_skip[l])
            return lambda h0b: (ua, wg[1], bg[1], lam[1], h0b, hf, ga, up, yc, band, inv, pw, pool_b[l],
                                pool_scale[l], w_out_l), hf[:, -1]

        x = _ffn(x, mx, norm_g[l, 0], f_in, f_out, (l, 0), final_g, j0=0, final=False, tm=tm_x)
        xc = _ffn(xc.reshape(1, bn * n_ctx, D_MODEL), mc[:1], norm_g[l, 0], f_in, f_out, (l, 0), final_g, j0=0,
                  final=False, tm=bn * n_ctx).reshape(bn, n_ctx, D_MODEL)
        if last:
            ua_c, _, _, _, hf_c = _proj(xc, mc, norm_g[l, 1], *proj_args, wg[0], bg[0], lam[0], zeros, tm=tm_c)
            h0f = hf_c[:, -1]
            h0b = _lru_scan(ua_c, wg[1], bg[1], lam[1], zeros, tm=tm_c, reverse=True)[:, 0]
        else:
            mix_c, h0f = mixer(xc, mc, tm_c, zeros, band_c, inv_c, n_ctx)
            xc, h0b = _ffn(xc, mc, norm_g[l, 2], f_in, f_out, (l, 1), final_g, mix_c(zeros), j0=6, final=False,
                           tm=tm_c)
        mix_x, _ = mixer(x, mx, tm_x, h0f, band_x, inv_x, n)
        x, _ = _ffn(x, mx, norm_g[l, 2], f_in, f_out, (l, 1), final_g, mix_x(h0b), j0=6, final=last, tm=tm_x)
    return x
```

```python
import functools
import math

import numpy as np
import jax
import jax.numpy as jnp
from jax import lax
from jax.experimental import pallas as pl
from jax.experimental.pallas import tpu as pltpu

F32 = jnp.float32
BF16 = jnp.bfloat16

D_MODEL = 1024
GRID_W = 64
N_MOD = 9
W_A, W_B, W_C = 512, 256, 256
D_IN = 2 * W_A + W_B + 3 * W_C
RG_HEADS, RG_HD, RG_CONV, RG_C = 8, 64, 4, 8.0
POOL_WINDOWS = (2, 4, 8, 16)
POOL_GW = W_B // len(POOL_WINDOWS)
HY_ORDER, HY_SHORT, HY_BANDS, HY_FH = 2, 3, 16, 64
HY_EMB = 2 * HY_BANDS + 1
HY_EMB_PAD = 40
HY_TARGET, HY_FAST, HY_SLOW = 1e-2, 0.3, 1.5
D_FF = 2816
EPS = 1e-6

V7X_LANES = 128
V7X_SUBLANES = 8
V7X_MXU_DIM = 256
V7X_VMEM_LIMIT_BYTES = 56 * 1024 * 1024

FFT_MINOR = V7X_LANES
HALO = V7X_SUBLANES
FF_CHUNKS = ((0, 768), (768, 768), (1536, 768), (2304, 512))
POOL_TILE = 256
SCAN_PARTS = 4
STAGE_B_MAX_SLABS = 13


def _params(*sem):
    return pltpu.CompilerParams(dimension_semantics=sem, vmem_limit_bytes=V7X_VMEM_LIMIT_BYTES)


def _resident(shape, index_map):
    return pl.BlockSpec(shape, index_map, pipeline_mode=pl.Buffered(1))


def _bf16_table(values):
    return jnp.asarray(values, F32).astype(BF16)


def _split_bf16(x):
    hi = x.astype(BF16)
    return hi, (x - hi.astype(F32)).astype(BF16)


def _dot_bf16x3(a, b):
    a_hi, a_lo = _split_bf16(a)
    b_hi, b_lo = _split_bf16(b)
    d = lambda p, q: jnp.dot(p, q, preferred_element_type=F32)
    return d(a_hi, b_hi) + (d(a_lo, b_hi) + d(a_hi, b_lo))


def _adanorm(x, g, shift, scale):
    ms = jnp.mean(x * x, axis=-1, keepdims=True)
    return (x * lax.rsqrt(ms + EPS)) * g * (1.0 + scale) + shift


def _mod_kernel(c_ref, w_ref, b_ref, o_ref):
    s = c_ref[...]
    s = s * jax.nn.sigmoid(s)
    o_ref[0] = jnp.dot(s.astype(BF16), w_ref[0].astype(BF16), preferred_element_type=F32) + b_ref[0]


def _modulation(cvec, w_mod, b_mod):
    depth, _, n = w_mod.shape
    tn = n // 4
    return pl.pallas_call(
        _mod_kernel,
        grid=(depth, n // tn),
        in_specs=[pl.BlockSpec((V7X_SUBLANES, D_MODEL), lambda l, j: (0, 0)),
                  pl.BlockSpec((1, D_MODEL, tn), lambda l, j: (l, 0, j)),
                  pl.BlockSpec((1, 1, tn), lambda l, j: (l, 0, j))],
        out_specs=pl.BlockSpec((1, V7X_SUBLANES, tn), lambda l, j: (l, 0, j)),
        out_shape=jax.ShapeDtypeStruct((depth, V7X_SUBLANES, n), F32),
        compiler_params=_params("arbitrary", "arbitrary"),
        name="modulation",
    )(cvec, w_mod, b_mod.reshape(depth, 1, n))


def _pool_tile(x, band_ref, inv_ref, w_ref, b_ref, s_ref):
    xb = x.astype(BF16)
    col = lax.broadcasted_iota(jnp.int32, x.shape, 1) // POOL_GW
    tot = jnp.zeros_like(x)
    for g in range(len(POOL_WINDOWS)):
        tot = jnp.where(col == g, jnp.dot(band_ref[g], xb, preferred_element_type=F32), tot)
    pooled = tot * inv_ref[...] - x
    y = jnp.dot(pooled.astype(BF16), w_ref[...], preferred_element_type=F32)
    return (y + b_ref[...]) * s_ref[...]


def _ffn_kernel(x_ref, m_ref, g_ref, win_ref, wout_ref, fg_ref, *rest, j0, final, mix, tm, n_tiles):
    x = x_ref[0]
    if mix:
        (u0_ref, hf0_ref, ga0_ref, u_ref, hf_ref, ga_ref, wg_ref, bg_ref, lam_ref, h0_ref, up_ref, yc_ref, band_ref,
         inv_ref, pw_ref, pb_ref, ps_ref, wo_ref, o_ref, hend_ref, a_s, b_s, c_s, ya_s) = rest
        i = pl.program_id(1)

        parts = len(FF_CHUNKS)
        part_rows = tm // parts

        @pl.when(i == 0)
        def _():
            _lru_coeffs(u0_ref[0], wg_ref, bg_ref, lam_ref, a_s, b_s)
            c_s[...] = _lru_scan_tile(a_s, b_s, h0_ref[0], tm=tm, reverse=True)
            ya_s[0] = (hf0_ref[0] + b_s[...]) * jax.nn.gelu(ga0_ref[0])

        def next_tile_part(j, carry):
            rows = slice(tm - (j + 1) * part_rows, tm - j * part_rows)
            per = part_rows // V7X_SUBLANES
            _lru_coeffs(u_ref[0, rows, :], wg_ref, bg_ref, lam_ref, a_s, b_s, rows)
            carry = _lru_scan_tile(a_s, b_s, carry, tm=tm, reverse=True, inline=range(j * per, (j + 1) * per))
            ya_s[(i + 1) % 2, rows, :] = (hf_ref[0, rows, :] + b_s[rows, :]) * jax.nn.gelu(ga_ref[0, rows, :])
            return carry

        ya = ya_s[i % 2]
        yb = jnp.concatenate([_pool_tile(up_ref[0, r:r + POOL_TILE, :], band_ref, inv_ref, pw_ref, pb_ref, ps_ref)
                              for r in range(0, tm, POOL_TILE)], axis=0)
        y = jnp.dot(ya.astype(BF16), wo_ref[0:W_A, :], preferred_element_type=F32)
        y = y + jnp.dot(yb.astype(BF16), wo_ref[W_A:W_A + W_B, :], preferred_element_type=F32)
        y = y + jnp.dot(yc_ref[0].astype(BF16), wo_ref[W_A + W_B:, :], preferred_element_type=F32)
        x = x + m_ref[0, 5:6, :] * y
    else:
        (o_ref,) = rest
    shift = m_ref[0, j0:j0 + 1, :]
    scale = m_ref[0, j0 + 1:j0 + 2, :]
    gate = m_ref[0, j0 + 2:j0 + 3, :]
    h = _adanorm(x, g_ref[...], shift, scale).astype(BF16)
    acc = None
    ahead = mix and n_tiles > 1
    carry = c_s[...] if ahead else None
    for j, (s, w) in enumerate(FF_CHUNKS):
        if ahead:
            carry = next_tile_part(j, carry)
        gt = jnp.dot(h, win_ref[:, s:s + w], preferred_element_type=F32)
        up = jnp.dot(h, win_ref[:, D_FF + s:D_FF + s + w], preferred_element_type=F32)
        a = (gt * jax.nn.sigmoid(gt) * up).astype(BF16)
        p = jnp.dot(a, wout_ref[s:s + w, :], preferred_element_type=F32)
        acc = p if acc is None else acc + p
    if ahead:
        c_s[...] = jnp.where(i + 1 < n_tiles, carry, c_s[...])
    if mix:
        hend_ref[0] = c_s[...]
    y = x + (0.5 * gate) * acc
    if final:
        ms = jnp.mean(y * y, axis=-1, keepdims=True)
        y = (y * lax.rsqrt(ms + EPS)) * fg_ref[...]
    o_ref[0] = y


def _ffn(x, mod, g, w_in, w_out, wsel, final_g, mix=None, *, j0, final, tm):
    bn, n, _ = x.shape
    last = n // tm - 1
    tile_map = (lambda b, i: (b, last - i, 0)) if mix is not None else (lambda b, i: (b, i, 0))
    tile = lambda w: pl.BlockSpec((1, tm, w), tile_map)
    const = lambda shape: pl.BlockSpec(shape, lambda b, i: (0,) * len(shape))
    in_specs = [tile(D_MODEL),
                pl.BlockSpec((1, N_MOD, D_MODEL), lambda b, i: (b, 0, 0)),
                const((1, D_MODEL)),
                _resident((None, None, D_MODEL, 2 * D_FF), lambda b, i: wsel + (0, 0)),
                _resident((None, None, D_FF, D_MODEL), lambda b, i: wsel + (0, 0)),
                const((1, D_MODEL))]
    args = [x, mod, g.reshape(1, D_MODEL), w_in, w_out, final_g.reshape(1, D_MODEL)]
    x_shape = jax.ShapeDtypeStruct(x.shape, F32)
    if mix is None:
        return pl.pallas_call(
            functools.partial(_ffn_kernel, j0=j0, final=final, mix=False, tm=tm, n_tiles=n // tm),
            grid=(bn, n // tm), in_specs=in_specs, out_specs=tile(D_MODEL), out_shape=x_shape,
            compiler_params=_params("arbitrary", "arbitrary"), name="ffn",
        )(*args)
    u, wg, bg, lam, h0, hf, ga, up, yc, band, inv, pw, pb, ps, wo = mix
    first = _resident((1, tm, W_A), lambda b, i: (b, last, 0))
    nxt = pl.BlockSpec((1, tm, W_A), lambda b, i: (b, last - jnp.minimum(i + 1, last), 0))
    in_specs += ([first] * 3 + [nxt] * 3 + _lru_param_specs()
                 + [tile(W_B), tile(W_C), const(band.shape), const(inv.shape),
                    const((W_B, W_B)), const((1, W_B)), const((1, W_B)),
                    _resident((None, D_MODEL, D_MODEL), lambda b, i: (wsel[0], 0, 0))])
    args += [u, hf, ga, u, hf, ga, wg, bg, lam, h0.reshape(bn, 1, W_A), up, yc, band, inv, pw, pb.reshape(1, W_B),
             ps.reshape(1, W_B), wo]
    out, hend = pl.pallas_call(
        functools.partial(_ffn_kernel, j0=j0, final=final, mix=True, tm=tm, n_tiles=n // tm),
        grid=(bn, n // tm),
        in_specs=in_specs,
        out_specs=[tile(D_MODEL), pl.BlockSpec((1, 1, W_A), lambda b, i: (b, 0, 0))],
        out_shape=[x_shape, jax.ShapeDtypeStruct((bn, 1, W_A), F32)],
        scratch_shapes=_lru_scratch(tm) + [pltpu.VMEM((2, tm, W_A), F32)],
        compiler_params=_params("arbitrary", "arbitrary"),
        name="mix_ffn",
    )(*args)
    return out, hend.reshape(bn, W_A)


def _proj_kernel(xp_ref, x_ref, xn_ref, m_ref, g_ref, w_ref, cwa_ref, cba_ref, cwh_ref, cbh_ref,
                 wg_ref, bg_ref, lam_ref, h0_ref, ua_ref, ga_ref, up_ref, z_ref, hf_ref,
                 pa_s, ph_s, a_s, b_s, c_s, uprev_s, *, tm, n_tiles):
    i = pl.program_id(1)
    ti = jnp.minimum(i, n_tiles - 1)
    pipelined = n_tiles > 1

    @pl.when(i == 0)
    def _():
        c_s[...] = h0_ref[0]
        if pipelined:
            uprev_s[...] = jnp.zeros_like(uprev_s)

    parts = SCAN_PARTS
    part_rows = tm // parts

    def scan_part(j, carry):
        rows = slice(j * part_rows, (j + 1) * part_rows)
        per = part_rows // V7X_SUBLANES
        _lru_coeffs(uprev_s[rows, :], wg_ref, bg_ref, lam_ref, a_s, b_s, rows)
        carry = _lru_scan_tile(a_s, b_s, carry, tm=tm, reverse=False, inline=range(j * per, (j + 1) * per))
        hf_ref[0, rows, :] = b_s[rows, :]
        return carry

    norm = lambda v: _adanorm(v, g_ref[...], m_ref[0, 3:4, :], m_ref[0, 4:5, :])
    h_prev = jnp.where(ti > 0, norm(xp_ref[0]), 0.0)
    h_next = jnp.where(ti < n_tiles - 1, norm(xn_ref[0]), 0.0)
    h = jnp.concatenate([h_prev, norm(x_ref[0]), h_next], axis=0).astype(BF16)
    proj = lambda lo, hi: jnp.dot(h, w_ref[:, lo:hi], preferred_element_type=F32)

    def short_conv(lo, hi, w_conv_ref, b_conv_ref, taps, pad_left, p_s):
        p_s[...] = proj(lo, hi)
        acc = b_conv_ref[...] + w_conv_ref[0:1, :] * p_s[pl.ds(HALO - pad_left, tm), :]
        for k in range(1, taps):
            acc = acc + w_conv_ref[k:k + 1, :] * p_s[pl.ds(HALO - pad_left + k, tm), :]
        return acc

    carry = scan_part(0, c_s[...]) if pipelined else None
    z_ref[0] = short_conv(2 * W_A + W_B, D_IN, cwh_ref, cbh_ref, HY_SHORT, HY_SHORT // 2, ph_s)
    carry = scan_part(1, carry) if pipelined else None
    ua = short_conv(0, W_A, cwa_ref, cba_ref, RG_CONV, RG_CONV // 2, pa_s)
    ua_ref[0] = ua
    carry = scan_part(2, carry) if pipelined else None
    rest = proj(W_A, 2 * W_A + W_B)
    ga_ref[0] = rest[HALO:HALO + tm, :W_A]
    up_ref[0] = rest[HALO:HALO + tm, W_A:]
    if pipelined:
        carry = scan_part(3, carry)
        c_s[...] = jnp.where(i > 0, carry, c_s[...])
        uprev_s[...] = ua
    else:
        uprev_s[...] = ua
        carry = c_s[...]
        for j in range(parts):
            carry = scan_part(j, carry)


def _proj(x, mod, g, w_in, cwa, cba, cwh, cbh, wg, bg, lam, h0, *, tm, layer):
    bn, n, _ = x.shape
    n_tiles = n // tm
    hb = tm // HALO
    last_hb = n // HALO - 1
    outs = (W_A, W_A, W_B, 3 * W_C)
    cur = lambda i: jnp.minimum(i, n_tiles - 1)
    const = lambda shape: pl.BlockSpec(shape, lambda b, i: (0,) * len(shape))
    return pl.pallas_call(
        functools.partial(_proj_kernel, tm=tm, n_tiles=n_tiles),
        grid=(bn, n_tiles + 1 if n_tiles > 1 else 1),
        in_specs=[pl.BlockSpec((1, HALO, D_MODEL), lambda b, i: (b, jnp.maximum(cur(i) * hb - 1, 0), 0)),
                  pl.BlockSpec((1, tm, D_MODEL), lambda b, i: (b, cur(i), 0)),
                  pl.BlockSpec((1, HALO, D_MODEL), lambda b, i: (b, jnp.minimum((cur(i) + 1) * hb, last_hb), 0)),
                  pl.BlockSpec((1, N_MOD, D_MODEL), lambda b, i: (b, 0, 0)),
                  const((1, D_MODEL)),
                  _resident((None, D_MODEL, D_IN), lambda b, i: (layer, 0, 0)),
                  const((RG_CONV, W_A)), const((1, W_A)), const((HY_SHORT, 3 * W_C)), const((1, 3 * W_C))]
                 + _lru_param_specs(),
        out_specs=[pl.BlockSpec((1, tm, w), lambda b, i: (b, cur(i), 0)) for w in outs]
                  + [pl.BlockSpec((1, tm, W_A), lambda b, i: (b, jnp.maximum(i - 1, 0), 0))],
        out_shape=[jax.ShapeDtypeStruct((bn, n, w), F32) for w in outs + (W_A,)],
        scratch_shapes=[pltpu.VMEM((tm + 2 * HALO, W_A), F32), pltpu.VMEM((tm + 2 * HALO, 3 * W_C), F32)]
                       + _lru_scratch(tm) + [pltpu.VMEM((tm, W_A), F32)],
        compiler_params=_params("arbitrary", "arbitrary"),
        name="proj_scan",
    )(x, x, x, mod, g.reshape(1, D_MODEL), w_in, cwa, cba.reshape(1, W_A), cwh, cbh.reshape(1, 3 * W_C),
      wg, bg, lam, h0.reshape(bn, 1, W_A))


def _lru_coeffs(u, wg_ref, bg_ref, lam_ref, a_s, b_s, rows=slice(None)):
    ub = u.astype(BF16)
    blk = V7X_MXU_DIM

    def gate_block(c):
        r0 = (c * blk) % W_A
        return jnp.dot(ub[:, r0:r0 + blk], wg_ref[r0:r0 + blk, c * blk:(c + 1) * blk], preferred_element_type=F32)

    half = jnp.concatenate([gate_block(c) for c in range(2 * W_A // blk)], axis=1) + bg_ref[...]
    q = (-0.25 * RG_C) * jax.nn.softplus(-lam_ref[...])
    t = jnp.tanh(q * jnp.tanh(half[:, :W_A]) + q)
    inv = 1.0 / (1.0 - t)
    a_s[rows, :] = (1.0 + t) * inv
    b_s[rows, :] = (jnp.sqrt(-t) * inv) * ((jnp.tanh(half[:, W_A:]) + 1.0) * u)


def _lru_scan_tile(a_s, b_s, carry, *, tm, reverse, inline=None):
    groups = tm // V7X_SUBLANES
    rowi = lax.broadcasted_iota(jnp.int32, (V7X_SUBLANES, W_A), 0)

    def body(k, c):
        gi = (groups - 1 - k) if reverse else k
        r0 = gi * V7X_SUBLANES if inline is not None else pl.multiple_of(gi * V7X_SUBLANES, V7X_SUBLANES)
        a = a_s[pl.ds(r0, V7X_SUBLANES), :]
        b = b_s[pl.ds(r0, V7X_SUBLANES), :]
        first = rowi == (V7X_SUBLANES - 1 if reverse else 0)
        b = jnp.where(first, a * c + b, b)
        for s in (1, 2, 4):
            if reverse:
                shift, m = V7X_SUBLANES - s, rowi < V7X_SUBLANES - s
            else:
                shift, m = s, rowi >= s
            b = jnp.where(m, a * pltpu.roll(b, shift, 0) + b, b)
            if s < V7X_SUBLANES // 2:
                a = jnp.where(m, a * pltpu.roll(a, shift, 0), a)
        b_s[pl.ds(r0, V7X_SUBLANES), :] = b
        return b[0:1, :] if reverse else b[V7X_SUBLANES - 1:V7X_SUBLANES, :]

    if inline is not None:
        for k in inline:
            carry = body(k, carry)
        return carry
    return lax.fori_loop(0, groups, body, carry, unroll=4)


def _lru_scan_kernel(u_ref, wg_ref, bg_ref, lam_ref, h0_ref, h_ref, a_s, b_s, c_s, *, tm, reverse):
    @pl.when(pl.program_id(1) == 0)
    def _():
        c_s[...] = h0_ref[0]

    _lru_coeffs(u_ref[0], wg_ref, bg_ref, lam_ref, a_s, b_s)
    c_s[...] = _lru_scan_tile(a_s, b_s, c_s[...], tm=tm, reverse=reverse)
    h_ref[0] = b_s[...]


def _lru_param_specs():
    const = lambda shape: pl.BlockSpec(shape, lambda b, i: (0,) * len(shape))
    return [const((W_A, 2 * W_A)), const((1, 2 * W_A)), const((1, W_A)),
            pl.BlockSpec((1, 1, W_A), lambda b, i: (b, 0, 0))]


def _lru_scratch(tm):
    return [pltpu.VMEM((tm, W_A), F32), pltpu.VMEM((tm, W_A), F32), pltpu.VMEM((1, W_A), F32)]


def _lru_scan(u, wg, bg, lam, h0, *, tm, reverse):
    bn, n, _ = u.shape
    last = n // tm - 1
    tile_map = (lambda b, i: (b, last - i, 0)) if reverse else (lambda b, i: (b, i, 0))
    return pl.pallas_call(
        functools.partial(_lru_scan_kernel, tm=tm, reverse=reverse),
        grid=(bn, n // tm),
        in_specs=[pl.BlockSpec((1, tm, W_A), tile_map)] + _lru_param_specs(),
        out_specs=pl.BlockSpec((1, tm, W_A), tile_map),
        out_shape=jax.ShapeDtypeStruct(u.shape, F32),
        scratch_shapes=_lru_scratch(tm),
        compiler_params=_params("arbitrary", "arbitrary"),
        name="lru_scan",
    )(u, wg, bg, lam, h0.reshape(bn, 1, W_A))


def _pool_tables(row_len):
    t = np.arange(POOL_TILE)
    p = t % row_len
    base = t - p
    band = np.zeros((len(POOL_WINDOWS), POOL_TILE, POOL_TILE), np.float32)
    inv = np.zeros((POOL_TILE, W_B), np.float32)
    for g, win in enumerate(POOL_WINDOWS):
        lo = np.clip(p - win // 2, 0, row_len)
        hi = np.clip(p + win - win // 2, 0, row_len)
        s = np.arange(POOL_TILE)[None, :]
        band[g] = ((s >= (base + lo)[:, None]) & (s < (base + hi)[:, None])).astype(np.float32)
        inv[:, g * POOL_GW:(g + 1) * POOL_GW] = (1.0 / (hi - lo).astype(np.float64))[:, None]
    return _bf16_table(band), jnp.asarray(inv, F32)


def _filter_kernel(f_ref, w1_ref, b1_ref, fr_ref, w2_ref, b2_ref, w3_ref, dl_ref, o_ref):
    feats = f_ref[...]
    fr = fr_ref[...]
    hid = jnp.sin(fr * (_dot_bf16x3(feats, w1_ref[...]) + b1_ref[...]))
    hid = jnp.sin(fr * (_dot_bf16x3(hid, w2_ref[...]) + b2_ref[...]))
    filt = _dot_bf16x3(hid, w3_ref[...])
    filt = filt * jnp.exp(-feats[:, 0:1] * dl_ref[...])
    o_ref[...] = filt

    @pl.when(pl.program_id(0) == 0)
    def _():
        head = filt[0:V7X_SUBLANES, :]
        row = lax.broadcasted_iota(jnp.int32, head.shape, 0)
        bwd_col = (lax.broadcasted_iota(jnp.int32, head.shape, 1) // W_C) % 2 == 1
        o_ref[0:V7X_SUBLANES, :] = jnp.where((row == 0) & bwd_col, 0.0, head)


def _filter_features(n):
    t = jnp.linspace(0.0, 1.0, n, dtype=F32)[:, None]
    wpos = 2.0 * math.pi * jnp.arange(n, dtype=F32)[:, None] / n
    bands = jnp.linspace(1e-4, HY_BANDS - 1, HY_BANDS, dtype=F32)[None, :]
    feats = jnp.concatenate([t, jnp.cos(bands * wpos), -jnp.sin(bands * wpos)], axis=-1)
    return jnp.pad(feats, ((0, 0), (0, HY_EMB_PAD - HY_EMB)))


def _filter_decay_rates():
    deltas = jnp.abs(jnp.linspace(math.log(HY_TARGET) / HY_FAST, math.log(HY_TARGET) / HY_SLOW, W_C, dtype=F32))
    return jnp.tile(deltas, 2 * HY_ORDER).reshape(1, 2 * HY_ORDER * W_C)


def _filters(n, w1, b1, freq, w2, b2, w3):
    tm = min(n, 512)
    nf = 2 * HY_ORDER * W_C
    const = lambda shape: pl.BlockSpec(shape, lambda i: (0, 0))
    return pl.pallas_call(
        _filter_kernel,
        grid=(n // tm,),
        in_specs=[pl.BlockSpec((tm, HY_EMB_PAD), lambda i: (i, 0)),
                  const((HY_EMB_PAD, HY_FH)), const((1, HY_FH)), const((1, HY_FH)),
                  const((HY_FH, HY_FH)), const((1, HY_FH)), const((HY_FH, nf)), const((1, nf))],
        out_specs=pl.BlockSpec((tm, nf), lambda i: (i, 0)),
        out_shape=jax.ShapeDtypeStruct((n, nf), F32),
        compiler_params=_params("arbitrary"),
        name="hyena_filter",
    )(_filter_features(n), jnp.pad(w1, ((0, HY_EMB_PAD - HY_EMB), (0, 0))), b1.reshape(1, HY_FH),
      freq.reshape(1, HY_FH), w2, b2.reshape(1, HY_FH), w3, _filter_decay_rates())


T2_BLOCKS = FFT_MINOR // V7X_SUBLANES


class _FftPlan:
    def __init__(self, n):
        assert n % (2 * FFT_MINOR) == 0
        self.n = n
        self.n_fft = 2 * n
        self.n1 = self.n_fft // FFT_MINOR
        self.t1 = self.n1 // 2
        self.k1 = self.n1 // 2 + 1
        self.k_block = max(d for d in range(1, STAGE_B_MAX_SLABS + 1) if self.k1 % d == 0)
        t1 = np.arange(self.t1)
        k1 = np.arange(self.k1)
        eye = np.eye(V7X_SUBLANES)
        ang = 2.0 * np.pi * ((k1[:, None] * t1[None, :]) % self.n1) / self.n1
        fa = np.concatenate([np.cos(ang), -np.sin(ang)], axis=0)
        self.stage_a = _bf16_table(np.kron(fa, eye))
        wgt = np.where((k1 == 0) | (k1 == self.n1 // 2), 1.0, 2.0) / self.n_fft
        fc = np.concatenate([np.cos(ang) * wgt[:, None], -np.sin(ang) * wgt[:, None]], axis=0).T
        self.stage_a_inv = _bf16_table(np.kron(fc, eye))
        t2 = np.arange(FFT_MINOR)
        k2 = np.arange(FFT_MINOR)
        idx = (t2[None, None, :] * (k1[:, None, None] + self.n1 * k2[None, :, None])) % self.n_fft
        ang_b = 2.0 * np.pi * idx / self.n_fft
        g = np.stack([np.cos(ang_b), -np.sin(ang_b)], axis=1)
        self.stage_b = _bf16_table(g)
        self.stage_b_inv = _bf16_table(np.transpose(g, (0, 1, 3, 2)))

    def scratch(self, signals):
        return pltpu.VMEM((signals, 2, self.k1, T2_BLOCKS, V7X_SUBLANES, V7X_LANES), F32)


def _time_tiles(ref, tb, rows):
    return ref[:, pl.ds(tb, 1)].reshape(rows, V7X_LANES)


def _stage_a_forward(plan, x_refs, fa_ref, a_s):
    def body(tb, carry):
        xs = jnp.concatenate([_time_tiles(xr, tb, plan.t1 * V7X_SUBLANES) for xr in x_refs], axis=1)
        av = jnp.dot(fa_ref[...], xs.astype(BF16), preferred_element_type=F32)
        for j in range(len(x_refs)):
            part = av[:, j * V7X_LANES:(j + 1) * V7X_LANES]
            a_s[j, :, :, pl.ds(tb, 1)] = part.reshape(2, plan.k1, 1, V7X_SUBLANES, V7X_LANES)
        return carry
    lax.fori_loop(0, T2_BLOCKS, body, 0, unroll=8)


def _stage_b_matrix(g_ref, i, inverse):
    gr, gi = g_ref[i, 0], g_ref[i, 1]
    if inverse:
        gi = -gi
    return jnp.concatenate([jnp.concatenate([gr, -gi], axis=1), jnp.concatenate([gi, gr], axis=1)], axis=0)


def _stage_b_slab(a_s, k, count):
    return jnp.concatenate(
        [a_s[j, :, pl.ds(k, 1)].reshape(2 * FFT_MINOR, V7X_LANES) for j in range(count)], axis=1)


def _spectrum_kernel(x0_ref, x1_ref, fa_ref, fb_ref, o_ref, a_s, *, plan):
    kb = pl.program_id(1)

    @pl.when(kb == 0)
    def _():
        _stage_a_forward(plan, (x0_ref, x1_ref), fa_ref, a_s)

    half = FFT_MINOR
    rows = 2 * half
    for i in range(plan.k_block):
        slab = _stage_b_slab(a_s, kb * plan.k_block + i, 2).astype(BF16)
        xs = jnp.dot(_stage_b_matrix(fb_ref, i, False), slab, preferred_element_type=F32)
        o_ref[i * rows:i * rows + half, :] = xs[:half, :V7X_LANES] + xs[:half, V7X_LANES:]
        o_ref[i * rows + half:(i + 1) * rows, :] = xs[half:, :V7X_LANES] - xs[half:, V7X_LANES:]


def _filter_spectrum(plan, filt):
    n, nf = filt.shape
    rows = 2 * FFT_MINOR
    kbs = plan.k_block
    n_cb = W_C // V7X_LANES
    filt = filt.reshape(plan.t1, T2_BLOCKS, V7X_SUBLANES, nf)
    sig = (plan.t1, T2_BLOCKS, V7X_SUBLANES, V7X_LANES)
    return pl.pallas_call(
        functools.partial(_spectrum_kernel, plan=plan),
        grid=(HY_ORDER * n_cb, plan.k1 // kbs),
        in_specs=[_resident(sig, lambda c, k: (0, 0, 0, 2 * (c // n_cb) * n_cb + c % n_cb)),
                  _resident(sig, lambda c, k: (0, 0, 0, (2 * (c // n_cb) + 1) * n_cb + c % n_cb)),
                  pl.BlockSpec(plan.stage_a.shape, lambda c, k: (0, 0)),
                  pl.BlockSpec((kbs, 2, FFT_MINOR, FFT_MINOR), lambda c, k: (k, 0, 0, 0))],
        out_specs=pl.BlockSpec((kbs * rows, V7X_LANES), lambda c, k: (k, c)),
        out_shape=jax.ShapeDtypeStruct((plan.k1 * rows, HY_ORDER * W_C), F32),
        scratch_shapes=[plan.scratch(2)],
        compiler_params=_params("arbitrary", "arbitrary"),
        name="filter_spectrum",
    )(filt, filt, plan.stage_a, plan.stage_b)


def _conv_kernel(v_ref, g_ref, h_ref, fa_ref, fb_ref, fbi_ref, fai_ref, sk_ref, o_ref, a_s, *, plan, bn):
    kb = pl.program_id(1)
    half = FFT_MINOR
    rows = 2 * half

    @pl.when(kb == 0)
    def _():
        _stage_a_forward(plan, [v_ref.at[b] for b in range(bn)], fa_ref, a_s)

    for i in range(plan.k_block):
        k = kb * plan.k_block + i
        xs = jnp.dot(_stage_b_matrix(fb_ref, i, False), _stage_b_slab(a_s, k, bn).astype(BF16),
                     preferred_element_type=F32)
        xr, xi = xs[:half], xs[half:]
        hr = jnp.concatenate([h_ref[i * rows:i * rows + half, :]] * bn, axis=1)
        hi = jnp.concatenate([h_ref[i * rows + half:(i + 1) * rows, :]] * bn, axis=1)
        ys = jnp.concatenate([xr * hr - xi * hi, xr * hi + xi * hr], axis=0)
        ds = jnp.dot(_stage_b_matrix(fbi_ref, i, True), ys.astype(BF16), preferred_element_type=F32)
        for b in range(bn):
            part = ds[:, b * V7X_LANES:(b + 1) * V7X_LANES]
            a_s[b, :, pl.ds(k, 1)] = part.reshape(2, 1, T2_BLOCKS, V7X_SUBLANES, V7X_LANES)

    @pl.when(kb == pl.num_programs(1) - 1)
    def _():
        t_rows = plan.t1 * V7X_SUBLANES

        def body(tb, carry):
            ds = jnp.concatenate(
                [a_s[b, :, :, pl.ds(tb, 1)].reshape(2 * plan.k1 * V7X_SUBLANES, V7X_LANES) for b in range(bn)],
                axis=1)
            y = jnp.dot(fai_ref[...], ds.astype(BF16), preferred_element_type=F32)
            for b in range(bn):
                vv = _time_tiles(v_ref.at[b], tb, t_rows)
                gg = _time_tiles(g_ref.at[b], tb, t_rows)
                res = gg * (y[:, b * V7X_LANES:(b + 1) * V7X_LANES] + sk_ref[...] * vv)
                o_ref[b, :, pl.ds(tb, 1)] = res.reshape(plan.t1, 1, V7X_SUBLANES, V7X_LANES)
            return carry
        lax.fori_loop(0, T2_BLOCKS, body, 0, unroll=8)


def _hyena_conv(plan, v_arr, v_col, g_arr, g_col, spec, order, skip):
    bn = v_arr.shape[0]
    cb = V7X_LANES
    rows = 2 * FFT_MINOR
    n_cb = W_C // cb
    kbs = plan.k_block
    sig = (bn, plan.t1, T2_BLOCKS, V7X_SUBLANES, cb)
    return pl.pallas_call(
        functools.partial(_conv_kernel, plan=plan, bn=bn),
        grid=(n_cb, plan.k1 // kbs),
        in_specs=[_resident(sig, lambda c, k: (0, 0, 0, 0, v_col + c)),
                  _resident(sig, lambda c, k: (0, 0, 0, 0, g_col + c)),
                  pl.BlockSpec((kbs * rows, cb), lambda c, k: (k, order * n_cb + c)),
                  pl.BlockSpec(plan.stage_a.shape, lambda c, k: (0, 0)),
                  pl.BlockSpec((kbs, 2, FFT_MINOR, FFT_MINOR), lambda c, k: (k, 0, 0, 0)),
                  pl.BlockSpec((kbs, 2, FFT_MINOR, FFT_MINOR), lambda c, k: (k, 0, 0, 0)),
                  pl.BlockSpec(plan.stage_a_inv.shape, lambda c, k: (0, 0)),
                  pl.BlockSpec((1, cb), lambda c, k: (0, c))],
        out_specs=_resident(sig, lambda c, k: (0, 0, 0, 0, c)),
        out_shape=jax.ShapeDtypeStruct((bn, plan.t1, T2_BLOCKS, V7X_SUBLANES, W_C), F32),
        scratch_shapes=[plan.scratch(bn)],
        compiler_params=_params("arbitrary", "arbitrary"),
        name="hyena_conv",
    )(v_arr, g_arr, spec, plan.stage_a, plan.stage_b, plan.stage_b_inv, plan.stage_a_inv,
      skip.reshape(1, W_C))


def _hyena(z, n_fft_len, filt_args, skip):
    bn, n, _ = z.shape
    plan = _FftPlan(n_fft_len)
    filt = _filters(n, *filt_args)
    if n_fft_len != n:
        filt = jnp.pad(filt, ((0, n_fft_len - n), (0, 0)))
        z = jnp.pad(z, ((0, 0), (0, n_fft_len - n), (0, 0)))
    spec = _filter_spectrum(plan, filt)
    n_cb = W_C // V7X_LANES
    z = z.reshape(bn, plan.t1, T2_BLOCKS, V7X_SUBLANES, 3 * W_C)
    y1 = _hyena_conv(plan, z, 0, z, n_cb, spec, 0, skip[0])
    y2 = _hyena_conv(plan, y1, 0, z, 2 * n_cb, spec, 1, skip[1])
    return y2.reshape(bn, n_fft_len, W_C)[:, :n]


def _dense_block_diag(w):
    h, c, d = w.shape[-3:]
    tiled = jnp.tile(w.reshape(w.shape[:-3] + (h * c, d)), (1,) * (w.ndim - 2) + (h,))
    on_diagonal = (jnp.arange(h * c)[:, None] // c) == (jnp.arange(h * d)[None, :] // d)
    return jnp.where(on_diagonal, tiled, jnp.zeros_like(tiled))


def _tile_rows(n):
    return min(n, 512)


def kernel(x, c, ctx, c_ctx, w_mod, b_mod, norm_g, ffn_w_in, ffn_w_out, w_in, w_out, lru_conv_w, lru_conv_b,
           lru_gate_w, lru_gate_b, lru_lambda, pool_w, pool_b, pool_scale, hy_conv_w, hy_conv_b, hy_w1, hy_b1,
           hy_freq, hy_w2, hy_b2, hy_w3, hy_skip, final_g):
    bn, n, _ = x.shape
    n_ctx = ctx.shape[1]
    depth = w_mod.shape[0]
    assert bn + 1 <= V7X_SUBLANES and n % GRID_W == 0 and n % POOL_TILE == 0 and n_ctx == POOL_TILE
    tm_x, tm_c = _tile_rows(n), _tile_rows(n_ctx)

    cvec = jnp.zeros((V7X_SUBLANES, D_MODEL), F32).at[:bn].set(c).at[bn].set(c_ctx)
    mods = _modulation(cvec, w_mod, b_mod)
    band_x, inv_x = _pool_tables(GRID_W)
    band_c, inv_c = _pool_tables(n_ctx)
    zeros = jnp.zeros((bn, W_A), F32)
    f_in = ffn_w_in.astype(BF16)
    f_out = ffn_w_out.astype(BF16)
    w_in_b = w_in.astype(BF16)
    w_out_b = w_out.astype(BF16)

    xc = ctx
    for l in range(depth):
        last = l == depth - 1
        mx = mods[l, :bn].reshape(bn, N_MOD, D_MODEL)
        mc = jnp.broadcast_to(mods[l, bn].reshape(1, N_MOD, D_MODEL), (bn, N_MOD, D_MODEL))
        gate_dense = _dense_block_diag(lru_gate_w[l])
        wg = [(0.5 * jnp.concatenate([gate_dense[d, 0], gate_dense[d, 1]], axis=-1)).astype(BF16) for d in range(2)]
        bg = [0.5 * lru_gate_b[l, d].reshape(1, 2 * W_A) for d in range(2)]
        lam = [lru_lambda[l, d].reshape(1, W_A) for d in range(2)]
        pw = _dense_block_diag(pool_w[l]).astype(BF16)
        filt_args = (hy_w1[l], hy_b1[l], hy_freq[l], hy_w2[l], hy_b2[l], hy_w3[l])
        proj_args = (w_in_b, lru_conv_w[l], lru_conv_b[l], hy_conv_w[l], hy_conv_b[l])

        def mixer(xs, mod, tm, h0f, band, inv, fft_len):
            ua, ga, up, z, hf = _proj(xs, mod, norm_g[l, 1], *proj_args, wg[0], bg[0], lam[0], h0f, tm=tm, layer=l)
            yc = _hyena(z, fft_len, filt_args, hy_skip[l])
            return lambda h0b: (ua, wg[1], bg[1], lam[1], h0b, hf, ga, up, yc, band, inv, pw, pool_b[l],
                                pool_scale[l], w_out_b), hf[:, -1]

        x = _ffn(x, mx, norm_g[l, 0], f_in, f_out, (l, 0), final_g, j0=0, final=False, tm=tm_x)
        xc = _ffn(xc, mc, norm_g[l, 0], f_in, f_out, (l, 0), final_g, j0=0, final=False, tm=tm_c)
        if last:
            ua_c, _, _, _, hf_c = _proj(xc, mc, norm_g[l, 1], *proj_args, wg[0], bg[0], lam[0], zeros, tm=tm_c, layer=l)
            h0f = hf_c[:, -1]
            h0b = _lru_scan(ua_c, wg[1], bg[1], lam[1], zeros, tm=tm_c, reverse=True)[:, 0]
        else:
            mix_c, h0f = mixer(xc, mc, tm_c, zeros, band_c, inv_c, n_ctx)
            xc, h0b = _ffn(xc, mc, norm_g[l, 2], f_in, f_out, (l, 1), final_g, mix_c(zeros), j0=6, final=False,
                           tm=tm_c)
        mix_x, _ = mixer(x, mx, tm_x, h0f, band_x, inv_x, n)
        x, _ = _ffn(x, mx, norm_g[l, 2], f_in, f_out, (l, 1), final_g, mix_x(h0b), j0=6, final=last, tm=tm_x)
    return x
```

```python
import functools
import math

import numpy as np
import jax
import jax.numpy as jnp
from jax import lax
from jax.experimental import pallas as pl
from jax.experimental.pallas import tpu as pltpu

F32 = jnp.float32
BF16 = jnp.bfloat16

D_MODEL = 1024
GRID_W = 64
N_MOD = 9
W_A, W_B, W_C = 512, 256, 256
D_IN = 2 * W_A + W_B + 3 * W_C
RG_HEADS, RG_HD, RG_CONV, RG_C = 8, 64, 4, 8.0
POOL_WINDOWS = (2, 4, 8, 16)
POOL_GW = W_B // len(POOL_WINDOWS)
HY_ORDER, HY_SHORT, HY_BANDS, HY_FH = 2, 3, 16, 64
HY_EMB = 2 * HY_BANDS + 1
HY_EMB_PAD = 40
HY_TARGET, HY_FAST, HY_SLOW = 1e-2, 0.3, 1.5
D_FF = 2816
EPS = 1e-6

V7X_LANES = 128
V7X_SUBLANES = 8
V7X_MXU_DIM = 256
V7X_VMEM_LIMIT_BYTES = 56 * 1024 * 1024

FFT_MINOR = V7X_LANES
HALO = V7X_SUBLANES
FF_CHUNKS = ((0, 768), (768, 768), (1536, 768), (2304, 512))
POOL_TILE = 256
SCAN_PARTS = 4
STAGE_B_MAX_SLABS = 13
DFT_LOOP_UNROLL = 8
FFN_TILE = 1024


def _params(*sem):
    return pltpu.CompilerParams(dimension_semantics=sem, vmem_limit_bytes=V7X_VMEM_LIMIT_BYTES)


def _resident(shape, index_map):
    return pl.BlockSpec(shape, index_map, pipeline_mode=pl.Buffered(1))


def _bf16_table(values):
    return jnp.asarray(values, F32).astype(BF16)


def _split_bf16(x):
    hi = x.astype(BF16)
    return hi, (x - hi.astype(F32)).astype(BF16)


def _dot_bf16x3(a, b):
    a_hi, a_lo = _split_bf16(a)
    b_hi, b_lo = _split_bf16(b)
    d = lambda p, q: jnp.dot(p, q, preferred_element_type=F32)
    return d(a_hi, b_hi) + (d(a_lo, b_hi) + d(a_hi, b_lo))


def _adanorm(x, g, shift, scale):
    ms = jnp.mean(x * x, axis=-1, keepdims=True)
    return (x * lax.rsqrt(ms + EPS)) * g * (1.0 + scale) + shift


def _mod_kernel(c_ref, w_ref, b_ref, o_ref):
    s = c_ref[...]
    s = s * jax.nn.sigmoid(s)
    o_ref[0] = jnp.dot(s.astype(BF16), w_ref[0].astype(BF16), preferred_element_type=F32) + b_ref[0]


def _modulation(cvec, w_mod, b_mod):
    depth, _, n = w_mod.shape
    tn = n // 4
    return pl.pallas_call(
        _mod_kernel,
        grid=(depth, n // tn),
        in_specs=[pl.BlockSpec((V7X_SUBLANES, D_MODEL), lambda l, j: (0, 0)),
                  pl.BlockSpec((1, D_MODEL, tn), lambda l, j: (l, 0, j)),
                  pl.BlockSpec((1, 1, tn), lambda l, j: (l, 0, j))],
        out_specs=pl.BlockSpec((1, V7X_SUBLANES, tn), lambda l, j: (l, 0, j)),
        out_shape=jax.ShapeDtypeStruct((depth, V7X_SUBLANES, n), F32),
        compiler_params=_params("arbitrary", "arbitrary"),
        name="modulation",
    )(cvec, w_mod, b_mod.reshape(depth, 1, n))


def _pool_tile(x, band_ref, inv_ref, w_ref, b_ref, s_ref):
    xb = x.astype(BF16)
    col = lax.broadcasted_iota(jnp.int32, x.shape, 1) // POOL_GW
    tot = jnp.zeros_like(x)
    for g in range(len(POOL_WINDOWS)):
        tot = jnp.where(col == g, jnp.dot(band_ref[g], xb, preferred_element_type=F32), tot)
    pooled = tot * inv_ref[...] - x
    y = jnp.dot(pooled.astype(BF16), w_ref[...], preferred_element_type=F32)
    return (y + b_ref[...]) * s_ref[...]


def _ffn_kernel(x_ref, m_ref, g_ref, win_ref, wout_ref, fg_ref, *rest, j0, final, mix, tm, n_tiles):
    x = x_ref[0]
    if mix:
        (u0_ref, hf0_ref, ga0_ref, u_ref, hf_ref, ga_ref, wg_ref, bg_ref, lam_ref, h0_ref, up_ref, yc_ref, band_ref,
         inv_ref, pw_ref, pb_ref, ps_ref, wo_ref, o_ref, hend_ref, a_s, b_s, c_s, ya_s) = rest
        i = pl.program_id(1)

        parts = len(FF_CHUNKS)
        part_rows = tm // parts

        @pl.when(i == 0)
        def _():
            _lru_coeffs(u0_ref[0], wg_ref, bg_ref, lam_ref, a_s, b_s)
            c_s[...] = _lru_scan_tile(a_s, b_s, h0_ref[0], tm=tm, reverse=True)
            ya_s[0] = (hf0_ref[0] + b_s[...]) * jax.nn.gelu(ga0_ref[0])

        def next_tile_part(j, carry):
            rows = slice(tm - (j + 1) * part_rows, tm - j * part_rows)
            per = part_rows // V7X_SUBLANES
            _lru_coeffs(u_ref[0, rows, :], wg_ref, bg_ref, lam_ref, a_s, b_s, rows)
            carry = _lru_scan_tile(a_s, b_s, carry, tm=tm, reverse=True, inline=range(j * per, (j + 1) * per))
            ya_s[(i + 1) % 2, rows, :] = (hf_ref[0, rows, :] + b_s[rows, :]) * jax.nn.gelu(ga_ref[0, rows, :])
            return carry

        ya = ya_s[i % 2]
        yb = jnp.concatenate([_pool_tile(up_ref[0, r:r + POOL_TILE, :], band_ref, inv_ref, pw_ref, pb_ref, ps_ref)
                              for r in range(0, tm, POOL_TILE)], axis=0)
        y = jnp.dot(ya.astype(BF16), wo_ref[0:W_A, :], preferred_element_type=F32)
        y = y + jnp.dot(yb.astype(BF16), wo_ref[W_A:W_A + W_B, :], preferred_element_type=F32)
        y = y + jnp.dot(yc_ref[0].astype(BF16), wo_ref[W_A + W_B:, :], preferred_element_type=F32)
        x = x + m_ref[0, 5:6, :] * y
    else:
        (o_ref,) = rest
    shift = m_ref[0, j0:j0 + 1, :]
    scale = m_ref[0, j0 + 1:j0 + 2, :]
    gate = m_ref[0, j0 + 2:j0 + 3, :]
    h = _adanorm(x, g_ref[...], shift, scale).astype(BF16)
    acc = None
    ahead = mix and n_tiles > 1
    carry = c_s[...] if ahead else None
    for j, (s, w) in enumerate(FF_CHUNKS):
        if ahead:
            carry = next_tile_part(j, carry)
        gt = jnp.dot(h, win_ref[:, s:s + w], preferred_element_type=F32)
        up = jnp.dot(h, win_ref[:, D_FF + s:D_FF + s + w], preferred_element_type=F32)
        a = (gt * jax.nn.sigmoid(gt) * up).astype(BF16)
        p = jnp.dot(a, wout_ref[s:s + w, :], preferred_element_type=F32)
        acc = p if acc is None else acc + p
    if ahead:
        c_s[...] = jnp.where(i + 1 < n_tiles, carry, c_s[...])
    if mix:
        hend_ref[0] = c_s[...]
    y = x + (0.5 * gate) * acc
    if final:
        ms = jnp.mean(y * y, axis=-1, keepdims=True)
        y = (y * lax.rsqrt(ms + EPS)) * fg_ref[...]
    o_ref[0] = y


def _ffn(x, mod, g, w_in, w_out, wsel, final_g, mix=None, *, j0, final, tm):
    bn, n, _ = x.shape
    last = n // tm - 1
    tile_map = (lambda b, i: (b, last - i, 0)) if mix is not None else (lambda b, i: (b, i, 0))
    tile = lambda w: pl.BlockSpec((1, tm, w), tile_map)
    const = lambda shape: pl.BlockSpec(shape, lambda b, i: (0,) * len(shape))
    in_specs = [tile(D_MODEL),
                pl.BlockSpec((1, N_MOD, D_MODEL), lambda b, i: (b, 0, 0)),
                const((1, D_MODEL)),
                _resident((None, None, D_MODEL, 2 * D_FF), lambda b, i: wsel + (0, 0)),
                _resident((None, None, D_FF, D_MODEL), lambda b, i: wsel + (0, 0)),
                const((1, D_MODEL))]
    args = [x, mod, g.reshape(1, D_MODEL), w_in, w_out, final_g.reshape(1, D_MODEL)]
    x_shape = jax.ShapeDtypeStruct(x.shape, F32)
    if mix is None:
        return pl.pallas_call(
            functools.partial(_ffn_kernel, j0=j0, final=final, mix=False, tm=tm, n_tiles=n // tm),
            grid=(bn, n // tm), in_specs=in_specs, out_specs=tile(D_MODEL), out_shape=x_shape,
            compiler_params=_params("arbitrary", "arbitrary"), name="ffn",
        )(*args)
    u, wg, bg, lam, h0, hf, ga, up, yc, band, inv, pw, pb, ps, wo = mix
    first = _resident((1, tm, W_A), lambda b, i: (b, last, 0))
    nxt = pl.BlockSpec((1, tm, W_A), lambda b, i: (b, last - jnp.minimum(i + 1, last), 0))
    in_specs += ([first] * 3 + [nxt] * 3 + _lru_param_specs()
                 + [tile(W_B), tile(W_C), const(band.shape), const(inv.shape),
                    const((W_B, W_B)), const((1, W_B)), const((1, W_B)),
                    _resident((None, D_MODEL, D_MODEL), lambda b, i: (wsel[0], 0, 0))])
    args += [u, hf, ga, u, hf, ga, wg, bg, lam, h0.reshape(bn, 1, W_A), up, yc, band, inv, pw, pb.reshape(1, W_B),
             ps.reshape(1, W_B), wo]
    out, hend = pl.pallas_call(
        functools.partial(_ffn_kernel, j0=j0, final=final, mix=True, tm=tm, n_tiles=n // tm),
        grid=(bn, n // tm),
        in_specs=in_specs,
        out_specs=[tile(D_MODEL), pl.BlockSpec((1, 1, W_A), lambda b, i: (b, 0, 0))],
        out_shape=[x_shape, jax.ShapeDtypeStruct((bn, 1, W_A), F32)],
        scratch_shapes=_lru_scratch(tm) + [pltpu.VMEM((2, tm, W_A), F32)],
        compiler_params=_params("arbitrary", "arbitrary"),
        name="mix_ffn",
    )(*args)
    return out, hend.reshape(bn, W_A)


def _proj_kernel(xp_ref, x_ref, xn_ref, m_ref, g_ref, w_ref, cwa_ref, cba_ref, cwh_ref, cbh_ref,
                 wg_ref, bg_ref, lam_ref, h0_ref, ua_ref, ga_ref, up_ref, z_ref, hf_ref,
                 pa_s, ph_s, a_s, b_s, c_s, uprev_s, *, tm, n_tiles):
    i = pl.program_id(1)
    ti = jnp.minimum(i, n_tiles - 1)
    pipelined = n_tiles > 1

    @pl.when(i == 0)
    def _():
        c_s[...] = h0_ref[0]
        if pipelined:
            uprev_s[...] = jnp.zeros_like(uprev_s)

    parts = SCAN_PARTS
    part_rows = tm // parts

    def scan_part(j, carry):
        rows = slice(j * part_rows, (j + 1) * part_rows)
        per = part_rows // V7X_SUBLANES
        _lru_coeffs(uprev_s[rows, :], wg_ref, bg_ref, lam_ref, a_s, b_s, rows)
        carry = _lru_scan_tile(a_s, b_s, carry, tm=tm, reverse=False, inline=range(j * per, (j + 1) * per))
        hf_ref[0, rows, :] = b_s[rows, :]
        return carry

    norm = lambda v: _adanorm(v, g_ref[...], m_ref[0, 3:4, :], m_ref[0, 4:5, :])
    h_prev = jnp.where(ti > 0, norm(xp_ref[0]), 0.0)
    h_next = jnp.where(ti < n_tiles - 1, norm(xn_ref[0]), 0.0)
    h = jnp.concatenate([h_prev, norm(x_ref[0]), h_next], axis=0).astype(BF16)
    proj = lambda lo, hi: jnp.dot(h, w_ref[:, lo:hi], preferred_element_type=F32)

    def short_conv(lo, hi, w_conv_ref, b_conv_ref, taps, pad_left, p_s):
        p_s[...] = proj(lo, hi)
        acc = b_conv_ref[...] + w_conv_ref[0:1, :] * p_s[pl.ds(HALO - pad_left, tm), :]
        for k in range(1, taps):
            acc = acc + w_conv_ref[k:k + 1, :] * p_s[pl.ds(HALO - pad_left + k, tm), :]
        return acc

    carry = scan_part(0, c_s[...]) if pipelined else None
    z_ref[0] = short_conv(2 * W_A + W_B, D_IN, cwh_ref, cbh_ref, HY_SHORT, HY_SHORT // 2, ph_s)
    carry = scan_part(1, carry) if pipelined else None
    ua = short_conv(0, W_A, cwa_ref, cba_ref, RG_CONV, RG_CONV // 2, pa_s)
    ua_ref[0] = ua
    carry = scan_part(2, carry) if pipelined else None
    rest = proj(W_A, 2 * W_A + W_B)
    ga_ref[0] = rest[HALO:HALO + tm, :W_A]
    up_ref[0] = rest[HALO:HALO + tm, W_A:]
    if pipelined:
        carry = scan_part(3, carry)
        c_s[...] = jnp.where(i > 0, carry, c_s[...])
        uprev_s[...] = ua
    else:
        uprev_s[...] = ua
        carry = c_s[...]
        for j in range(parts):
            carry = scan_part(j, carry)


def _proj(x, mod, g, w_in, cwa, cba, cwh, cbh, wg, bg, lam, h0, *, tm, layer):
    bn, n, _ = x.shape
    n_tiles = n // tm
    hb = tm // HALO
    last_hb = n // HALO - 1
    outs = (W_A, W_A, W_B, 3 * W_C)
    cur = lambda i: jnp.minimum(i, n_tiles - 1)
    const = lambda shape: pl.BlockSpec(shape, lambda b, i: (0,) * len(shape))
    return pl.pallas_call(
        functools.partial(_proj_kernel, tm=tm, n_tiles=n_tiles),
        grid=(bn, n_tiles + 1 if n_tiles > 1 else 1),
        in_specs=[pl.BlockSpec((1, HALO, D_MODEL), lambda b, i: (b, jnp.maximum(cur(i) * hb - 1, 0), 0)),
                  pl.BlockSpec((1, tm, D_MODEL), lambda b, i: (b, cur(i), 0)),
                  pl.BlockSpec((1, HALO, D_MODEL), lambda b, i: (b, jnp.minimum((cur(i) + 1) * hb, last_hb), 0)),
                  pl.BlockSpec((1, N_MOD, D_MODEL), lambda b, i: (b, 0, 0)),
                  const((1, D_MODEL)),
                  _resident((None, D_MODEL, D_IN), lambda b, i: (layer, 0, 0)),
                  const((RG_CONV, W_A)), const((1, W_A)), const((HY_SHORT, 3 * W_C)), const((1, 3 * W_C))]
                 + _lru_param_specs(),
        out_specs=[pl.BlockSpec((1, tm, w), lambda b, i: (b, cur(i), 0)) for w in outs]
                  + [pl.BlockSpec((1, tm, W_A), lambda b, i: (b, jnp.maximum(i - 1, 0), 0))],
        out_shape=[jax.ShapeDtypeStruct((bn, n, w), F32) for w in outs + (W_A,)],
        scratch_shapes=[pltpu.VMEM((tm + 2 * HALO, W_A), F32), pltpu.VMEM((tm + 2 * HALO, 3 * W_C), F32)]
                       + _lru_scratch(tm) + [pltpu.VMEM((tm, W_A), F32)],
        compiler_params=_params("arbitrary", "arbitrary"),
        name="proj_scan",
    )(x, x, x, mod, g.reshape(1, D_MODEL), w_in, cwa, cba.reshape(1, W_A), cwh, cbh.reshape(1, 3 * W_C),
      wg, bg, lam, h0.reshape(bn, 1, W_A))


def _lru_coeffs(u, wg_ref, bg_ref, lam_ref, a_s, b_s, rows=slice(None)):
    ub = u.astype(BF16)
    blk = V7X_MXU_DIM

    def gate_block(c):
        r0 = (c * blk) % W_A
        return jnp.dot(ub[:, r0:r0 + blk], wg_ref[r0:r0 + blk, c * blk:(c + 1) * blk], preferred_element_type=F32)

    half = jnp.concatenate([gate_block(c) for c in range(2 * W_A // blk)], axis=1) + bg_ref[...]
    q = (-0.25 * RG_C) * jax.nn.softplus(-lam_ref[...])
    t = jnp.tanh(q * jnp.tanh(half[:, :W_A]) + q)
    inv = 1.0 / (1.0 - t)
    a_s[rows, :] = (1.0 + t) * inv
    b_s[rows, :] = (jnp.sqrt(-t) * inv) * ((jnp.tanh(half[:, W_A:]) + 1.0) * u)


def _lru_scan_tile(a_s, b_s, carry, *, tm, reverse, inline=None):
    groups = tm // V7X_SUBLANES
    rowi = lax.broadcasted_iota(jnp.int32, (V7X_SUBLANES, W_A), 0)

    def body(k, c):
        gi = (groups - 1 - k) if reverse else k
        r0 = gi * V7X_SUBLANES if inline is not None else pl.multiple_of(gi * V7X_SUBLANES, V7X_SUBLANES)
        a = a_s[pl.ds(r0, V7X_SUBLANES), :]
        b = b_s[pl.ds(r0, V7X_SUBLANES), :]
        first = rowi == (V7X_SUBLANES - 1 if reverse else 0)
        b = jnp.where(first, a * c + b, b)
        for s in (1, 2, 4):
            if reverse:
                shift, m = V7X_SUBLANES - s, rowi < V7X_SUBLANES - s
            else:
                shift, m = s, rowi >= s
            b = jnp.where(m, a * pltpu.roll(b, shift, 0) + b, b)
            if s < V7X_SUBLANES // 2:
                a = jnp.where(m, a * pltpu.roll(a, shift, 0), a)
        b_s[pl.ds(r0, V7X_SUBLANES), :] = b
        return b[0:1, :] if reverse else b[V7X_SUBLANES - 1:V7X_SUBLANES, :]

    if inline is not None:
        for k in inline:
            carry = body(k, carry)
        return carry
    return lax.fori_loop(0, groups, body, carry, unroll=4)


def _lru_scan_kernel(u_ref, wg_ref, bg_ref, lam_ref, h0_ref, h_ref, a_s, b_s, c_s, *, tm, reverse):
    @pl.when(pl.program_id(1) == 0)
    def _():
        c_s[...] = h0_ref[0]

    _lru_coeffs(u_ref[0], wg_ref, bg_ref, lam_ref, a_s, b_s)
    c_s[...] = _lru_scan_tile(a_s, b_s, c_s[...], tm=tm, reverse=reverse)
    h_ref[0] = b_s[...]


def _lru_param_specs():
    const = lambda shape: pl.BlockSpec(shape, lambda b, i: (0,) * len(shape))
    return [const((W_A, 2 * W_A)), const((1, 2 * W_A)), const((1, W_A)),
            pl.BlockSpec((1, 1, W_A), lambda b, i: (b, 0, 0))]


def _lru_scratch(tm):
    return [pltpu.VMEM((tm, W_A), F32), pltpu.VMEM((tm, W_A), F32), pltpu.VMEM((1, W_A), F32)]


def _lru_scan(u, wg, bg, lam, h0, *, tm, reverse):
    bn, n, _ = u.shape
    last = n // tm - 1
    tile_map = (lambda b, i: (b, last - i, 0)) if reverse else (lambda b, i: (b, i, 0))
    return pl.pallas_call(
        functools.partial(_lru_scan_kernel, tm=tm, reverse=reverse),
        grid=(bn, n // tm),
        in_specs=[pl.BlockSpec((1, tm, W_A), tile_map)] + _lru_param_specs(),
        out_specs=pl.BlockSpec((1, tm, W_A), tile_map),
        out_shape=jax.ShapeDtypeStruct(u.shape, F32),
        scratch_shapes=_lru_scratch(tm),
        compiler_params=_params("arbitrary", "arbitrary"),
        name="lru_scan",
    )(u, wg, bg, lam, h0.reshape(bn, 1, W_A))


def _pool_tables(row_len):
    t = np.arange(POOL_TILE)
    p = t % row_len
    base = t - p
    band = np.zeros((len(POOL_WINDOWS), POOL_TILE, POOL_TILE), np.float32)
    inv = np.zeros((POOL_TILE, W_B), np.float32)
    for g, win in enumerate(POOL_WINDOWS):
        lo = np.clip(p - win // 2, 0, row_len)
        hi = np.clip(p + win - win // 2, 0, row_len)
        s = np.arange(POOL_TILE)[None, :]
        band[g] = ((s >= (base + lo)[:, None]) & (s < (base + hi)[:, None])).astype(np.float32)
        inv[:, g * POOL_GW:(g + 1) * POOL_GW] = (1.0 / (hi - lo).astype(np.float64))[:, None]
    return _bf16_table(band), jnp.asarray(inv, F32)


def _filter_kernel(f_ref, w1_ref, b1_ref, fr_ref, w2_ref, b2_ref, w3_ref, dl_ref, o_ref):
    feats = f_ref[...]
    fr = fr_ref[...]
    hid = jnp.sin(fr * (_dot_bf16x3(feats, w1_ref[...]) + b1_ref[...]))
    hid = jnp.sin(fr * (_dot_bf16x3(hid, w2_ref[...]) + b2_ref[...]))
    filt = _dot_bf16x3(hid, w3_ref[...])
    filt = filt * jnp.exp(-feats[:, 0:1] * dl_ref[...])
    o_ref[...] = filt

    @pl.when(pl.program_id(0) == 0)
    def _():
        head = filt[0:V7X_SUBLANES, :]
        row = lax.broadcasted_iota(jnp.int32, head.shape, 0)
        bwd_col = (lax.broadcasted_iota(jnp.int32, head.shape, 1) // W_C) % 2 == 1
        o_ref[0:V7X_SUBLANES, :] = jnp.where((row == 0) & bwd_col, 0.0, head)


def _filter_features(n):
    t = jnp.linspace(0.0, 1.0, n, dtype=F32)[:, None]
    wpos = 2.0 * math.pi * jnp.arange(n, dtype=F32)[:, None] / n
    bands = jnp.linspace(1e-4, HY_BANDS - 1, HY_BANDS, dtype=F32)[None, :]
    feats = jnp.concatenate([t, jnp.cos(bands * wpos), -jnp.sin(bands * wpos)], axis=-1)
    return jnp.pad(feats, ((0, 0), (0, HY_EMB_PAD - HY_EMB)))


def _filter_decay_rates():
    deltas = jnp.abs(jnp.linspace(math.log(HY_TARGET) / HY_FAST, math.log(HY_TARGET) / HY_SLOW, W_C, dtype=F32))
    return jnp.tile(deltas, 2 * HY_ORDER).reshape(1, 2 * HY_ORDER * W_C)


def _filters(n, w1, b1, freq, w2, b2, w3):
    tm = min(n, 512)
    nf = 2 * HY_ORDER * W_C
    const = lambda shape: pl.BlockSpec(shape, lambda i: (0, 0))
    return pl.pallas_call(
        _filter_kernel,
        grid=(n // tm,),
        in_specs=[pl.BlockSpec((tm, HY_EMB_PAD), lambda i: (i, 0)),
                  const((HY_EMB_PAD, HY_FH)), const((1, HY_FH)), const((1, HY_FH)),
                  const((HY_FH, HY_FH)), const((1, HY_FH)), const((HY_FH, nf)), const((1, nf))],
        out_specs=pl.BlockSpec((tm, nf), lambda i: (i, 0)),
        out_shape=jax.ShapeDtypeStruct((n, nf), F32),
        compiler_params=_params("arbitrary"),
        name="hyena_filter",
    )(_filter_features(n), jnp.pad(w1, ((0, HY_EMB_PAD - HY_EMB), (0, 0))), b1.reshape(1, HY_FH),
      freq.reshape(1, HY_FH), w2, b2.reshape(1, HY_FH), w3, _filter_decay_rates())


T2_BLOCKS = FFT_MINOR // V7X_SUBLANES


class _FftPlan:
    def __init__(self, n):
        assert n % (2 * FFT_MINOR) == 0
        self.n_fft = 2 * n
        self.n1 = self.n_fft // FFT_MINOR
        self.t1 = self.n1 // 2
        self.k1 = self.n1 // 2 + 1
        self.k_block = max(d for d in range(1, STAGE_B_MAX_SLABS + 1) if self.k1 % d == 0)
        t1 = np.arange(self.t1)
        k1 = np.arange(self.k1)
        eye = np.eye(V7X_SUBLANES)
        ang = 2.0 * np.pi * ((k1[:, None] * t1[None, :]) % self.n1) / self.n1
        fa = np.concatenate([np.cos(ang), -np.sin(ang)], axis=0)
        self.stage_a = _bf16_table(np.kron(fa, eye))
        wgt = np.where((k1 == 0) | (k1 == self.n1 // 2), 1.0, 2.0) / self.n_fft
        fc = np.concatenate([np.cos(ang) * wgt[:, None], -np.sin(ang) * wgt[:, None]], axis=0).T
        self.stage_a_inv = _bf16_table(np.kron(fc, eye))
        t2 = np.arange(FFT_MINOR)
        k2 = np.arange(FFT_MINOR)
        idx = (t2[None, None, :] * (k1[:, None, None] + self.n1 * k2[None, :, None])) % self.n_fft
        ang_b = 2.0 * np.pi * idx / self.n_fft
        g = np.stack([np.cos(ang_b), -np.sin(ang_b)], axis=1)
        self.stage_b = _bf16_table(g)
        self.stage_b_inv = _bf16_table(np.transpose(g, (0, 1, 3, 2)))

    def scratch(self, signals):
        return pltpu.VMEM((signals, 2, self.k1, T2_BLOCKS, V7X_SUBLANES, V7X_LANES), F32)


def _time_tiles(ref, tb, rows):
    return ref[:, pl.ds(tb, 1)].reshape(rows, V7X_LANES)


def _stage_a_forward(plan, x_refs, fa_ref, a_s):
    def body(tb, carry):
        xs = jnp.concatenate([_time_tiles(xr, tb, plan.t1 * V7X_SUBLANES) for xr in x_refs], axis=1)
        av = jnp.dot(fa_ref[...], xs.astype(BF16), preferred_element_type=F32)
        for j in range(len(x_refs)):
            part = av[:, j * V7X_LANES:(j + 1) * V7X_LANES]
            a_s[j, :, :, pl.ds(tb, 1)] = part.reshape(2, plan.k1, 1, V7X_SUBLANES, V7X_LANES)
        return carry
    lax.fori_loop(0, T2_BLOCKS, body, 0, unroll=DFT_LOOP_UNROLL)


def _stage_b_matrix(g_ref, i, inverse):
    gr, gi = g_ref[i, 0], g_ref[i, 1]
    if inverse:
        gi = -gi
    return jnp.concatenate([jnp.concatenate([gr, -gi], axis=1), jnp.concatenate([gi, gr], axis=1)], axis=0)


def _stage_b_slab(a_s, k, count):
    return jnp.concatenate(
        [a_s[j, :, pl.ds(k, 1)].reshape(2 * FFT_MINOR, V7X_LANES) for j in range(count)], axis=1)


def _spectrum_kernel(x0_ref, x1_ref, fa_ref, fb_ref, o_ref, a_s, *, plan):
    kb = pl.program_id(1)

    @pl.when(kb == 0)
    def _():
        _stage_a_forward(plan, (x0_ref, x1_ref), fa_ref, a_s)

    half = FFT_MINOR
    rows = 2 * half
    for i in range(plan.k_block):
        slab = _stage_b_slab(a_s, kb * plan.k_block + i, 2).astype(BF16)
        xs = jnp.dot(_stage_b_matrix(fb_ref, i, False), slab, preferred_element_type=F32)
        o_ref[i * rows:i * rows + half, :] = xs[:half, :V7X_LANES] + xs[:half, V7X_LANES:]
        o_ref[i * rows + half:(i + 1) * rows, :] = xs[half:, :V7X_LANES] - xs[half:, V7X_LANES:]


def _filter_spectrum(plan, filt):
    n, nf = filt.shape
    rows = 2 * FFT_MINOR
    kbs = plan.k_block
    n_cb = W_C // V7X_LANES
    filt = filt.reshape(plan.t1, T2_BLOCKS, V7X_SUBLANES, nf)
    sig = (plan.t1, T2_BLOCKS, V7X_SUBLANES, V7X_LANES)
    return pl.pallas_call(
        functools.partial(_spectrum_kernel, plan=plan),
        grid=(HY_ORDER * n_cb, plan.k1 // kbs),
        in_specs=[_resident(sig, lambda c, k: (0, 0, 0, 2 * (c // n_cb) * n_cb + c % n_cb)),
                  _resident(sig, lambda c, k: (0, 0, 0, (2 * (c // n_cb) + 1) * n_cb + c % n_cb)),
                  pl.BlockSpec(plan.stage_a.shape, lambda c, k: (0, 0)),
                  pl.BlockSpec((kbs, 2, FFT_MINOR, FFT_MINOR), lambda c, k: (k, 0, 0, 0))],
        out_specs=pl.BlockSpec((kbs * rows, V7X_LANES), lambda c, k: (k, c)),
        out_shape=jax.ShapeDtypeStruct((plan.k1 * rows, HY_ORDER * W_C), F32),
        scratch_shapes=[plan.scratch(2)],
        compiler_params=_params("arbitrary", "arbitrary"),
        name="filter_spectrum",
    )(filt, filt, plan.stage_a, plan.stage_b)


def _conv_kernel(v_ref, g_ref, h_ref, fa_ref, fb_ref, fbi_ref, fai_ref, sk_ref, o_ref, a_s, *, plan, bn):
    kb = pl.program_id(1)
    half = FFT_MINOR
    rows = 2 * half

    @pl.when(kb == 0)
    def _():
        _stage_a_forward(plan, [v_ref.at[b] for b in range(bn)], fa_ref, a_s)

    for i in range(plan.k_block):
        k = kb * plan.k_block + i
        xs = jnp.dot(_stage_b_matrix(fb_ref, i, False), _stage_b_slab(a_s, k, bn).astype(BF16),
                     preferred_element_type=F32)
        xr, xi = xs[:half], xs[half:]
        hr = jnp.concatenate([h_ref[i * rows:i * rows + half, :]] * bn, axis=1)
        hi = jnp.concatenate([h_ref[i * rows + half:(i + 1) * rows, :]] * bn, axis=1)
        ys = jnp.concatenate([xr * hr - xi * hi, xr * hi + xi * hr], axis=0)
        ds = jnp.dot(_stage_b_matrix(fbi_ref, i, True), ys.astype(BF16), preferred_element_type=F32)
        for b in range(bn):
            part = ds[:, b * V7X_LANES:(b + 1) * V7X_LANES]
            a_s[b, :, pl.ds(k, 1)] = part.reshape(2, 1, T2_BLOCKS, V7X_SUBLANES, V7X_LANES)

    @pl.when(kb == pl.num_programs(1) - 1)
    def _():
        t_rows = plan.t1 * V7X_SUBLANES

        def body(tb, carry):
            ds = jnp.concatenate(
                [a_s[b, :, :, pl.ds(tb, 1)].reshape(2 * plan.k1 * V7X_SUBLANES, V7X_LANES) for b in range(bn)],
                axis=1)
            y = jnp.dot(fai_ref[...], ds.astype(BF16), preferred_element_type=F32)
            for b in range(bn):
                vv = _time_tiles(v_ref.at[b], tb, t_rows)
                gg = _time_tiles(g_ref.at[b], tb, t_rows)
                res = gg * (y[:, b * V7X_LANES:(b + 1) * V7X_LANES] + sk_ref[...] * vv)
                o_ref[b, :, pl.ds(tb, 1)] = res.reshape(plan.t1, 1, V7X_SUBLANES, V7X_LANES)
            return carry
        lax.fori_loop(0, T2_BLOCKS, body, 0, unroll=DFT_LOOP_UNROLL)


def _hyena_conv(plan, v_arr, v_col, g_arr, g_col, spec, order, skip):
    bn = v_arr.shape[0]
    cb = V7X_LANES
    rows = 2 * FFT_MINOR
    n_cb = W_C // cb
    kbs = plan.k_block
    sig = (bn, plan.t1, T2_BLOCKS, V7X_SUBLANES, cb)
    return pl.pallas_call(
        functools.partial(_conv_kernel, plan=plan, bn=bn),
        grid=(n_cb, plan.k1 // kbs),
        in_specs=[_resident(sig, lambda c, k: (0, 0, 0, 0, v_col + c)),
                  _resident(sig, lambda c, k: (0, 0, 0, 0, g_col + c)),
                  pl.BlockSpec((kbs * rows, cb), lambda c, k: (k, order * n_cb + c)),
                  pl.BlockSpec(plan.stage_a.shape, lambda c, k: (0, 0)),
                  pl.BlockSpec((kbs, 2, FFT_MINOR, FFT_MINOR), lambda c, k: (k, 0, 0, 0)),
                  pl.BlockSpec((kbs, 2, FFT_MINOR, FFT_MINOR), lambda c, k: (k, 0, 0, 0)),
                  pl.BlockSpec(plan.stage_a_inv.shape, lambda c, k: (0, 0)),
                  pl.BlockSpec((1, cb), lambda c, k: (0, c))],
        out_specs=_resident(sig, lambda c, k: (0, 0, 0, 0, c)),
        out_shape=jax.ShapeDtypeStruct((bn, plan.t1, T2_BLOCKS, V7X_SUBLANES, W_C), F32),
        scratch_shapes=[plan.scratch(bn)],
        compiler_params=_params("arbitrary", "arbitrary"),
        name="hyena_conv",
    )(v_arr, g_arr, spec, plan.stage_a, plan.stage_b, plan.stage_b_inv, plan.stage_a_inv,
      skip.reshape(1, W_C))


def _hyena(z, n_fft_len, filt_args, skip):
    bn, n, _ = z.shape
    plan = _FftPlan(n_fft_len)
    filt = _filters(n, *filt_args)
    if n_fft_len != n:
        filt = jnp.pad(filt, ((0, n_fft_len - n), (0, 0)))
        z = jnp.pad(z, ((0, 0), (0, n_fft_len - n), (0, 0)))
    spec = _filter_spectrum(plan, filt)
    n_cb = W_C // V7X_LANES
    z = z.reshape(bn, plan.t1, T2_BLOCKS, V7X_SUBLANES, 3 * W_C)
    y1 = _hyena_conv(plan, z, 0, z, n_cb, spec, 0, skip[0])
    y2 = _hyena_conv(plan, y1, 0, z, 2 * n_cb, spec, 1, skip[1])
    return y2.reshape(bn, n_fft_len, W_C)[:, :n]


def _dense_block_diag(w):
    h, c, d = w.shape[-3:]
    tiled = jnp.tile(w.reshape(w.shape[:-3] + (h * c, d)), (1,) * (w.ndim - 2) + (h,))
    on_diagonal = (jnp.arange(h * c)[:, None] // c) == (jnp.arange(h * d)[None, :] // d)
    return jnp.where(on_diagonal, tiled, jnp.zeros_like(tiled))


def _tile_rows(n):
    return min(n, 512)


def kernel(x, c, ctx, c_ctx, w_mod, b_mod, norm_g, ffn_w_in, ffn_w_out, w_in, w_out, lru_conv_w, lru_conv_b,
           lru_gate_w, lru_gate_b, lru_lambda, pool_w, pool_b, pool_scale, hy_conv_w, hy_conv_b, hy_w1, hy_b1,
           hy_freq, hy_w2, hy_b2, hy_w3, hy_skip, final_g):
    bn, n, _ = x.shape
    n_ctx = ctx.shape[1]
    depth = w_mod.shape[0]
    assert bn + 1 <= V7X_SUBLANES and n % GRID_W == 0 and n % POOL_TILE == 0 and n_ctx == POOL_TILE
    assert RG_HEADS * RG_HD == W_A and V7X_MXU_DIM % RG_HD == 0
    tm_x, tm_c = _tile_rows(n), _tile_rows(n_ctx)

    cvec = jnp.zeros((V7X_SUBLANES, D_MODEL), F32).at[:bn].set(c).at[bn].set(c_ctx)
    mods = _modulation(cvec, w_mod, b_mod)
    band_x, inv_x = _pool_tables(GRID_W)
    band_c, inv_c = _pool_tables(n_ctx)
    zeros = jnp.zeros((bn, W_A), F32)
    f_in = ffn_w_in.astype(BF16)
    f_out = ffn_w_out.astype(BF16)
    w_in_b = w_in.astype(BF16)
    w_out_b = w_out.astype(BF16)

    xc = ctx
    for l in range(depth):
        last = l == depth - 1
        mx = mods[l, :bn].reshape(bn, N_MOD, D_MODEL)
        mc = jnp.broadcast_to(mods[l, bn].reshape(1, N_MOD, D_MODEL), (bn, N_MOD, D_MODEL))
        gate_dense = _dense_block_diag(lru_gate_w[l])
        wg = [(0.5 * jnp.concatenate([gate_dense[d, 0], gate_dense[d, 1]], axis=-1)).astype(BF16) for d in range(2)]
        bg = [0.5 * lru_gate_b[l, d].reshape(1, 2 * W_A) for d in range(2)]
        lam = [lru_lambda[l, d].reshape(1, W_A) for d in range(2)]
        pw = _dense_block_diag(pool_w[l]).astype(BF16)
        filt_args = (hy_w1[l], hy_b1[l], hy_freq[l], hy_w2[l], hy_b2[l], hy_w3[l])
        proj_args = (w_in_b, lru_conv_w[l], lru_conv_b[l], hy_conv_w[l], hy_conv_b[l])

        def mixer(xs, mod, tm, h0f, band, inv, fft_len):
            ua, ga, up, z, hf = _proj(xs, mod, norm_g[l, 1], *proj_args, wg[0], bg[0], lam[0], h0f, tm=tm, layer=l)
            yc = _hyena(z, fft_len, filt_args, hy_skip[l])
            return lambda h0b: (ua, wg[1], bg[1], lam[1], h0b, hf, ga, up, yc, band, inv, pw, pool_b[l],
                                pool_scale[l], w_out_b), hf[:, -1]

        x = _ffn(x, mx, norm_g[l, 0], f_in, f_out, (l, 0), final_g, j0=0, final=False, tm=min(n, FFN_TILE))
        xc = _ffn(xc, mc, norm_g[l, 0], f_in, f_out, (l, 0), final_g, j0=0, final=False, tm=tm_c)
        if last:
            ua_c, _, _, _, hf_c = _proj(xc, mc, norm_g[l, 1], *proj_args, wg[0], bg[0], lam[0], zeros, tm=tm_c, layer=l)
            h0f = hf_c[:, -1]
            h0b = _lru_scan(ua_c, wg[1], bg[1], lam[1], zeros, tm=tm_c, reverse=True)[:, 0]
        else:
            mix_c, h0f = mixer(xc, mc, tm_c, zeros, band_c, inv_c, n_ctx)
            xc, h0b = _ffn(xc, mc, norm_g[l, 2], f_in, f_out, (l, 1), final_g, mix_c(zeros), j0=6, final=False,
                           tm=tm_c)
        mix_x, _ = mixer(x, mx, tm_x, h0f, band_x, inv_x, n)
        x, _ = _ffn(x, mx, norm_g[l, 2], f_in, f_out, (l, 1), final_g, mix_x(h0b), j0=6, final=last, tm=tm_x)
    return x
```

```python
import functools
import math

import numpy as np
import jax
import jax.numpy as jnp
from jax import lax
from jax.experimental import pallas as pl
from jax.experimental.pallas import tpu as pltpu

F32 = jnp.float32
BF16 = jnp.bfloat16

D_MODEL = 1024
GRID_W = 64
N_MOD = 9
W_A, W_B, W_C = 512, 256, 256
D_IN = 2 * W_A + W_B + 3 * W_C
RG_HEADS, RG_HD, RG_CONV, RG_C = 8, 64, 4, 8.0
POOL_WINDOWS = (2, 4, 8, 16)
POOL_GW = W_B // len(POOL_WINDOWS)
HY_ORDER, HY_SHORT, HY_BANDS, HY_FH = 2, 3, 16, 64
HY_EMB = 2 * HY_BANDS + 1
HY_EMB_PAD = 40
HY_TARGET, HY_FAST, HY_SLOW = 1e-2, 0.3, 1.5
D_FF = 2816
EPS = 1e-6

V7X_LANES = 128
V7X_SUBLANES = 8
V7X_MXU_DIM = 256
V7X_VMEM_LIMIT_BYTES = 56 * 1024 * 1024

FFT_MINOR = V7X_LANES
HALO = V7X_SUBLANES
FF_CHUNKS = ((0, 768), (768, 768), (1536, 768), (2304, 512))
POOL_TILE = 256
SCAN_PARTS = 4
STAGE_B_MAX_SLABS = 13
DFT_LOOP_UNROLL = 8
FFN_TILE = 1024
FILTER_TILE = 1024


def _params(*sem):
    return pltpu.CompilerParams(dimension_semantics=sem, vmem_limit_bytes=V7X_VMEM_LIMIT_BYTES)


def _resident(shape, index_map):
    return pl.BlockSpec(shape, index_map, pipeline_mode=pl.Buffered(1))


def _bf16_table(values):
    return jnp.asarray(values, F32).astype(BF16)


def _split_bf16(x):
    hi = x.astype(BF16)
    return hi, (x - hi.astype(F32)).astype(BF16)


def _dot_bf16x3(a, b):
    a_hi, a_lo = _split_bf16(a)
    b_hi, b_lo = _split_bf16(b)
    d = lambda p, q: jnp.dot(p, q, preferred_element_type=F32)
    return d(a_hi, b_hi) + (d(a_lo, b_hi) + d(a_hi, b_lo))


def _adanorm(x, g, shift, scale):
    ms = jnp.mean(x * x, axis=-1, keepdims=True)
    return (x * lax.rsqrt(ms + EPS)) * g * (1.0 + scale) + shift


def _mod_kernel(c_ref, w_ref, b_ref, o_ref):
    s = c_ref[...]
    s = s * jax.nn.sigmoid(s)
    o_ref[0] = jnp.dot(s.astype(BF16), w_ref[0].astype(BF16), preferred_element_type=F32) + b_ref[0]


def _modulation(cvec, w_mod, b_mod):
    depth, _, n = w_mod.shape
    tn = n // 4
    return pl.pallas_call(
        _mod_kernel,
        grid=(depth, n // tn),
        in_specs=[pl.BlockSpec((V7X_SUBLANES, D_MODEL), lambda l, j: (0, 0)),
                  pl.BlockSpec((1, D_MODEL, tn), lambda l, j: (l, 0, j)),
                  pl.BlockSpec((1, 1, tn), lambda l, j: (l, 0, j))],
        out_specs=pl.BlockSpec((1, V7X_SUBLANES, tn), lambda l, j: (l, 0, j)),
        out_shape=jax.ShapeDtypeStruct((depth, V7X_SUBLANES, n), F32),
        compiler_params=_params("arbitrary", "arbitrary"),
        name="modulation",
    )(cvec, w_mod, b_mod.reshape(depth, 1, n))


def _pool_tile(x, band_ref, inv_ref, w_ref, b_ref, s_ref):
    xb = x.astype(BF16)
    col = lax.broadcasted_iota(jnp.int32, x.shape, 1) // POOL_GW
    tot = jnp.zeros_like(x)
    for g in range(len(POOL_WINDOWS)):
        tot = jnp.where(col == g, jnp.dot(band_ref[g], xb, preferred_element_type=F32), tot)
    pooled = tot * inv_ref[...] - x
    y = jnp.dot(pooled.astype(BF16), w_ref[...], preferred_element_type=F32)
    return (y + b_ref[...]) * s_ref[...]


def _ffn_kernel(x_ref, m_ref, g_ref, win_ref, wout_ref, fg_ref, *rest, j0, final, mix, tm, n_tiles):
    x = x_ref[0]
    if mix:
        (u0_ref, hf0_ref, ga0_ref, u_ref, hf_ref, ga_ref, wg_ref, bg_ref, lam_ref, h0_ref, up_ref, yc_ref, band_ref,
         inv_ref, pw_ref, pb_ref, ps_ref, wo_ref, o_ref, hend_ref, a_s, b_s, c_s, ya_s) = rest
        i = pl.program_id(1)

        parts = len(FF_CHUNKS)
        part_rows = tm // parts

        @pl.when(i == 0)
        def _():
            _lru_coeffs(u0_ref[0], wg_ref, bg_ref, lam_ref, a_s, b_s)
            c_s[...] = _lru_scan_tile(a_s, b_s, h0_ref[0], tm=tm, reverse=True)
            ya_s[0] = (hf0_ref[0] + b_s[...]) * jax.nn.gelu(ga0_ref[0])

        def next_tile_part(j, carry):
            rows = slice(tm - (j + 1) * part_rows, tm - j * part_rows)
            per = part_rows // V7X_SUBLANES
            _lru_coeffs(u_ref[0, rows, :], wg_ref, bg_ref, lam_ref, a_s, b_s, rows)
            carry = _lru_scan_tile(a_s, b_s, carry, tm=tm, reverse=True, inline=range(j * per, (j + 1) * per))
            ya_s[(i + 1) % 2, rows, :] = (hf_ref[0, rows, :] + b_s[rows, :]) * jax.nn.gelu(ga_ref[0, rows, :])
            return carry

        ya = ya_s[i % 2]
        yb = jnp.concatenate([_pool_tile(up_ref[0, r:r + POOL_TILE, :], band_ref, inv_ref, pw_ref, pb_ref, ps_ref)
                              for r in range(0, tm, POOL_TILE)], axis=0)
        y = jnp.dot(ya.astype(BF16), wo_ref[0:W_A, :], preferred_element_type=F32)
        y = y + jnp.dot(yb.astype(BF16), wo_ref[W_A:W_A + W_B, :], preferred_element_type=F32)
        y = y + jnp.dot(yc_ref[0].astype(BF16), wo_ref[W_A + W_B:, :], preferred_element_type=F32)
        x = x + m_ref[0, 5:6, :] * y
    else:
        (o_ref,) = rest
    shift = m_ref[0, j0:j0 + 1, :]
    scale = m_ref[0, j0 + 1:j0 + 2, :]
    gate = m_ref[0, j0 + 2:j0 + 3, :]
    h = _adanorm(x, g_ref[...], shift, scale).astype(BF16)
    acc = None
    ahead = mix and n_tiles > 1
    carry = c_s[...] if ahead else None
    for j, (s, w) in enumerate(FF_CHUNKS):
        if ahead:
            carry = next_tile_part(j, carry)
        gt = jnp.dot(h, win_ref[:, s:s + w], preferred_element_type=F32)
        up = jnp.dot(h, win_ref[:, D_FF + s:D_FF + s + w], preferred_element_type=F32)
        a = (gt * jax.nn.sigmoid(gt) * up).astype(BF16)
        p = jnp.dot(a, wout_ref[s:s + w, :], preferred_element_type=F32)
        acc = p if acc is None else acc + p
    if ahead:
        c_s[...] = jnp.where(i + 1 < n_tiles, carry, c_s[...])
    if mix:
        hend_ref[0] = c_s[...]
    y = x + (0.5 * gate) * acc
    if final:
        ms = jnp.mean(y * y, axis=-1, keepdims=True)
        y = (y * lax.rsqrt(ms + EPS)) * fg_ref[...]
    o_ref[0] = y


def _ffn(x, mod, g, w_in, w_out, wsel, final_g, mix=None, *, j0, final, tm):
    bn, n, _ = x.shape
    last = n // tm - 1
    tile_map = (lambda b, i: (b, last - i, 0)) if mix is not None else (lambda b, i: (b, i, 0))
    tile = lambda w: pl.BlockSpec((1, tm, w), tile_map)
    const = lambda shape: pl.BlockSpec(shape, lambda b, i: (0,) * len(shape))
    in_specs = [tile(D_MODEL),
                pl.BlockSpec((1, N_MOD, D_MODEL), lambda b, i: (b, 0, 0)),
                const((1, D_MODEL)),
                _resident((None, None, D_MODEL, 2 * D_FF), lambda b, i: wsel + (0, 0)),
                _resident((None, None, D_FF, D_MODEL), lambda b, i: wsel + (0, 0)),
                const((1, D_MODEL))]
    args = [x, mod, g.reshape(1, D_MODEL), w_in, w_out, final_g.reshape(1, D_MODEL)]
    x_shape = jax.ShapeDtypeStruct(x.shape, F32)
    if mix is None:
        return pl.pallas_call(
            functools.partial(_ffn_kernel, j0=j0, final=final, mix=False, tm=tm, n_tiles=n // tm),
            grid=(bn, n // tm), in_specs=in_specs, out_specs=tile(D_MODEL), out_shape=x_shape,
            compiler_params=_params("arbitrary", "arbitrary"), name="ffn",
        )(*args)
    u, wg, bg, lam, h0, hf, ga, up, yc, band, inv, pw, pb, ps, wo = mix
    first = _resident((1, tm, W_A), lambda b, i: (b, last, 0))
    nxt = pl.BlockSpec((1, tm, W_A), lambda b, i: (b, last - jnp.minimum(i + 1, last), 0))
    in_specs += ([first] * 3 + [nxt] * 3 + _lru_param_specs()
                 + [tile(W_B), tile(W_C), const(band.shape), const(inv.shape),
                    const((W_B, W_B)), const((1, W_B)), const((1, W_B)),
                    _resident((None, D_MODEL, D_MODEL), lambda b, i: (wsel[0], 0, 0))])
    args += [u, hf, ga, u, hf, ga, wg, bg, lam, h0.reshape(bn, 1, W_A), up, yc, band, inv, pw, pb.reshape(1, W_B),
             ps.reshape(1, W_B), wo]
    out, hend = pl.pallas_call(
        functools.partial(_ffn_kernel, j0=j0, final=final, mix=True, tm=tm, n_tiles=n // tm),
        grid=(bn, n // tm),
        in_specs=in_specs,
        out_specs=[tile(D_MODEL), pl.BlockSpec((1, 1, W_A), lambda b, i: (b, 0, 0))],
        out_shape=[x_shape, jax.ShapeDtypeStruct((bn, 1, W_A), F32)],
        scratch_shapes=_lru_scratch(tm) + [pltpu.VMEM((2, tm, W_A), F32)],
        compiler_params=_params("arbitrary", "arbitrary"),
        name="mix_ffn",
    )(*args)
    return out, hend.reshape(bn, W_A)


def _proj_kernel(xp_ref, x_ref, xn_ref, m_ref, g_ref, w_ref, cwa_ref, cba_ref, cwh_ref, cbh_ref,
                 wg_ref, bg_ref, lam_ref, h0_ref, ua_ref, ga_ref, up_ref, z_ref, hf_ref,
                 pa_s, ph_s, a_s, b_s, c_s, uprev_s, *, tm, n_tiles):
    i = pl.program_id(1)
    ti = jnp.minimum(i, n_tiles - 1)
    pipelined = n_tiles > 1

    @pl.when(i == 0)
    def _():
        c_s[...] = h0_ref[0]
        if pipelined:
            uprev_s[...] = jnp.zeros_like(uprev_s)

    parts = SCAN_PARTS
    part_rows = tm // parts

    def scan_part(j, carry):
        rows = slice(j * part_rows, (j + 1) * part_rows)
        per = part_rows // V7X_SUBLANES
        _lru_coeffs(uprev_s[rows, :], wg_ref, bg_ref, lam_ref, a_s, b_s, rows)
        carry = _lru_scan_tile(a_s, b_s, carry, tm=tm, reverse=False, inline=range(j * per, (j + 1) * per))
        hf_ref[0, rows, :] = b_s[rows, :]
        return carry

    norm = lambda v: _adanorm(v, g_ref[...], m_ref[0, 3:4, :], m_ref[0, 4:5, :])
    h_prev = jnp.where(ti > 0, norm(xp_ref[0]), 0.0)
    h_next = jnp.where(ti < n_tiles - 1, norm(xn_ref[0]), 0.0)
    h = jnp.concatenate([h_prev, norm(x_ref[0]), h_next], axis=0).astype(BF16)
    proj = lambda lo, hi: jnp.dot(h, w_ref[:, lo:hi], preferred_element_type=F32)

    def short_conv(lo, hi, w_conv_ref, b_conv_ref, taps, pad_left, p_s):
        p_s[...] = proj(lo, hi)
        acc = b_conv_ref[...] + w_conv_ref[0:1, :] * p_s[pl.ds(HALO - pad_left, tm), :]
        for k in range(1, taps):
            acc = acc + w_conv_ref[k:k + 1, :] * p_s[pl.ds(HALO - pad_left + k, tm), :]
        return acc

    carry = scan_part(0, c_s[...]) if pipelined else None
    z_ref[0] = short_conv(2 * W_A + W_B, D_IN, cwh_ref, cbh_ref, HY_SHORT, HY_SHORT // 2, ph_s)
    carry = scan_part(1, carry) if pipelined else None
    ua = short_conv(0, W_A, cwa_ref, cba_ref, RG_CONV, RG_CONV // 2, pa_s)
    ua_ref[0] = ua
    carry = scan_part(2, carry) if pipelined else None
    rest = proj(W_A, 2 * W_A + W_B)
    ga_ref[0] = rest[HALO:HALO + tm, :W_A]
    up_ref[0] = rest[HALO:HALO + tm, W_A:]
    if pipelined:
        carry = scan_part(3, carry)
        c_s[...] = jnp.where(i > 0, carry, c_s[...])
        uprev_s[...] = ua
    else:
        uprev_s[...] = ua
        carry = c_s[...]
        for j in range(parts):
            carry = scan_part(j, carry)


def _proj(x, mod, g, w_in, cwa, cba, cwh, cbh, wg, bg, lam, h0, *, tm, layer):
    bn, n, _ = x.shape
    n_tiles = n // tm
    hb = tm // HALO
    last_hb = n // HALO - 1
    outs = (W_A, W_A, W_B, 3 * W_C)
    cur = lambda i: jnp.minimum(i, n_tiles - 1)
    const = lambda shape: pl.BlockSpec(shape, lambda b, i: (0,) * len(shape))
    return pl.pallas_call(
        functools.partial(_proj_kernel, tm=tm, n_tiles=n_tiles),
        grid=(bn, n_tiles + 1 if n_tiles > 1 else 1),
        in_specs=[pl.BlockSpec((1, HALO, D_MODEL), lambda b, i: (b, jnp.maximum(cur(i) * hb - 1, 0), 0)),
                  pl.BlockSpec((1, tm, D_MODEL), lambda b, i: (b, cur(i), 0)),
                  pl.BlockSpec((1, HALO, D_MODEL), lambda b, i: (b, jnp.minimum((cur(i) + 1) * hb, last_hb), 0)),
                  pl.BlockSpec((1, N_MOD, D_MODEL), lambda b, i: (b, 0, 0)),
                  const((1, D_MODEL)),
                  _resident((None, D_MODEL, D_IN), lambda b, i: (layer, 0, 0)),
                  const((RG_CONV, W_A)), const((1, W_A)), const((HY_SHORT, 3 * W_C)), const((1, 3 * W_C))]
                 + _lru_param_specs(),
        out_specs=[pl.BlockSpec((1, tm, w), lambda b, i: (b, cur(i), 0)) for w in outs]
                  + [pl.BlockSpec((1, tm, W_A), lambda b, i: (b, jnp.maximum(i - 1, 0), 0))],
        out_shape=[jax.ShapeDtypeStruct((bn, n, w), F32) for w in outs + (W_A,)],
        scratch_shapes=[pltpu.VMEM((tm + 2 * HALO, W_A), F32), pltpu.VMEM((tm + 2 * HALO, 3 * W_C), F32)]
                       + _lru_scratch(tm) + [pltpu.VMEM((tm, W_A), F32)],
        compiler_params=_params("arbitrary", "arbitrary"),
        name="proj_scan",
    )(x, x, x, mod, g.reshape(1, D_MODEL), w_in, cwa, cba.reshape(1, W_A), cwh, cbh.reshape(1, 3 * W_C),
      wg, bg, lam, h0.reshape(bn, 1, W_A))


def _lru_coeffs(u, wg_ref, bg_ref, lam_ref, a_s, b_s, rows=slice(None)):
    ub = u.astype(BF16)
    blk = V7X_MXU_DIM

    def gate_block(c):
        r0 = (c * blk) % W_A
        return jnp.dot(ub[:, r0:r0 + blk], wg_ref[r0:r0 + blk, c * blk:(c + 1) * blk], preferred_element_type=F32)

    half = jnp.concatenate([gate_block(c) for c in range(2 * W_A // blk)], axis=1) + bg_ref[...]
    q = (-0.25 * RG_C) * jax.nn.softplus(-lam_ref[...])
    t = jnp.tanh(q * jnp.tanh(half[:, :W_A]) + q)
    inv = 1.0 / (1.0 - t)
    a_s[rows, :] = (1.0 + t) * inv
    b_s[rows, :] = (jnp.sqrt(-t) * inv) * ((jnp.tanh(half[:, W_A:]) + 1.0) * u)


def _lru_scan_tile(a_s, b_s, carry, *, tm, reverse, inline=None):
    groups = tm // V7X_SUBLANES
    rowi = lax.broadcasted_iota(jnp.int32, (V7X_SUBLANES, W_A), 0)

    def body(k, c):
        gi = (groups - 1 - k) if reverse else k
        r0 = gi * V7X_SUBLANES if inline is not None else pl.multiple_of(gi * V7X_SUBLANES, V7X_SUBLANES)
        a = a_s[pl.ds(r0, V7X_SUBLANES), :]
        b = b_s[pl.ds(r0, V7X_SUBLANES), :]
        first = rowi == (V7X_SUBLANES - 1 if reverse else 0)
        b = jnp.where(first, a * c + b, b)
        for s in (1, 2, 4):
            if reverse:
                shift, m = V7X_SUBLANES - s, rowi < V7X_SUBLANES - s
            else:
                shift, m = s, rowi >= s
            b = jnp.where(m, a * pltpu.roll(b, shift, 0) + b, b)
            if s < V7X_SUBLANES // 2:
                a = jnp.where(m, a * pltpu.roll(a, shift, 0), a)
        b_s[pl.ds(r0, V7X_SUBLANES), :] = b
        return b[0:1, :] if reverse else b[V7X_SUBLANES - 1:V7X_SUBLANES, :]

    if inline is not None:
        for k in inline:
            carry = body(k, carry)
        return carry
    return lax.fori_loop(0, groups, body, carry, unroll=4)


def _lru_scan_kernel(u_ref, wg_ref, bg_ref, lam_ref, h0_ref, h_ref, a_s, b_s, c_s, *, tm, reverse):
    @pl.when(pl.program_id(1) == 0)
    def _():
        c_s[...] = h0_ref[0]

    _lru_coeffs(u_ref[0], wg_ref, bg_ref, lam_ref, a_s, b_s)
    c_s[...] = _lru_scan_tile(a_s, b_s, c_s[...], tm=tm, reverse=reverse)
    h_ref[0] = b_s[...]


def _lru_param_specs():
    const = lambda shape: pl.BlockSpec(shape, lambda b, i: (0,) * len(shape))
    return [const((W_A, 2 * W_A)), const((1, 2 * W_A)), const((1, W_A)),
            pl.BlockSpec((1, 1, W_A), lambda b, i: (b, 0, 0))]


def _lru_scratch(tm):
    return [pltpu.VMEM((tm, W_A), F32), pltpu.VMEM((tm, W_A), F32), pltpu.VMEM((1, W_A), F32)]


def _lru_scan(u, wg, bg, lam, h0, *, tm, reverse):
    bn, n, _ = u.shape
    last = n // tm - 1
    tile_map = (lambda b, i: (b, last - i, 0)) if reverse else (lambda b, i: (b, i, 0))
    return pl.pallas_call(
        functools.partial(_lru_scan_kernel, tm=tm, reverse=reverse),
        grid=(bn, n // tm),
        in_specs=[pl.BlockSpec((1, tm, W_A), tile_map)] + _lru_param_specs(),
        out_specs=pl.BlockSpec((1, tm, W_A), tile_map),
        out_shape=jax.ShapeDtypeStruct(u.shape, F32),
        scratch_shapes=_lru_scratch(tm),
        compiler_params=_params("arbitrary", "arbitrary"),
        name="lru_scan",
    )(u, wg, bg, lam, h0.reshape(bn, 1, W_A))


def _pool_tables(row_len):
    t = np.arange(POOL_TILE)
    p = t % row_len
    base = t - p
    band = np.zeros((len(POOL_WINDOWS), POOL_TILE, POOL_TILE), np.float32)
    inv = np.zeros((POOL_TILE, W_B), np.float32)
    for g, win in enumerate(POOL_WINDOWS):
        lo = np.clip(p - win // 2, 0, row_len)
        hi = np.clip(p + win - win // 2, 0, row_len)
        s = np.arange(POOL_TILE)[None, :]
        band[g] = ((s >= (base + lo)[:, None]) & (s < (base + hi)[:, None])).astype(np.float32)
        inv[:, g * POOL_GW:(g + 1) * POOL_GW] = (1.0 / (hi - lo).astype(np.float64))[:, None]
    return _bf16_table(band), jnp.asarray(inv, F32)


def _filter_kernel(f_ref, w1_ref, b1_ref, fr_ref, w2_ref, b2_ref, w3_ref, dl_ref, o_ref):
    feats = f_ref[...]
    fr = fr_ref[...]
    hid = jnp.sin(fr * (_dot_bf16x3(feats, w1_ref[...]) + b1_ref[...]))
    hid = jnp.sin(fr * (_dot_bf16x3(hid, w2_ref[...]) + b2_ref[...]))
    filt = _dot_bf16x3(hid, w3_ref[...])
    filt = filt * jnp.exp(-feats[:, 0:1] * dl_ref[...])
    o_ref[...] = filt

    @pl.when(pl.program_id(0) == 0)
    def _():
        head = filt[0:V7X_SUBLANES, :]
        row = lax.broadcasted_iota(jnp.int32, head.shape, 0)
        bwd_col = (lax.broadcasted_iota(jnp.int32, head.shape, 1) // W_C) % 2 == 1
        o_ref[0:V7X_SUBLANES, :] = jnp.where((row == 0) & bwd_col, 0.0, head)


def _filter_features(n):
    t = jnp.linspace(0.0, 1.0, n, dtype=F32)[:, None]
    wpos = 2.0 * math.pi * jnp.arange(n, dtype=F32)[:, None] / n
    bands = jnp.linspace(1e-4, HY_BANDS - 1, HY_BANDS, dtype=F32)[None, :]
    feats = jnp.concatenate([t, jnp.cos(bands * wpos), -jnp.sin(bands * wpos)], axis=-1)
    return jnp.pad(feats, ((0, 0), (0, HY_EMB_PAD - HY_EMB)))


def _filter_decay_rates():
    deltas = jnp.abs(jnp.linspace(math.log(HY_TARGET) / HY_FAST, math.log(HY_TARGET) / HY_SLOW, W_C, dtype=F32))
    return jnp.tile(deltas, 2 * HY_ORDER).reshape(1, 2 * HY_ORDER * W_C)


def _filters(n, w1, b1, freq, w2, b2, w3):
    tm = min(n, FILTER_TILE)
    nf = 2 * HY_ORDER * W_C
    const = lambda shape: pl.BlockSpec(shape, lambda i: (0, 0))
    return pl.pallas_call(
        _filter_kernel,
        grid=(n // tm,),
        in_specs=[pl.BlockSpec((tm, HY_EMB_PAD), lambda i: (i, 0)),
                  const((HY_EMB_PAD, HY_FH)), const((1, HY_FH)), const((1, HY_FH)),
                  const((HY_FH, HY_FH)), const((1, HY_FH)), const((HY_FH, nf)), const((1, nf))],
        out_specs=pl.BlockSpec((tm, nf), lambda i: (i, 0)),
        out_shape=jax.ShapeDtypeStruct((n, nf), F32),
        compiler_params=_params("arbitrary"),
        name="hyena_filter",
    )(_filter_features(n), jnp.pad(w1, ((0, HY_EMB_PAD - HY_EMB), (0, 0))), b1.reshape(1, HY_FH),
      freq.reshape(1, HY_FH), w2, b2.reshape(1, HY_FH), w3, _filter_decay_rates())


T2_BLOCKS = FFT_MINOR // V7X_SUBLANES


class _FftPlan:
    def __init__(self, n):
        assert n % (2 * FFT_MINOR) == 0
        self.n_fft = 2 * n
        self.n1 = self.n_fft // FFT_MINOR
        self.t1 = self.n1 // 2
        self.k1 = self.n1 // 2 + 1
        self.k_block = max(d for d in range(1, STAGE_B_MAX_SLABS + 1) if self.k1 % d == 0)
        t1 = np.arange(self.t1)
        k1 = np.arange(self.k1)
        eye = np.eye(V7X_SUBLANES)
        ang = 2.0 * np.pi * ((k1[:, None] * t1[None, :]) % self.n1) / self.n1
        fa = np.concatenate([np.cos(ang), -np.sin(ang)], axis=0)
        self.stage_a = _bf16_table(np.kron(fa, eye))
        wgt = np.where((k1 == 0) | (k1 == self.n1 // 2), 1.0, 2.0) / self.n_fft
        fc = np.concatenate([np.cos(ang) * wgt[:, None], -np.sin(ang) * wgt[:, None]], axis=0).T
        self.stage_a_inv = _bf16_table(np.kron(fc, eye))
        t2 = np.arange(FFT_MINOR)
        k2 = np.arange(FFT_MINOR)
        idx = (t2[None, None, :] * (k1[:, None, None] + self.n1 * k2[None, :, None])) % self.n_fft
        ang_b = 2.0 * np.pi * idx / self.n_fft
        g = np.stack([np.cos(ang_b), -np.sin(ang_b)], axis=1)
        self.stage_b = _bf16_table(g)
        self.stage_b_inv = _bf16_table(np.transpose(g, (0, 1, 3, 2)))

    def scratch(self, signals):
        return pltpu.VMEM((signals, 2, self.k1, T2_BLOCKS, V7X_SUBLANES, V7X_LANES), F32)


def _time_tiles(ref, tb, rows):
    return ref[:, pl.ds(tb, 1)].reshape(rows, V7X_LANES)


def _stage_a_forward(plan, x_refs, fa_ref, a_s):
    def body(tb, carry):
        xs = jnp.concatenate([_time_tiles(xr, tb, plan.t1 * V7X_SUBLANES) for xr in x_refs], axis=1)
        av = jnp.dot(fa_ref[...], xs.astype(BF16), preferred_element_type=F32)
        for j in range(len(x_refs)):
            part = av[:, j * V7X_LANES:(j + 1) * V7X_LANES]
            a_s[j, :, :, pl.ds(tb, 1)] = part.reshape(2, plan.k1, 1, V7X_SUBLANES, V7X_LANES)
        return carry
    lax.fori_loop(0, T2_BLOCKS, body, 0, unroll=DFT_LOOP_UNROLL)


def _stage_b_matrix(g_ref, i, inverse):
    gr, gi = g_ref[i, 0], g_ref[i, 1]
    if inverse:
        gi = -gi
    return jnp.concatenate([jnp.concatenate([gr, -gi], axis=1), jnp.concatenate([gi, gr], axis=1)], axis=0)


def _stage_b_slab(a_s, k, count):
    return jnp.concatenate(
        [a_s[j, :, pl.ds(k, 1)].reshape(2 * FFT_MINOR, V7X_LANES) for j in range(count)], axis=1)


def _spectrum_kernel(x0_ref, x1_ref, fa_ref, fb_ref, o_ref, a_s, *, plan):
    kb = pl.program_id(1)

    @pl.when(kb == 0)
    def _():
        _stage_a_forward(plan, (x0_ref, x1_ref), fa_ref, a_s)

    half = FFT_MINOR
    rows = 2 * half
    for i in range(plan.k_block):
        slab = _stage_b_slab(a_s, kb * plan.k_block + i, 2).astype(BF16)
        xs = jnp.dot(_stage_b_matrix(fb_ref, i, False), slab, preferred_element_type=F32)
        o_ref[i * rows:i * rows + half, :] = xs[:half, :V7X_LANES] + xs[:half, V7X_LANES:]
        o_ref[i * rows + half:(i + 1) * rows, :] = xs[half:, :V7X_LANES] - xs[half:, V7X_LANES:]


def _filter_spectrum(plan, filt):
    n, nf = filt.shape
    rows = 2 * FFT_MINOR
    kbs = plan.k_block
    n_cb = W_C // V7X_LANES
    filt = filt.reshape(plan.t1, T2_BLOCKS, V7X_SUBLANES, nf)
    sig = (plan.t1, T2_BLOCKS, V7X_SUBLANES, V7X_LANES)
    return pl.pallas_call(
        functools.partial(_spectrum_kernel, plan=plan),
        grid=(HY_ORDER * n_cb, plan.k1 // kbs),
        in_specs=[pl.BlockSpec(sig, lambda c, k: (0, 0, 0, 2 * (c // n_cb) * n_cb + c % n_cb)),
                  pl.BlockSpec(sig, lambda c, k: (0, 0, 0, (2 * (c // n_cb) + 1) * n_cb + c % n_cb)),
                  pl.BlockSpec(plan.stage_a.shape, lambda c, k: (0, 0)),
                  pl.BlockSpec((kbs, 2, FFT_MINOR, FFT_MINOR), lambda c, k: (k, 0, 0, 0))],
        out_specs=pl.BlockSpec((kbs * rows, V7X_LANES), lambda c, k: (k, c)),
        out_shape=jax.ShapeDtypeStruct((plan.k1 * rows, HY_ORDER * W_C), F32),
        scratch_shapes=[plan.scratch(2)],
        compiler_params=_params("arbitrary", "arbitrary"),
        name="filter_spectrum",
    )(filt, filt, plan.stage_a, plan.stage_b)


def _conv_kernel(v_ref, g_ref, h_ref, fa_ref, fb_ref, fbi_ref, fai_ref, sk_ref, o_ref, a_s, *, plan, bn):
    kb = pl.program_id(1)
    half = FFT_MINOR
    rows = 2 * half

    @pl.when(kb == 0)
    def _():
        _stage_a_forward(plan, [v_ref.at[b] for b in range(bn)], fa_ref, a_s)

    for i in range(plan.k_block):
        k = kb * plan.k_block + i
        xs = jnp.dot(_stage_b_matrix(fb_ref, i, False), _stage_b_slab(a_s, k, bn).astype(BF16),
                     preferred_element_type=F32)
        xr, xi = xs[:half], xs[half:]
        hr = jnp.concatenate([h_ref[i * rows:i * rows + half, :]] * bn, axis=1)
        hi = jnp.concatenate([h_ref[i * rows + half:(i + 1) * rows, :]] * bn, axis=1)
        ys = jnp.concatenate([xr * hr - xi * hi, xr * hi + xi * hr], axis=0)
        ds = jnp.dot(_stage_b_matrix(fbi_ref, i, True), ys.astype(BF16), preferred_element_type=F32)
        for b in range(bn):
            part = ds[:, b * V7X_LANES:(b + 1) * V7X_LANES]
            a_s[b, :, pl.ds(k, 1)] = part.reshape(2, 1, T2_BLOCKS, V7X_SUBLANES, V7X_LANES)

    @pl.when(kb == pl.num_programs(1) - 1)
    def _():
        t_rows = plan.t1 * V7X_SUBLANES

        def body(tb, carry):
            ds = jnp.concatenate(
                [a_s[b, :, :, pl.ds(tb, 1)].reshape(2 * plan.k1 * V7X_SUBLANES, V7X_LANES) for b in range(bn)],
                axis=1)
            y = jnp.dot(fai_ref[...], ds.astype(BF16), preferred_element_type=F32)
            for b in range(bn):
                vv = _time_tiles(v_ref.at[b], tb, t_rows)
                gg = _time_tiles(g_ref.at[b], tb, t_rows)
                res = gg * (y[:, b * V7X_LANES:(b + 1) * V7X_LANES] + sk_ref[...] * vv)
                o_ref[b, :, pl.ds(tb, 1)] = res.reshape(plan.t1, 1, V7X_SUBLANES, V7X_LANES)
            return carry
        lax.fori_loop(0, T2_BLOCKS, body, 0, unroll=DFT_LOOP_UNROLL)


def _hyena_conv(plan, v_arr, v_col, g_arr, g_col, spec, order, skip):
    bn = v_arr.shape[0]
    cb = V7X_LANES
    rows = 2 * FFT_MINOR
    n_cb = W_C // cb
    kbs = plan.k_block
    sig = (bn, plan.t1, T2_BLOCKS, V7X_SUBLANES, cb)
    return pl.pallas_call(
        functools.partial(_conv_kernel, plan=plan, bn=bn),
        grid=(n_cb, plan.k1 // kbs),
        in_specs=[_resident(sig, lambda c, k: (0, 0, 0, 0, v_col + c)),
                  _resident(sig, lambda c, k: (0, 0, 0, 0, g_col + c)),
                  pl.BlockSpec((kbs * rows, cb), lambda c, k: (k, order * n_cb + c)),
                  pl.BlockSpec(plan.stage_a.shape, lambda c, k: (0, 0)),
                  pl.BlockSpec((kbs, 2, FFT_MINOR, FFT_MINOR), lambda c, k: (k, 0, 0, 0)),
                  pl.BlockSpec((kbs, 2, FFT_MINOR, FFT_MINOR), lambda c, k: (k, 0, 0, 0)),
                  pl.BlockSpec(plan.stage_a_inv.shape, lambda c, k: (0, 0)),
                  pl.BlockSpec((1, cb), lambda c, k: (0, c))],
        out_specs=_resident(sig, lambda c, k: (0, 0, 0, 0, c)),
        out_shape=jax.ShapeDtypeStruct((bn, plan.t1, T2_BLOCKS, V7X_SUBLANES, W_C), F32),
        scratch_shapes=[plan.scratch(bn)],
        compiler_params=_params("arbitrary", "arbitrary"),
        name="hyena_conv",
    )(v_arr, g_arr, spec, plan.stage_a, plan.stage_b, plan.stage_b_inv, plan.stage_a_inv,
      skip.reshape(1, W_C))


def _hyena(z, n_fft_len, filt_args, skip):
    bn, n, _ = z.shape
    plan = _FftPlan(n_fft_len)
    filt = _filters(n, *filt_args)
    if n_fft_len != n:
        filt = jnp.pad(filt, ((0, n_fft_len - n), (0, 0)))
        z = jnp.pad(z, ((0, 0), (0, n_fft_len - n), (0, 0)))
    spec = _filter_spectrum(plan, filt)
    n_cb = W_C // V7X_LANES
    z = z.reshape(bn, plan.t1, T2_BLOCKS, V7X_SUBLANES, 3 * W_C)
    y1 = _hyena_conv(plan, z, 0, z, n_cb, spec, 0, skip[0])
    y2 = _hyena_conv(plan, y1, 0, z, 2 * n_cb, spec, 1, skip[1])
    return y2.reshape(bn, n_fft_len, W_C)[:, :n]


def _dense_block_diag(w):
    h, c, d = w.shape[-3:]
    tiled = jnp.tile(w.reshape(w.shape[:-3] + (h * c, d)), (1,) * (w.ndim - 2) + (h,))
    on_diagonal = (jnp.arange(h * c)[:, None] // c) == (jnp.arange(h * d)[None, :] // d)
    return jnp.where(on_diagonal, tiled, jnp.zeros_like(tiled))


def _tile_rows(n):
    return min(n, 512)


def kernel(x, c, ctx, c_ctx, w_mod, b_mod, norm_g, ffn_w_in, ffn_w_out, w_in, w_out, lru_conv_w, lru_conv_b,
           lru_gate_w, lru_gate_b, lru_lambda, pool_w, pool_b, pool_scale, hy_conv_w, hy_conv_b, hy_w1, hy_b1,
           hy_freq, hy_w2, hy_b2, hy_w3, hy_skip, final_g):
    bn, n, _ = x.shape
    n_ctx = ctx.shape[1]
    depth = w_mod.shape[0]
    assert bn + 1 <= V7X_SUBLANES and n % GRID_W == 0 and n % POOL_TILE == 0 and n_ctx == POOL_TILE
    assert RG_HEADS * RG_HD == W_A and V7X_MXU_DIM % RG_HD == 0
    tm_x, tm_c = _tile_rows(n), _tile_rows(n_ctx)

    cvec = jnp.zeros((V7X_SUBLANES, D_MODEL), F32).at[:bn].set(c).at[bn].set(c_ctx)
    mods = _modulation(cvec, w_mod, b_mod)
    band_x, inv_x = _pool_tables(GRID_W)
    band_c, inv_c = _pool_tables(n_ctx)
    zeros = jnp.zeros((bn, W_A), F32)
    f_in = ffn_w_in.astype(BF16)
    f_out = ffn_w_out.astype(BF16)
    w_in_b = w_in.astype(BF16)
    w_out_b = w_out.astype(BF16)

    xc = ctx
    for l in range(depth):
        last = l == depth - 1
        mx = mods[l, :bn].reshape(bn, N_MOD, D_MODEL)
        mc = jnp.broadcast_to(mods[l, bn].reshape(1, N_MOD, D_MODEL), (bn, N_MOD, D_MODEL))
        gate_dense = _dense_block_diag(lru_gate_w[l])
        wg = [(0.5 * jnp.concatenate([gate_dense[d, 0], gate_dense[d, 1]], axis=-1)).astype(BF16) for d in range(2)]
        bg = [0.5 * lru_gate_b[l, d].reshape(1, 2 * W_A) for d in range(2)]
        lam = [lru_lambda[l, d].reshape(1, W_A) for d in range(2)]
        pw = _dense_block_diag(pool_w[l]).astype(BF16)
        filt_args = (hy_w1[l], hy_b1[l], hy_freq[l], hy_w2[l], hy_b2[l], hy_w3[l])
        proj_args = (w_in_b, lru_conv_w[l], lru_conv_b[l], hy_conv_w[l], hy_conv_b[l])

        def mixer(xs, mod, tm, h0f, band, inv, fft_len):
            ua, ga, up, z, hf = _proj(xs, mod, norm_g[l, 1], *proj_args, wg[0], bg[0], lam[0], h0f, tm=tm, layer=l)
            yc = _hyena(z, fft_len, filt_args, hy_skip[l])
            return lambda h0b: (ua, wg[1], bg[1], lam[1], h0b, hf, ga, up, yc, band, inv, pw, pool_b[l],
                                pool_scale[l], w_out_b), hf[:, -1]

        x = _ffn(x, mx, norm_g[l, 0], f_in, f_out, (l, 0), final_g, j0=0, final=False, tm=min(n, FFN_TILE))
        xc = _ffn(xc, mc, norm_g[l, 0], f_in, f_out, (l, 0), final_g, j0=0, final=False, tm=tm_c)
        if last:
            ua_c, _, _, _, hf_c = _proj(xc, mc, norm_g[l, 1], *proj_args, wg[0], bg[0], lam[0], zeros, tm=tm_c, layer=l)
            h0f = hf_c[:, -1]
            h0b = _lru_scan(ua_c, wg[1], bg[1], lam[1], zeros, tm=tm_c, reverse=True)[:, 0]
        else:
            mix_c, h0f = mixer(xc, mc, tm_c, zeros, band_c, inv_c, n_ctx)
            xc, h0b = _ffn(xc, mc, norm_g[l, 2], f_in, f_out, (l, 1), final_g, mix_c(zeros), j0=6, final=False,
                           tm=tm_c)
        mix_x, _ = mixer(x, mx, tm_x, h0f, band_x, inv_x, n)
        x, _ = _ffn(x, mx, norm_g[l, 2], f_in, f_out, (l, 1), final_g, mix_x(h0b), j0=6, final=last, tm=tm_x)
    return x
```
